```python
import jax, jax.numpy as jnp
from jax import lax
import numpy as np

D_MODEL = 1024
BATCH = 1
SEQ = 16384
DEPTH = 2
DEC_BATCH = 32
DEC_SEQ = 8
PAST_LEN = 16384
PAGE_SIZE = 128

A_WIDTH = 512
A_GROUPS = 8
A_GDIM = A_WIDTH // A_GROUPS
CHUNK = 128
B_HEADS = 8
HEAD_DIM = 64
B_WIDTH = B_HEADS * HEAD_DIM
MOBA_BLOCK = 256
MOBA_TOPK = 3
QUERY_BLOCK = 128
ROPE_THETA = 10000.0
EVEN_IN = 2 * A_WIDTH + 3 * B_WIDTH
C_WIDTH = 512
CONV_WIDTH = 31
D_WIDTH = 512
POOL_WINDOWS = (2, 4, 8, 16)
POOL_GDIM = D_WIDTH // len(POOL_WINDOWS)
POOL_MAX = 16
ODD_IN = 2 * C_WIDTH + D_WIDTH
MIX_WIDTH = 1024
D_FF = 2816
N_EXPERTS = 8
TOP_K = 2
D_FF_EXPERT = 3584
MOE_BLOCK = 128
PLE_DIM = 256
EPS = 1e-6
NEG = -1e30
N_EVEN = (DEPTH + 1) // 2
N_ODD = DEPTH // 2
F32 = jnp.float32

kernel_name = 'hybrid_gmlp_moba_conv_pool_decoder_step'


def rmsnorm(x, g):
    xf = x.astype(F32)
    y = xf * lax.rsqrt(jnp.mean(xf * xf, axis=-1, keepdims=True) + EPS) * g.astype(F32)
    return y.astype(x.dtype)


def layernorm(x, g, b):
    xf = x.astype(F32)
    xc = xf - jnp.mean(xf, axis=-1, keepdims=True)
    var = jnp.mean(xc * xc, axis=-1, keepdims=True)
    return (xc * lax.rsqrt(var + EPS) * g.astype(F32) + b.astype(F32)).astype(x.dtype)


def rope(x, pos):
    half = HEAD_DIM // 2
    inv = ROPE_THETA ** (-jnp.arange(half, dtype=F32) * (2.0 / HEAD_DIM))
    ang = pos.astype(F32)[:, None] * inv[None, :]
    cos = jnp.cos(ang)[:, None, :]
    sin = jnp.sin(ang)[:, None, :]
    xf = x.astype(F32)
    x1, x2 = xf[..., :half], xf[..., half:]
    return jnp.concatenate([x1 * cos - x2 * sin, x2 * cos + x1 * sin], axis=-1).astype(x.dtype)


def swiglu(x, w1, w3, w2):
    return (jax.nn.silu(x @ w1) * (x @ w3)) @ w2


def chunk_spatial_gate(v, w_s, b_s):
    b_, l_ = v.shape[0], v.shape[1]
    n_c = -(-l_ // CHUNK)
    vp = jnp.pad(v, ((0, 0), (0, n_c * CHUNK - l_), (0, 0), (0, 0)))
    vp = vp.reshape(b_, n_c, CHUNK, A_GROUPS, A_GDIM)
    w = w_s * jnp.tril(jnp.ones((CHUNK, CHUNK), w_s.dtype))
    out = jnp.einsum('gts,bcsgd->bctgd', w, vp) + jnp.transpose(b_s)[None, None, :, :, None]
    return out.reshape(b_, n_c * CHUNK, A_GROUPS, A_GDIM)[:, :l_]


def gmlp_and_qkv(hn, pos, w_in, a_ln_g, a_ln_b, a_ws, a_bs):
    b_, l_ = hn.shape[0], hn.shape[1]
    z = hn @ w_in
    za, zq, zk, zv = jnp.split(z, [2 * A_WIDTH, 2 * A_WIDTH + B_WIDTH, 2 * A_WIDTH + 2 * B_WIDTH], axis=-1)
    za = jax.nn.gelu(za)
    u = za[..., :A_WIDTH]
    va = layernorm(za[..., A_WIDTH:], a_ln_g, a_ln_b)
    gate = chunk_spatial_gate(va.reshape(b_, l_, A_GROUPS, A_GDIM), a_ws, a_bs).reshape(b_, l_, A_WIDTH)
    a_out = u * gate
    q = rope(zq.reshape(b_, l_, B_HEADS, HEAD_DIM), pos)
    k = rope(zk.reshape(b_, l_, B_HEADS, HEAD_DIM), pos)
    v = zv.reshape(b_, l_, B_HEADS, HEAD_DIM)
    return a_out, va, q, k, v


def moba_softmax(q, k_own, v_own, own_ok, k_sel=None, v_sel=None, sel_ok=None):
    qf = q.astype(F32) * (HEAD_DIM ** -0.5)
    s_own = jnp.einsum('bqhd,bkhd->bqhk', qf, k_own.astype(F32))
    s_own = jnp.where(own_ok[None, :, None, :], s_own, NEG)
    if k_sel is None:
        p = jax.nn.softmax(s_own, axis=-1)
        out = jnp.einsum('bqhk,bkhd->bqhd', p, v_own.astype(F32))
    else:
        s_sel = jnp.einsum('bqhd,bqhnd->bqhn', qf, k_sel.astype(F32))
        if sel_ok is not None:
            s_sel = jnp.where(sel_ok, s_sel, NEG)
        n = s_sel.shape[-1]
        p = jax.nn.softmax(jnp.concatenate([s_sel, s_own], axis=-1), axis=-1)
        out = (jnp.einsum('bqhn,bqhnd->bqhd', p[..., :n], v_sel.astype(F32))
               + jnp.einsum('bqhk,bkhd->bqhd', p[..., n:], v_own.astype(F32)))
    return out.reshape(q.shape[0], q.shape[1], -1).astype(q.dtype)


def moba_prompt(q, k, v):
    b_, s_, h_, d_ = q.shape
    n_cb = s_ // MOBA_BLOCK
    n_kb = -(-s_ // MOBA_BLOCK)
    top = min(MOBA_TOPK, (s_ - 1) // MOBA_BLOCK)
    pad = n_kb * MOBA_BLOCK - s_
    kp = jnp.pad(k, ((0, 0), (0, pad), (0, 0), (0, 0)))
    vp = jnp.pad(v, ((0, 0), (0, pad), (0, 0), (0, 0)))
    if top > 0:
        kb = k[:, :n_cb * MOBA_BLOCK].reshape(b_, n_cb, MOBA_BLOCK, h_, d_)
        vb = v[:, :n_cb * MOBA_BLOCK].reshape(b_, n_cb, MOBA_BLOCK, h_, d_)
        kmean = jnp.mean(kb.astype(F32), axis=2)
        kb = jnp.transpose(kb, (0, 3, 1, 2, 4))
        vb = jnp.transpose(vb, (0, 3, 1, 2, 4))
    bi = jnp.arange(b_)[:, None, None, None]
    hi = jnp.arange(h_)[None, None, :, None]
    qq = jnp.arange(QUERY_BLOCK)
    kk = jnp.arange(MOBA_BLOCK)

    def one_block(qb):
        q0 = qb * QUERY_BLOCK
        ob = q0 // MOBA_BLOCK
        qblk = lax.dynamic_slice_in_dim(q, q0, QUERY_BLOCK, axis=1)
        k_own = lax.dynamic_slice_in_dim(kp, ob * MOBA_BLOCK, MOBA_BLOCK, axis=1)
        v_own = lax.dynamic_slice_in_dim(vp, ob * MOBA_BLOCK, MOBA_BLOCK, axis=1)
        own_ok = (ob * MOBA_BLOCK + kk)[None, :] <= (q0 + qq)[:, None]
        if top == 0:
            return moba_softmax(qblk, k_own, v_own, own_ok)
        sc = jnp.einsum('bqhd,bnhd->bqhn', qblk.astype(F32), kmean)
        sc = jnp.where(jnp.arange(n_cb) < ob, sc, NEG)
        _, idx = lax.top_k(sc, top)
        sel_ok = jnp.repeat(idx < ob, MOBA_BLOCK, axis=-1)
        k_sel = kb[bi, hi, idx].reshape(b_, QUERY_BLOCK, h_, top * MOBA_BLOCK, d_)
        v_sel = vb[bi, hi, idx].reshape(b_, QUERY_BLOCK, h_, top * MOBA_BLOCK, d_)
        return moba_softmax(qblk, k_own, v_own, own_ok, k_sel, v_sel, sel_ok)

    out = lax.map(one_block, jnp.arange(s_ // QUERY_BLOCK))
    return jnp.transpose(out, (1, 0, 2, 3)).reshape(b_, s_, h_ * d_)


def moba_sample(q, k, v, cache_k, cache_v, page_table, li):
    b_, t_, h_, d_ = q.shape
    ppb = MOBA_BLOCK // PAGE_SIZE
    n_full = PAST_LEN // MOBA_BLOCK
    tail = PAST_LEN - n_full * MOBA_BLOCK
    top = min(MOBA_TOPK, n_full)
    if tail > 0:
        tp = page_table[:, n_full * ppb: n_full * ppb + tail // PAGE_SIZE]
        k_own = jnp.concatenate([cache_k[li, tp].reshape(b_, tail, h_, d_), k], axis=1)
        v_own = jnp.concatenate([cache_v[li, tp].reshape(b_, tail, h_, d_), v], axis=1)
    else:
        k_own, v_own = k, v
    own_ok = jnp.arange(tail + t_)[None, :] <= (tail + jnp.arange(t_))[:, None]
    if top == 0:
        return moba_softmax(q, k_own, v_own, own_ok)
    past = cache_k[li, page_table[:, :n_full * ppb]]
    kmean = jnp.mean(past.reshape(b_, n_full, MOBA_BLOCK, h_, d_).astype(F32), axis=2)
    sc = jnp.einsum('bqhd,bnhd->bqhn', q.astype(F32), kmean)
    _, idx = lax.top_k(sc, top)
    bi = jnp.arange(b_)[:, None, None, None, None]
    hi = jnp.arange(h_)[None, None, :, None, None]
    phys = page_table[bi, idx[..., None] * ppb + jnp.arange(ppb)]
    lidx = jnp.full(phys.shape, li, jnp.int32)
    k_sel = cache_k[lidx, phys, :, hi].reshape(b_, t_, h_, top * MOBA_BLOCK, d_)
    v_sel = cache_v[lidx, phys, :, hi].reshape(b_, t_, h_, top * MOBA_BLOCK, d_)
    return moba_softmax(q, k_own, v_own, own_ok, k_sel, v_sel)


def even_layer(h, pos, attend, norm_mix, w_in, a_ln_g, a_ln_b, a_ws, a_bs, w_out, norm_ffn, w1, w3, w2):
    a_out, va, q, k, v = gmlp_and_qkv(rmsnorm(h, norm_mix), pos, w_in, a_ln_g, a_ln_b, a_ws, a_bs)
    h = h + jnp.concatenate([a_out, attend(q, k, v)], axis=-1) @ w_out
    h = h + swiglu(rmsnorm(h, norm_ffn), w1, w3, w2)
    return h, k, v, va


def causal_dwconv(x, hist, w, b):
    xp = jnp.concatenate([hist.astype(x.dtype), x], axis=1)
    y = lax.conv_general_dilated(xp, w[:, None, :], window_strides=(1,), padding='VALID',
                                 dimension_numbers=('NWC', 'WIO', 'NWC'), feature_group_count=x.shape[-1])
    return y + b, xp[:, -(CONV_WIDTH - 1):]


def causal_multipool(x, hist, start):
    l_ = x.shape[1]
    xp = jnp.concatenate([hist.astype(x.dtype), x], axis=1)
    xf = xp.astype(F32)
    cs = jnp.concatenate([jnp.zeros_like(xf[:, :1]), jnp.cumsum(xf, axis=1)], axis=1)
    pos = start + jnp.arange(l_)
    outs = []
    for g, w in enumerate(POOL_WINDOWS):
        sl = slice(g * POOL_GDIM, (g + 1) * POOL_GDIM)
        win = cs[:, POOL_MAX:POOL_MAX + l_, sl] - cs[:, POOL_MAX - w:POOL_MAX - w + l_, sl]
        cnt = jnp.minimum(pos + 1, w).astype(F32)
        outs.append(win / cnt[None, :, None])
    pooled = jnp.concatenate(outs, axis=-1)
    return (pooled - x.astype(F32)).astype(x.dtype), xp[:, -(POOL_MAX - 1):]


def moe_swiglu(h, router_w, router_b, e_w1, e_w3, e_w2):
    shp = h.shape
    x = h.reshape(-1, shp[-1])
    t_ = x.shape[0]
    logits = x.astype(F32) @ router_w.astype(F32) + router_b.astype(F32)
    top_val, top_idx = lax.top_k(logits, TOP_K)
    gates = jax.nn.softmax(top_val, axis=-1)
    flat_e = top_idx.reshape(-1)
    flat_tok = jnp.repeat(jnp.arange(t_), TOP_K)
    flat_g = gates.reshape(-1)
    order = jnp.argsort(flat_e)
    se, stok, sg = flat_e[order], flat_tok[order], flat_g[order]
    counts = jnp.bincount(flat_e, length=N_EXPERTS)
    starts = jnp.cumsum(counts) - counts
    padded = (counts + MOE_BLOCK - 1) // MOE_BLOCK * MOE_BLOCK
    pends = jnp.cumsum(padded)
    pstarts = pends - padded
    dest = pstarts[se] + (jnp.arange(t_ * TOP_K) - starts[se])
    n_blocks = -(-(t_ * TOP_K) // MOE_BLOCK) + N_EXPERTS
    rows = n_blocks * MOE_BLOCK
    xbuf = jnp.zeros((rows, shp[-1]), x.dtype).at[dest].set(x[stok])
    blk_e = jnp.minimum(jnp.searchsorted(pends, jnp.arange(n_blocks) * MOE_BLOCK, side='right'), N_EXPERTS - 1)

    def expert_block(args):
        xb, e = args
        return swiglu(xb, e_w1[e], e_w3[e], e_w2[e])

    ybuf = lax.map(expert_block, (xbuf.reshape(n_blocks, MOE_BLOCK, shp[-1]), blk_e)).reshape(rows, shp[-1])
    contrib = ybuf[dest] * sg[:, None].astype(ybuf.dtype)
    return jax.ops.segment_sum(contrib, stok, num_segments=t_).reshape(shp)


def odd_layer(h, conv_hist, pool_hist, start, norm_mix, w_in, conv_w, conv_b, c_ln_g, c_ln_b,
              pool_w, pool_scale, w_out, norm_ffn, router_w, router_b, e_w1, e_w3, e_w2):
    b_, l_ = h.shape[0], h.shape[1]
    z = rmsnorm(h, norm_mix) @ w_in
    ca, cg, dx = jnp.split(z, [C_WIDTH, 2 * C_WIDTH], axis=-1)
    c_in = ca * jax.nn.sigmoid(cg)
    c, conv_state = causal_dwconv(c_in, conv_hist, conv_w, conv_b)
    c = jax.nn.silu(layernorm(c, c_ln_g, c_ln_b))
    d, pool_state = causal_multipool(dx, pool_hist, start)
    d = jnp.einsum('blgc,gce->blge', d.reshape(b_, l_, len(POOL_WINDOWS), POOL_GDIM), pool_w)
    d = d.reshape(b_, l_, D_WIDTH) * pool_scale
    h = h + jnp.concatenate([c, d], axis=-1) @ w_out
    h = h + moe_swiglu(rmsnorm(h, norm_ffn), router_w, router_b, e_w1, e_w3, e_w2)
    return h, conv_state, pool_state


def per_layer_embed(h, p, w_p, w_g, g):
    return h + (p @ w_p) * jax.nn.sigmoid(rmsnorm(h, g) @ w_g)


def setup_inputs(seed: int = 0) -> dict:
    key = jax.random.key(seed)
    keys = iter(jax.random.split(key, 48))

    def nrm(shape, scale=1.0):
        return jax.random.normal(next(keys), shape, F32) * scale

    def gain(shape):
        return 1.0 + nrm(shape, 0.05)

    n_pages = PAST_LEN // PAGE_SIZE
    n_used = DEC_BATCH * n_pages
    n_phys = n_used + max(1, n_used // 4)
    E, O = N_EVEN, N_ODD
    d = {}
    d['x_prompt'] = nrm((BATCH, SEQ, D_MODEL))
    d['x_sample'] = nrm((DEC_BATCH, DEC_SEQ, D_MODEL))
    d['cache_k'] = nrm((E, n_phys, PAGE_SIZE, B_HEADS, HEAD_DIM))
    d['cache_v'] = nrm((E, n_phys, PAGE_SIZE, B_HEADS, HEAD_DIM))
    d['state_conv'] = nrm((O, DEC_BATCH, CONV_WIDTH - 1, C_WIDTH), 0.5)
    d['state_pool'] = nrm((O, DEC_BATCH, POOL_MAX - 1, D_WIDTH))
    d['page_table'] = jax.random.permutation(next(keys), n_phys)[:n_used].reshape(DEC_BATCH, n_pages).astype(jnp.int32)
    d['p_prompt'] = nrm((DEPTH, BATCH, SEQ, PLE_DIM))
    d['p_sample'] = nrm((DEPTH, DEC_BATCH, DEC_SEQ, PLE_DIM))
    d['norm_mix_e'] = gain((E, D_MODEL))
    d['w_in_e'] = nrm((E, D_MODEL, EVEN_IN), D_MODEL ** -0.5)
    d['a_ln_g'] = gain((E, A_WIDTH))
    d['a_ln_b'] = nrm((E, A_WIDTH), 0.02)
    d['a_ws'] = nrm((E, A_GROUPS, CHUNK, CHUNK), CHUNK ** -0.5)
    d['a_bs'] = 1.0 + nrm((E, A_GROUPS, CHUNK), 0.1)
    d['w_out_e'] = nrm((E, MIX_WIDTH, D_MODEL), MIX_WIDTH ** -0.5)
    d['norm_ffn_e'] = gain((E, D_MODEL))
    d['ffn_w1'] = nrm((E, D_MODEL, D_FF), D_MODEL ** -0.5)
    d['ffn_w3'] = nrm((E, D_MODEL, D_FF), D_MODEL ** -0.5)
    d['ffn_w2'] = nrm((E, D_FF, D_MODEL), D_FF ** -0.5)
    d['norm_mix_o'] = gain((O, D_MODEL))
    d['w_in_o'] = nrm((O, D_MODEL, ODD_IN), D_MODEL ** -0.5)
    d['conv_w'] = nrm((O, CONV_WIDTH, C_WIDTH), CONV_WIDTH ** -0.5)
    d['conv_b'] = nrm((O, C_WIDTH), 0.02)
    d['c_ln_g'] = gain((O, C_WIDTH))
    d['c_ln_b'] = nrm((O, C_WIDTH), 0.02)
    d['pool_w'] = nrm((O, len(POOL_WINDOWS), POOL_GDIM, POOL_GDIM), POOL_GDIM ** -0.5)
    d['pool_scale'] = 1.0 + nrm((O, D_WIDTH), 0.1)
    d['w_out_o'] = nrm((O, MIX_WIDTH, D_MODEL), MIX_WIDTH ** -0.5)
    d['norm_ffn_o'] = gain((O, D_MODEL))
    d['router_w'] = nrm((O, D_MODEL, N_EXPERTS), D_MODEL ** -0.5)
    d['router_b'] = nrm((O, N_EXPERTS), 0.01)
    d['exp_w1'] = nrm((O, N_EXPERTS, D_MODEL, D_FF_EXPERT), D_MODEL ** -0.5)
    d['exp_w3'] = nrm((O, N_EXPERTS, D_MODEL, D_FF_EXPERT), D_MODEL ** -0.5)
    d['exp_w2'] = nrm((O, N_EXPERTS, D_FF_EXPERT, D_MODEL), D_FF_EXPERT ** -0.5)
    d['ple_w'] = nrm((DEPTH, PLE_DIM, D_MODEL), PLE_DIM ** -0.5)
    d['ple_gate_w'] = nrm((DEPTH, D_MODEL, D_MODEL), D_MODEL ** -0.5)
    d['ple_norm'] = gain((DEPTH, D_MODEL))
    d['final_norm'] = gain((D_MODEL,))
    return d


def reference(x_prompt, x_sample, cache_k, cache_v, state_conv, state_pool, page_table, p_prompt, p_sample,
              norm_mix_e, w_in_e, a_ln_g, a_ln_b, a_ws, a_bs, w_out_e, norm_ffn_e, ffn_w1, ffn_w3, ffn_w2,
              norm_mix_o, w_in_o, conv_w, conv_b, c_ln_g, c_ln_b, pool_w, pool_scale, w_out_o, norm_ffn_o,
              router_w, router_b, exp_w1, exp_w3, exp_w2, ple_w, ple_gate_w, ple_norm, final_norm):
    pos_p = jnp.arange(x_prompt.shape[1], dtype=jnp.int32)
    pos_s = PAST_LEN + jnp.arange(x_sample.shape[1], dtype=jnp.int32)
    hp, hs = x_prompt, x_sample
    kp_l, vp_l, ks_l, vs_l, av_l = [], [], [], [], []
    cp_l, cs_l, pp_l, ps_l = [], [], [], []
    for i in range(DEPTH):
        li = i // 2
        if i % 2 == 0:
            ew = (norm_mix_e[li], w_in_e[li], a_ln_g[li], a_ln_b[li], a_ws[li], a_bs[li], w_out_e[li],
                  norm_ffn_e[li], ffn_w1[li], ffn_w3[li], ffn_w2[li])
            hp, k_, v_, _ = even_layer(hp, pos_p, moba_prompt, *ew)
            kp_l.append(k_)
            vp_l.append(v_)
            hs, k_, v_, va_ = even_layer(
                hs, pos_s, lambda q, k, v: moba_sample(q, k, v, cache_k, cache_v, page_table, li), *ew)
            ks_l.append(k_)
            vs_l.append(v_)
            av_l.append(va_)
        else:
            ow = (norm_mix_o[li], w_in_o[li], conv_w[li], conv_b[li], c_ln_g[li], c_ln_b[li], pool_w[li],
                  pool_scale[li], w_out_o[li], norm_ffn_o[li], router_w[li], router_b[li],
                  exp_w1[li], exp_w3[li], exp_w2[li])
            conv0 = jnp.zeros((hp.shape[0], CONV_WIDTH - 1, C_WIDTH), hp.dtype)
            pool0 = jnp.zeros((hp.shape[0], POOL_MAX - 1, D_WIDTH), hp.dtype)
            hp, c_, p_ = odd_layer(hp, conv0, pool0, 0, *ow)
            cp_l.append(c_)
            pp_l.append(p_)
            hs, c_, p_ = odd_layer(hs, state_conv[li], state_pool[li], PAST_LEN, *ow)
            cs_l.append(c_)
            ps_l.append(p_)
        hp = per_layer_embed(hp, p_prompt[i], ple_w[i], ple_gate_w[i], ple_norm[i])
        hs = per_layer_embed(hs, p_sample[i], ple_w[i], ple_gate_w[i], ple_norm[i])
    y_prompt = rmsnorm(hp, final_norm)
    y_sample = rmsnorm(hs, final_norm)
    new_k_prompt = jnp.stack(kp_l)
    new_v_prompt = jnp.stack(vp_l)
    new_k_sample = jnp.stack(ks_l)
    new_v_sample = jnp.stack(vs_l)
    new_chunk_v_sample = jnp.stack(av_l)
    new_conv_prompt = jnp.stack(cp_l)
    new_conv_sample = jnp.stack(cs_l)
    new_pool_prompt = jnp.stack(pp_l)
    new_pool_sample = jnp.stack(ps_l)
    return (y_prompt, y_sample, new_k_prompt, new_v_prompt, new_k_sample, new_v_sample, new_chunk_v_sample,
            new_conv_prompt, new_conv_sample, new_pool_prompt, new_pool_sample)
```

```python
import functools
import math

import jax
import jax.numpy as jnp
from jax import lax
from jax.experimental import pallas as pl
from jax.experimental.pallas import tpu as pltpu

F32 = jnp.float32
BF16 = jnp.bfloat16
I32 = jnp.int32

EPS = 1e-6
NEG = -1e30
CHUNK = 128
A_GROUPS = 8
HEAD_DIM = 64
MOBA_BLOCK = 256
MOBA_TOPK = 3
ROPE_THETA = 10000.0
POOL_WINDOWS = (2, 4, 8, 16)
TOP_K = 2

LANES = 128
VMEM_LIMIT_BYTES = 56 * 1024 * 1024
HALO = 32
MOE_TM = 512
KV_PAGES_PER_STEP = 16


def _cparams(sem):
    return pltpu.CompilerParams(dimension_semantics=sem, vmem_limit_bytes=VMEM_LIMIT_BYTES)


def _full(shape):
    n = len(shape)
    return pl.BlockSpec(shape, lambda *a, _n=n: (0,) * _n, pipeline_mode=pl.Buffered(1))


def _whole(shape):
    n = len(shape)
    return pl.BlockSpec(shape, lambda *a, _n=n: (0,) * _n)


def _dot(a, b):
    return jnp.dot(a, b, preferred_element_type=F32)


def _dot_nt(a, b):
    return lax.dot_general(a, b, (((1,), (1,)), ((), ())), preferred_element_type=F32)


def _rms(x, g):
    return x * lax.rsqrt(jnp.mean(x * x, axis=-1, keepdims=True) + EPS) * g


def _layernorm(x, g, b):
    xc = x - jnp.mean(x, axis=-1, keepdims=True)
    var = jnp.mean(xc * xc, axis=-1, keepdims=True)
    return xc * lax.rsqrt(var + EPS) * g + b


def _split_bf16(x):
    hi = x.astype(BF16)
    lo = (x - hi.astype(F32)).astype(BF16)
    return hi, lo


def _lane_pick(x, lane, idx):
    return jnp.sum(jnp.where(lane == idx, x, 0.0), axis=-1, keepdims=True)


def _even_in_kernel(h_ref, g_ref, w_ref, lng_ref, lnb_ref, ws_ref, bias_ref, cos_ref, sin_ref, *outs,
                    seq_l, prompt, a_width, b_width):
    tm = h_ref.shape[0]
    i = pl.program_id(0)
    xn = _rms(h_ref[...], g_ref[...]).astype(BF16)
    aw, bw = a_width, b_width

    u = jax.nn.gelu(_dot(xn, w_ref[:, 0:aw]))
    va = _layernorm(jax.nn.gelu(_dot(xn, w_ref[:, aw:2 * aw])), lng_ref[...], lnb_ref[...])
    vab = va.astype(BF16)

    r = lax.broadcasted_iota(I32, (CHUNK, CHUNK), 0)
    c = lax.broadcasted_iota(I32, (CHUNK, CHUNK), 1)
    sh = int(math.log2(seq_l))
    ok = (lax.shift_right_logical(r, sh) == lax.shift_right_logical(c, sh)) & (c <= r)
    wm = [jnp.where(ok, ws_ref[g], 0.0).astype(BF16) for g in range(A_GROUPS)]
    lane = lax.broadcasted_iota(I32, (CHUNK, LANES), 1)
    gdim = aw // A_GROUPS
    gates = []
    for ci in range(tm // CHUNK):
        vc = vab[ci * CHUNK:(ci + 1) * CHUNK]
        parts = []
        for p in range(aw // LANES):
            vp = vc[:, p * LANES:(p + 1) * LANES]
            r0 = _dot(wm[2 * p], vp)
            r1 = _dot(wm[2 * p + 1], vp)
            parts.append(jnp.where(lane < gdim, r0, r1))
        gates.append(jnp.concatenate(parts, axis=1) + bias_ref[...])
    gate = jnp.concatenate(gates, axis=0)
    a_out = u * gate

    cosv = cos_ref[...]
    sinv = sin_ref[...]
    lane_t = lax.broadcasted_iota(I32, (tm, LANES), 1)
    first_half = (lane_t & (HEAD_DIM - 1)) < HEAD_DIM // 2

    def rope(z):
        parts = []
        for p in range(bw // LANES):
            xs = z[:, p * LANES:(p + 1) * LANES]
            rot = jnp.where(first_half, pltpu.roll(xs, LANES - HEAD_DIM // 2, 1), pltpu.roll(xs, HEAD_DIM // 2, 1))
            parts.append(xs * cosv + rot * sinv)
        return parts

    q_parts = rope(_dot(xn, w_ref[:, 2 * aw:2 * aw + bw]))
    k_parts = rope(_dot(xn, w_ref[:, 2 * aw + bw:2 * aw + 2 * bw]))
    v = _dot(xn, w_ref[:, 2 * aw + 2 * bw:2 * aw + 3 * bw])
    k = jnp.concatenate(k_parts, axis=1)
    q = jnp.concatenate(q_parts, axis=1) * (HEAD_DIM ** -0.5)

    if prompt:
        a_ref, q_ref, k_ref, v_ref, kaug_ref, vaug_ref, kmean_ref = outs
        a_ref[...] = a_out.astype(BF16)
        q_ref[...] = q.astype(BF16)
        k_ref[...] = k
        v_ref[...] = v
        row = lax.broadcasted_iota(I32, (tm, LANES), 0)
        blk = i * (tm // MOBA_BLOCK) + lax.shift_right_logical(row, int(math.log2(MOBA_BLOCK)))
        lo = lane_t < HEAD_DIM
        hot_hi = jnp.where(blk == lane_t - HEAD_DIM, 1.0, 0.0)
        hot_lo = jnp.where(blk == lane_t, 1.0, 0.0)
        for p in range(bw // LANES):
            kp = k_parts[p]
            vp = v[:, p * LANES:(p + 1) * LANES]
            kaug_ref[:, (2 * p) * LANES:(2 * p + 1) * LANES] = jnp.where(lo, kp, hot_hi).astype(BF16)
            kaug_ref[:, (2 * p + 1) * LANES:(2 * p + 2) * LANES] = jnp.where(lo, hot_lo, kp).astype(BF16)
            vaug_ref[:, (2 * p) * LANES:(2 * p + 1) * LANES] = jnp.where(lo, vp, 1.0).astype(BF16)
            vaug_ref[:, (2 * p + 1) * LANES:(2 * p + 2) * LANES] = jnp.where(lo, 1.0, vp).astype(BF16)
        for b in range(tm // MOBA_BLOCK):
            kmean_ref[b] = jnp.mean(k[b * MOBA_BLOCK:(b + 1) * MOBA_BLOCK], axis=0, keepdims=True)
    else:
        a_ref, va_ref, q_ref, k_ref, v_ref = outs
        a_ref[...] = a_out.astype(BF16)
        va_ref[...] = va
        q_ref[...] = q
        k_ref[...] = k
        v_ref[...] = v


def _even_in(h, norm_g, w_in, ln_g, ln_b, ws, bias, cos_t, sin_t, *, tm, seq_l, prompt, a_width, b_width):
    t, d = h.shape
    nw = w_in.shape[1]
    row = lambda i: (i, 0)
    in_specs = [pl.BlockSpec((tm, d), row), _full((1, d)), _full((d, nw)), _full((1, a_width)), _full((1, a_width)),
                _full(ws.shape), _full(bias.shape), pl.BlockSpec((tm, LANES), row), pl.BlockSpec((tm, LANES), row)]
    if prompt:
        nb = t // MOBA_BLOCK
        out_shape = [jax.ShapeDtypeStruct((t, a_width), BF16), jax.ShapeDtypeStruct((t, b_width), BF16),
                     jax.ShapeDtypeStruct((t, b_width), F32), jax.ShapeDtypeStruct((t, b_width), F32),
                     jax.ShapeDtypeStruct((t, 2 * b_width), BF16), jax.ShapeDtypeStruct((t, 2 * b_width), BF16),
                     jax.ShapeDtypeStruct((nb, 1, b_width), F32)]
        out_specs = [pl.BlockSpec((tm, a_width), row), pl.BlockSpec((tm, b_width), row),
                     pl.BlockSpec((tm, b_width), row), pl.BlockSpec((tm, b_width), row),
                     pl.BlockSpec((tm, 2 * b_width), row), pl.BlockSpec((tm, 2 * b_width), row),
                     pl.BlockSpec((tm // MOBA_BLOCK, 1, b_width), lambda i: (i, 0, 0))]
    else:
        out_shape = [jax.ShapeDtypeStruct((t, a_width), BF16), jax.ShapeDtypeStruct((t, a_width), F32),
                     jax.ShapeDtypeStruct((t, b_width), F32), jax.ShapeDtypeStruct((t, b_width), F32),
                     jax.ShapeDtypeStruct((t, b_width), F32)]
        out_specs = [pl.BlockSpec((tm, a_width), row), pl.BlockSpec((tm, a_width), row),
                     pl.BlockSpec((tm, b_width), row), pl.BlockSpec((tm, b_width), row),
                     pl.BlockSpec((tm, b_width), row)]
    return pl.pallas_call(
        functools.partial(_even_in_kernel, seq_l=seq_l, prompt=prompt, a_width=a_width, b_width=b_width),
        grid=(t // tm,), in_specs=in_specs, out_specs=out_specs, out_shape=out_shape,
        compiler_params=_cparams(("arbitrary",)), name="even_in_prompt" if prompt else "even_in_sample",
    )(h, norm_g, w_in, ln_g, ln_b, ws, bias, cos_t, sin_t)


def _moba_prompt_kernel(q_ref, k0_ref, k1_ref, v0_ref, v1_ref, kmh_ref, kml_ref, o_ref):
    tq = q_ref.shape[0]
    i = pl.program_id(1)
    q = q_ref[...]
    qf = q.astype(F32)
    lane = lax.broadcasted_iota(I32, (tq, LANES), 1)
    k_refs = (k0_ref, k1_ref)
    v_refs = (v0_ref, v1_ref)
    neg_inf = jnp.float32(-jnp.inf)

    qa = []
    for e in range(2):
        in_blk = (lane >= HEAD_DIM) if e == 0 else (lane < HEAD_DIM)
        blk = lane - HEAD_DIM if e == 0 else lane
        sc = _dot(q, kmh_ref[0, e]) + _dot(q, kml_ref[0, e])
        cand = jnp.where(in_blk & (blk < i), sc, neg_inf)
        sel = jnp.zeros((tq, LANES), jnp.bool_)
        for _ in range(MOBA_TOPK):
            mx = jnp.max(cand, axis=-1, keepdims=True)
            first = jnp.min(jnp.where(cand == mx, blk, jnp.int32(2 ** 30)), axis=-1, keepdims=True)
            pick = (blk == first) & in_blk & (mx > neg_inf)
            sel = sel | pick
            cand = jnp.where(pick, neg_inf, cand)
        allow = jnp.where(sel | (blk == i), 0.0, NEG)
        qa.append(jnp.where(in_blk, allow, qf).astype(BF16))

    r = lax.broadcasted_iota(I32, (tq, MOBA_BLOCK), 0)
    c = lax.broadcasted_iota(I32, (tq, MOBA_BLOCK), 1)
    causal = c <= r
    own = pl.multiple_of(i * MOBA_BLOCK, MOBA_BLOCK)
    m0, acc0 = [], []
    for e in range(2):
        s = jnp.where(causal, _dot_nt(qa[e], k_refs[e][pl.ds(own, MOBA_BLOCK), :]), NEG)
        m = jnp.max(s, axis=-1, keepdims=True)
        p = jnp.exp(s - m).astype(BF16)
        m0.append(m)
        acc0.append(_dot(p, v_refs[e][pl.ds(own, MOBA_BLOCK), :]))

    def body(n, carry):
        ms, accs = carry
        off = pl.multiple_of(n * MOBA_BLOCK, MOBA_BLOCK)
        new_m, new_acc = [], []
        for e in range(2):
            s = _dot_nt(qa[e], k_refs[e][pl.ds(off, MOBA_BLOCK), :])
            m_new = jnp.maximum(ms[e], jnp.max(s, axis=-1, keepdims=True))
            alpha = jnp.exp(ms[e] - m_new)
            p = jnp.exp(s - m_new).astype(BF16)
            new_m.append(m_new)
            new_acc.append(alpha * accs[e] + _dot(p, v_refs[e][pl.ds(off, MOBA_BLOCK), :]))
        return tuple(new_m), tuple(new_acc)

    _, accs = lax.fori_loop(0, i, body, (tuple(m0), tuple(acc0)))
    outs = [a / pltpu.roll(a, HEAD_DIM, 1) for a in accs]
    o_ref[...] = jnp.where(lane < HEAD_DIM, outs[0], outs[1]).astype(o_ref.dtype)


def _moba_prompt(q, kaug, vaug, km_hi, km_lo):
    t, bw = q.shape
    n_pairs = bw // LANES
    tq = MOBA_BLOCK
    col = lambda e: (lambda p, i, _e=e: (0, 2 * p + _e))
    in_specs = [pl.BlockSpec((tq, LANES), lambda p, i: (i, p)),
                pl.BlockSpec((t, LANES), col(0)), pl.BlockSpec((t, LANES), col(1)),
                pl.BlockSpec((t, LANES), col(0)), pl.BlockSpec((t, LANES), col(1)),
                pl.BlockSpec((1, 2, LANES, LANES), lambda p, i: (p, 0, 0, 0)),
                pl.BlockSpec((1, 2, LANES, LANES), lambda p, i: (p, 0, 0, 0))]
    return pl.pallas_call(
        _moba_prompt_kernel, grid=(n_pairs, t // tq), in_specs=in_specs,
        out_specs=pl.BlockSpec((tq, LANES), lambda p, i: (i, p)),
        out_shape=jax.ShapeDtypeStruct((t, bw), BF16),
        compiler_params=_cparams(("arbitrary", "arbitrary")), name="moba_prompt",
    )(q, kaug, kaug, vaug, vaug, km_hi, km_lo)


def _moba_sample_keys_kernel(pt_ref, *refs, n_steps):
    npg = KV_PAGES_PER_STEP
    pages = refs[:npg]
    qbd_ref, knew_ref, p_ref, pown_ref, linv_ref, s_scr, kmean_scr, bmax_scr = refs[npg:]
    c = pl.program_id(1)
    page = pages[0].shape[1]
    ppb = MOBA_BLOCK // page
    bps = npg // ppb
    qbd = qbd_ref[0]
    n_rows = qbd.shape[0]
    n_blocks = n_steps * bps
    lane = lax.broadcasted_iota(I32, (n_rows, LANES), 1)
    neg_inf = jnp.float32(-jnp.inf)

    @pl.when(c == 0)
    def _():
        bmax_scr[...] = jnp.full((n_rows, LANES), neg_inf, F32)
        kmean_scr[...] = jnp.zeros_like(kmean_scr)

    for j in range(bps):
        kb = jnp.concatenate([pages[j * ppb + x][0] for x in range(ppb)], axis=0)
        blk = c * bps + j
        kmean_scr[pl.ds(blk, 1), :] = jnp.mean(kb, axis=0, keepdims=True)
        s = _dot_nt(qbd, kb.astype(BF16))
        s_scr[:, pl.ds(pl.multiple_of(blk * MOBA_BLOCK, MOBA_BLOCK), MOBA_BLOCK)] = s
        bmax_scr[...] = jnp.where(lane == blk, jnp.max(s, axis=-1, keepdims=True), bmax_scr[...])

    @pl.when(c == n_steps - 1)
    def _():
        kmh, kml = _split_bf16(kmean_scr[...])
        sc = _dot_nt(qbd, kmh) + _dot_nt(qbd, kml)
        cand = jnp.where(lane < n_blocks, sc, neg_inf)
        sel = jnp.zeros((n_rows, LANES), jnp.bool_)
        for _ in range(min(MOBA_TOPK, n_blocks)):
            mx = jnp.max(cand, axis=-1, keepdims=True)
            first = jnp.min(jnp.where(cand == mx, lane, jnp.int32(2 ** 30)), axis=-1, keepdims=True)
            pick = lane == first
            sel = sel | pick
            cand = jnp.where(pick, neg_inf, cand)
        so = _dot_nt(qbd, knew_ref[0].astype(BF16))
        n_new = so.shape[1]
        rr = lax.broadcasted_iota(I32, (n_rows, n_new), 0)
        ss = lax.broadcasted_iota(I32, (n_rows, n_new), 1)
        so = jnp.where(ss <= (rr & (n_new - 1)), so, NEG)
        m = jnp.maximum(jnp.max(jnp.where(sel, bmax_scr[...], neg_inf), axis=-1, keepdims=True),
                        jnp.max(so, axis=-1, keepdims=True))
        po = jnp.exp(so - m)
        l = jnp.sum(po, axis=-1, keepdims=True)
        self32 = sel.astype(F32)
        for b in range(n_blocks):
            on = jnp.sum(jnp.where(lane == b, self32, 0.0), axis=-1, keepdims=True)
            sb = s_scr[:, b * MOBA_BLOCK:(b + 1) * MOBA_BLOCK]
            pb = (jnp.exp(sb - m) * on).astype(BF16)
            l = l + jnp.sum(pb.astype(F32), axis=-1, keepdims=True)
            p_ref[0, :, b * MOBA_BLOCK:(b + 1) * MOBA_BLOCK] = pb
        pown_ref[0] = jnp.concatenate([po, jnp.zeros((n_rows, LANES - n_new), F32)], axis=1)
        linv_ref[0] = jnp.broadcast_to(1.0 / l, (n_rows, LANES))


def _moba_sample_values_kernel(pt_ref, *refs, n_steps, n_heads):
    npg = KV_PAGES_PER_STEP
    pages = refs[:npg]
    p_ref, pown_ref, linv_ref, vnew_ref, o_ref, acc_scr = refs[npg:]
    c = pl.program_id(1)
    vb = jnp.concatenate([pg[0] for pg in pages], axis=0).astype(BF16)
    part = _dot(p_ref[0], vb)

    @pl.when(c == 0)
    def _():
        acc_scr[...] = part

    @pl.when(c > 0)
    def _():
        acc_scr[...] += part

    @pl.when(c == n_steps - 1)
    def _():
        vnew = vnew_ref[0]
        n_new = vnew.shape[0]
        o = (acc_scr[...] + _dot(pown_ref[0][:, :n_new].astype(BF16), vnew.astype(BF16))) * linv_ref[0][:, :1]
        n_rows, hw = o.shape
        rr = lax.broadcasted_iota(I32, (n_rows, hw), 0)
        ll = lax.broadcasted_iota(I32, (n_rows, hw), 1)
        keep = (lax.shift_right_logical(rr, int(math.log2(n_new)))
                == lax.shift_right_logical(ll, int(math.log2(HEAD_DIM))))
        o = jnp.where(keep, o, 0.0).reshape(n_heads, n_new, hw)
        o_ref[0] = jnp.sum(o, axis=0).astype(o_ref.dtype)


def _moba_sample(q, k, v, cache_k, cache_v, page_table, n_batch):
    hw = q.shape[1]
    n_heads = hw // HEAD_DIM
    n_new = q.shape[0] // n_batch
    n_phys, page = cache_k.shape[0], cache_k.shape[1]
    n_pages = page_table.shape[1]
    npg = KV_PAGES_PER_STEP
    assert n_pages % npg == 0 and (n_pages * page) % MOBA_BLOCK == 0 and MOBA_BLOCK % page == 0
    n_steps = n_pages // npg
    past = n_pages * page
    n_blocks = past // MOBA_BLOCK
    assert n_blocks <= LANES and n_new & (n_new - 1) == 0
    n_rows = n_heads * n_new
    ck = cache_k.reshape(n_phys, page, hw)
    cv = cache_v.reshape(n_phys, page, hw)
    pt = page_table.reshape(-1).astype(I32)
    q3 = q.reshape(n_batch, n_new, n_heads, HEAD_DIM)
    eye = jnp.eye(n_heads, dtype=F32)
    qbd = (q3.transpose(0, 2, 1, 3)[:, :, :, None, :] * eye[None, :, None, :, None]).reshape(n_batch, n_rows, hw)
    qbd = qbd.astype(BF16)
    k3 = k.reshape(n_batch, n_new, hw)
    v3 = v.reshape(n_batch, n_new, hw)

    def page_spec(s):
        return pl.BlockSpec((1, page, hw), lambda b, c, pt_ref, _s=s: (pt_ref[b * n_pages + c * npg + _s], 0, 0))

    per_b = lambda shape: pl.BlockSpec(shape, lambda b, c, pt_ref: (b,) + (0,) * (len(shape) - 1))
    p, pown, linv = pl.pallas_call(
        functools.partial(_moba_sample_keys_kernel, n_steps=n_steps),
        grid_spec=pltpu.PrefetchScalarGridSpec(
            num_scalar_prefetch=1, grid=(n_batch, n_steps),
            in_specs=[page_spec(s) for s in range(npg)] + [per_b((1, n_rows, hw)), per_b((1, n_new, hw))],
            out_specs=[per_b((1, n_rows, past)), per_b((1, n_rows, LANES)), per_b((1, n_rows, LANES))],
            scratch_shapes=[pltpu.VMEM((n_rows, past), F32), pltpu.VMEM((LANES, hw), F32),
                            pltpu.VMEM((n_rows, LANES), F32)]),
        out_shape=[jax.ShapeDtypeStruct((n_batch, n_rows, past), BF16),
                   jax.ShapeDtypeStruct((n_batch, n_rows, LANES), F32),
                   jax.ShapeDtypeStruct((n_batch, n_rows, LANES), F32)],
        compiler_params=_cparams(("arbitrary", "arbitrary")), name="moba_sample_keys",
    )(pt, *([ck] * npg), qbd, k3)

    out = pl.pallas_call(
        functools.partial(_moba_sample_values_kernel, n_steps=n_steps, n_heads=n_heads),
        grid_spec=pltpu.PrefetchScalarGridSpec(
            num_scalar_prefetch=1, grid=(n_batch, n_steps),
            in_specs=[page_spec(s) for s in range(npg)]
            + [pl.BlockSpec((1, n_rows, npg * page), lambda b, c, pt_ref: (b, 0, c)),
               per_b((1, n_rows, LANES)), per_b((1, n_rows, LANES)), per_b((1, n_new, hw))],
            out_specs=per_b((1, n_new, hw)),
            scratch_shapes=[pltpu.VMEM((n_rows, hw), F32)]),
        out_shape=jax.ShapeDtypeStruct((n_batch, n_new, hw), BF16),
        compiler_params=_cparams(("arbitrary", "arbitrary")), name="moba_sample_values",
    )(pt, *([cv] * npg), p, pown, linv, v3)
    return out.reshape(n_batch * n_new, hw)


def _ple(h, p_ref, wp_ref, wg_ref, pn_ref):
    gate = jax.nn.sigmoid(_dot(_rms(h, pn_ref[...]).astype(BF16), wg_ref[...]))
    return h + _dot(p_ref[...].astype(BF16), wp_ref[...]) * gate


def _even_tail_kernel(h_ref, a_ref, b_ref, wo_ref, nf_ref, w1_ref, w3_ref, w2_ref, p_ref, wp_ref, wg_ref, pn_ref,
                      nn_ref, h_out_ref, xn_out_ref, *, ff_chunks):
    aw = a_ref.shape[1]
    h1 = h_ref[...] + _dot(a_ref[...], wo_ref[0:aw, :]) + _dot(b_ref[...], wo_ref[aw:, :])
    xn = _rms(h1, nf_ref[...]).astype(BF16)
    ff = w1_ref.shape[1]
    cw = ff // ff_chunks
    hid = []
    for ci in range(ff_chunks):
        sl = slice(ci * cw, (ci + 1) * cw)
        hid.append((jax.nn.silu(_dot(xn, w1_ref[:, sl])) * _dot(xn, w3_ref[:, sl])).astype(BF16))
    acc = h1 + _dot(jnp.concatenate(hid, axis=1), w2_ref[...])
    h3 = _ple(acc, p_ref, wp_ref, wg_ref, pn_ref)
    h_out_ref[...] = h3
    xn_out_ref[...] = _rms(h3, nn_ref[...]).astype(BF16)


def _even_tail(h, a, b, w_out, norm_ffn, w1, w3, w2, p, wp, wg, pn, next_norm, *, tm):
    t, d = h.shape
    ff = w1.shape[1]
    ff_chunks = next(n for n in (4, 2, 1, 11, 22) if ff % (n * LANES) == 0) if ff % LANES == 0 else 1
    row = lambda i: (i, 0)
    in_specs = [pl.BlockSpec((tm, d), row), pl.BlockSpec((tm, a.shape[1]), row), pl.BlockSpec((tm, b.shape[1]), row),
                _full(w_out.shape), _full((1, d)), _full(w1.shape), _full(w3.shape), _full(w2.shape),
                pl.BlockSpec((tm, p.shape[1]), row), _full(wp.shape), _full(wg.shape), _full((1, d)), _full((1, d))]
    return pl.pallas_call(
        functools.partial(_even_tail_kernel, ff_chunks=ff_chunks),
        grid=(t // tm,), in_specs=in_specs,
        out_specs=[pl.BlockSpec((tm, d), row), pl.BlockSpec((tm, d), row)],
        out_shape=[jax.ShapeDtypeStruct((t, d), F32), jax.ShapeDtypeStruct((t, d), BF16)],
        compiler_params=_cparams(("arbitrary",)), name="even_tail",
    )(h, a, b, w_out, norm_ffn, w1, w3, w2, p, wp, wg, pn, next_norm)


def _odd_tail(conv, d_in, h3, refs, count_scr, n_experts):
    (cb_ref, clg_ref, clb_ref, pw_ref, ps_ref, wo_ref, nf_ref, rwh_ref, rwl_ref, rb_ref) = refs
    rows = conv.shape[0]
    cw = conv.shape[1]
    cact = jax.nn.silu(_layernorm(conv + cb_ref[...], clg_ref[...], clb_ref[...]))
    db = d_in.astype(BF16)
    gdim = d_in.shape[1] // len(POOL_WINDOWS)
    dparts = [_dot(db[:, g * gdim:(g + 1) * gdim], pw_ref[g]) for g in range(len(POOL_WINDOWS))]
    dmix = jnp.concatenate(dparts, axis=1) * ps_ref[...]
    h4 = h3 + _dot(cact.astype(BF16), wo_ref[0:cw, :]) + _dot(dmix.astype(BF16), wo_ref[cw:, :])
    xn = _rms(h4, nf_ref[...])
    xh, xl = _split_bf16(xn)
    logits = _dot(xh, rwh_ref[...]) + _dot(xl, rwh_ref[...]) + _dot(xh, rwl_ref[...]) + rb_ref[...]
    lane = lax.broadcasted_iota(I32, (rows, LANES), 1)
    neg_inf = jnp.float32(-jnp.inf)
    cand = jnp.where(lane < n_experts, logits, neg_inf)
    v1 = jnp.max(cand, axis=-1, keepdims=True)
    e1 = jnp.min(jnp.where(cand == v1, lane, jnp.int32(2 ** 30)), axis=-1, keepdims=True)
    cand2 = jnp.where(lane == e1, neg_inf, cand)
    v2 = jnp.max(cand2, axis=-1, keepdims=True)
    e2 = jnp.min(jnp.where(cand2 == v2, lane, jnp.int32(2 ** 30)), axis=-1, keepdims=True)
    ex = jnp.exp(v2 - v1)
    g1 = 1.0 / (1.0 + ex)
    g2 = ex / (1.0 + ex)
    chosen = jnp.where((lane == e1) | (lane == e2), 1.0, 0.0).astype(BF16)
    rr = lax.broadcasted_iota(I32, (rows, rows), 0)
    cc = lax.broadcasted_iota(I32, (rows, rows), 1)
    before = jnp.where(cc < rr, 1.0, 0.0).astype(BF16)
    prefix = _dot(before, chosen) + count_scr[...]
    r1 = _lane_pick(prefix, lane, e1)
    r2 = _lane_pick(prefix, lane, e2)
    count_scr[...] = count_scr[...] + jnp.sum(chosen.astype(F32), axis=0, keepdims=True)
    meta = jnp.zeros((rows, LANES), F32)
    for idx, val in enumerate((e1.astype(F32), e2.astype(F32), g1, g2, r1, r2)):
        meta = jnp.where(lane == idx, val, meta)
    return h4, xn, meta


def _odd_prompt_kernel(xm_ref, xh_ref, h3_ref, wi_ref, cw_ref, *refs, n_experts, conv_hist, pool_hist):
    tail_refs = refs[:10]
    cnt_in_ref, h4_ref, xn_ref, meta_ref, cnt_ref, cstate_ref, pstate_ref, count_scr = refs[10:]
    i = pl.program_id(0)
    tm = xm_ref.shape[0]
    cwid = cw_ref.shape[1]

    @pl.when(i == 0)
    def _():
        count_scr[...] = cnt_in_ref[...]

    x = jnp.concatenate([xh_ref[...], xm_ref[...]], axis=0)
    z = _dot(x, wi_ref[...])
    row = lax.broadcasted_iota(I32, (HALO + tm, 1), 0)
    z = jnp.where((row >= HALO) | (i > 0), z, 0.0)
    c_in = z[:, :cwid] * jax.nn.sigmoid(z[:, cwid:2 * cwid])
    dx = z[:, 2 * cwid:]

    cwv = cw_ref[...]
    width = cwv.shape[0]
    lead = HALO - (width - 1)

    def tap(kp):
        k = kp - lead
        return cwv[k:k + 1, :] if 0 <= k < width else None

    ext = c_in
    conv = None
    for b in range(8):
        n_a = (HALO // 8 + 1) if b == 0 else HALO // 8
        rows_b = tm if b == 0 else tm + 8
        ub = None
        for a in range(n_a):
            w = tap(8 * a + b)
            if w is None:
                continue
            term = ext[8 * a:8 * a + rows_b] * w
            ub = term if ub is None else ub + term
        if ub is None:
            continue
        piece = ub if b == 0 else pltpu.roll(ub, rows_b - b, 0)[:tm]
        conv = piece if conv is None else conv + piece

    s = dx
    sums = {}
    span = 1
    while span < max(POOL_WINDOWS):
        s = s + pltpu.roll(s, span, 0)
        span *= 2
        sums[span] = s
    gdim = dx.shape[1] // len(POOL_WINDOWS)
    pos = i * tm + lax.broadcasted_iota(I32, (tm, 1), 0)
    pooled = []
    for g, w in enumerate(POOL_WINDOWS):
        win = sums[w][HALO:, g * gdim:(g + 1) * gdim]
        cnt = jnp.minimum(pos + 1, w).astype(F32)
        pooled.append(win / cnt)
    dmain = dx[HALO:]
    d_in = jnp.concatenate(pooled, axis=1) - dmain

    h4, xn, meta = _odd_tail(conv, d_in, h3_ref[...], tail_refs, count_scr, n_experts)
    h4_ref[...] = h4
    xn_ref[...] = xn
    meta_ref[...] = meta
    cnt_ref[...] = count_scr[...]
    ctail = c_in[tm:]
    cstate_ref[...] = pltpu.roll(ctail, conv_hist, 0)[:conv_hist] if conv_hist < HALO else ctail
    ptail_rows = 8 * (-(-pool_hist // 8))
    ptail = dmain[tm - ptail_rows:]
    pstate_ref[...] = pltpu.roll(ptail, pool_hist, 0)[:pool_hist] if pool_hist < ptail_rows else ptail


def _odd_sample_kernel(x_ref, h3_ref, wi_ref, cw_ref, *refs, n_experts, n_batch):
    tail_refs = refs[:10]
    cnt_in_ref, chist_ref, phist_ref, h4_ref, xn_ref, meta_ref, cnt_ref, cstate_ref, pstate_ref, count_scr = refs[10:]
    rows = x_ref.shape[0]
    n_new = rows // n_batch
    cwid = cw_ref.shape[1]
    count_scr[...] = cnt_in_ref[...]
    z = _dot(x_ref[...], wi_ref[...])
    c_in = z[:, :cwid] * jax.nn.sigmoid(z[:, cwid:2 * cwid])
    dx = z[:, 2 * cwid:]
    cwv = cw_ref[...]
    width = cwv.shape[0]
    hist_c = chist_ref[...]
    xp = [hist_c[j] for j in range(width - 1)] + [c_in[t * n_batch:(t + 1) * n_batch] for t in range(n_new)]
    conv = []
    for t in range(n_new):
        acc = xp[t] * cwv[0:1, :]
        for k in range(1, width):
            acc = acc + xp[t + k] * cwv[k:k + 1, :]
        conv.append(acc)
    conv = jnp.concatenate(conv, axis=0)
    cstate_ref[...] = jnp.stack(xp[n_new:], axis=0)

    hist_p = phist_ref[...]
    n_ph = hist_p.shape[0]
    xq = [hist_p[j] for j in range(n_ph)] + [dx[t * n_batch:(t + 1) * n_batch] for t in range(n_new)]
    gdim = dx.shape[1] // len(POOL_WINDOWS)
    lane = lax.broadcasted_iota(I32, (n_batch, dx.shape[1]), 1)
    d_in = []
    for t in range(n_new):
        run = xq[n_ph + t]
        tot = jnp.zeros_like(run)
        done = 1
        for g, w in enumerate(POOL_WINDOWS):
            for j in range(done, w):
                run = run + xq[n_ph + t - j]
            done = w
            in_g = (lane >= g * gdim) & (lane < (g + 1) * gdim)
            tot = jnp.where(in_g, run / float(w), tot)
        d_in.append(tot - xq[n_ph + t])
    d_in = jnp.concatenate(d_in, axis=0)
    pstate_ref[...] = jnp.stack(xq[n_new:], axis=0)

    h4, xn, meta = _odd_tail(conv, d_in, h3_ref[...], tail_refs, count_scr, n_experts)
    h4_ref[...] = h4
    xn_ref[...] = xn
    meta_ref[...] = meta
    cnt_ref[...] = count_scr[...]


def _odd_common_args(conv_b, c_ln_g, c_ln_b, pool_w, pool_scale, w_out, norm_ffn, rw_hi, rw_lo, rb):
    args = (conv_b, c_ln_g, c_ln_b, pool_w, pool_scale, w_out, norm_ffn, rw_hi, rw_lo, rb)
    return args, [_full(a.shape) for a in args]


def _odd_prompt(xn3, h3, w_in, conv_w, common, counts_in, *, tm, n_experts):
    t, d = h3.shape
    cwid = conv_w.shape[1]
    conv_hist = conv_w.shape[0] - 1
    pool_hist = max(POOL_WINDOWS) - 1
    args, specs = common
    row = lambda i: (i, 0)
    hpb = tm // HALO
    in_specs = [pl.BlockSpec((tm, d), row), pl.BlockSpec((HALO, d), lambda i: (jnp.maximum(i * hpb - 1, 0), 0)),
                pl.BlockSpec((tm, d), row), _full(w_in.shape), _full(conv_w.shape)] + specs + [_full((1, LANES))]
    dwid = w_in.shape[1] - 2 * cwid
    return pl.pallas_call(
        functools.partial(_odd_prompt_kernel, n_experts=n_experts, conv_hist=conv_hist, pool_hist=pool_hist),
        grid=(t // tm,), in_specs=in_specs,
        out_specs=[pl.BlockSpec((tm, d), row), pl.BlockSpec((tm, d), row), pl.BlockSpec((tm, LANES), row),
                   _whole((1, LANES)), _whole((conv_hist, cwid)), _whole((pool_hist, dwid))],
        out_shape=[jax.ShapeDtypeStruct((t, d), F32), jax.ShapeDtypeStruct((t, d), F32),
                   jax.ShapeDtypeStruct((t, LANES), F32), jax.ShapeDtypeStruct((1, LANES), F32),
                   jax.ShapeDtypeStruct((conv_hist, cwid), F32), jax.ShapeDtypeStruct((pool_hist, dwid), F32)],
        scratch_shapes=[pltpu.VMEM((1, LANES), F32)],
        compiler_params=_cparams(("arbitrary",)), name="odd_prompt",
    )(xn3, xn3, h3, w_in, conv_w, *args, counts_in)


def _odd_sample(xn3, h3, w_in, conv_w, common, counts_in, conv_hist_t, pool_hist_t, *, n_batch, n_experts):
    t, d = h3.shape
    cwid = conv_w.shape[1]
    dwid = w_in.shape[1] - 2 * cwid
    args, specs = common
    in_specs = [_full((t, d)), _full((t, d)), _full(w_in.shape), _full(conv_w.shape)] + specs + \
               [_full((1, LANES)), _full(conv_hist_t.shape), _full(pool_hist_t.shape)]
    return pl.pallas_call(
        functools.partial(_odd_sample_kernel, n_experts=n_experts, n_batch=n_batch),
        grid=(1,), in_specs=in_specs,
        out_specs=[_whole((t, d)), _whole((t, d)), _whole((t, LANES)), _whole((1, LANES)),
                   _whole(conv_hist_t.shape), _whole(pool_hist_t.shape)],
        out_shape=[jax.ShapeDtypeStruct((t, d), F32), jax.ShapeDtypeStruct((t, d), F32),
                   jax.ShapeDtypeStruct((t, LANES), F32), jax.ShapeDtypeStruct((1, LANES), F32),
                   jax.ShapeDtypeStruct(conv_hist_t.shape, F32), jax.ShapeDtypeStruct(pool_hist_t.shape, F32)],
        scratch_shapes=[pltpu.VMEM((1, LANES), F32)],
        compiler_params=_cparams(("arbitrary",)), name="odd_sample",
    )(xn3, h3, w_in, conv_w, *args, counts_in, conv_hist_t, pool_hist_t)


def _row_copy(src, s, dst, d, sem):
    return pltpu.make_async_copy(src.at[pl.ds(s, 1)], dst.at[pl.ds(d, 1)], sem)


def _moe_scatter_kernel(dest_ref, x_hbm, buf_in_hbm, buf_hbm, sem, *, tm):
    del buf_in_hbm
    base = pl.program_id(0) * tm

    def start(r, carry):
        for k in range(TOP_K):
            _row_copy(x_hbm, base + r, buf_hbm, dest_ref[(base + r) * TOP_K + k], sem).start()
        return carry

    lax.fori_loop(0, tm, start, 0)

    def wait(r, carry):
        for k in range(TOP_K):
            _row_copy(x_hbm, 0, buf_hbm, 0, sem).wait()
        return carry

    lax.fori_loop(0, tm, wait, 0)


def _moe_scatter(dest, x, buf, *, tm):
    t = x.shape[0]
    any_spec = pl.BlockSpec(memory_space=pl.ANY)
    return pl.pallas_call(
        functools.partial(_moe_scatter_kernel, tm=tm),
        grid_spec=pltpu.PrefetchScalarGridSpec(
            num_scalar_prefetch=1, grid=(t // tm,), in_specs=[any_spec, any_spec], out_specs=any_spec,
            scratch_shapes=[pltpu.SemaphoreType.DMA(())]),
        out_shape=jax.ShapeDtypeStruct(buf.shape, buf.dtype),
        input_output_aliases={2: 0},
        compiler_params=_cparams(("arbitrary",)), name="moe_scatter",
    )(dest, x, buf)


def _moe_experts_kernel(be_ref, nu_ref, x_ref, w1_ref, w3_ref, w2_ref, y_ref):
    i = pl.program_id(0)
    j = pl.program_id(1)
    used = i < nu_ref[0]

    @pl.when(used)
    def _():
        xb = x_ref[...].astype(BF16)
        hid = jax.nn.silu(_dot(xb, w1_ref[0])) * _dot(xb, w3_ref[0])
        part = _dot(hid.astype(BF16), w2_ref[0])

        @pl.when(j == 0)
        def _():
            y_ref[...] = part

        @pl.when(j > 0)
        def _():
            y_ref[...] += part

    @pl.when(jnp.logical_not(used) & (j == 0))
    def _():
        y_ref[...] = jnp.zeros_like(y_ref)


def _moe_experts(blk_e, n_used, xbuf, w1, w3, w2, *, tm, tf):
    rows, d = xbuf.shape
    ff = w1.shape[2]
    nj = ff // tf
    nb = rows // tm
    live_j = lambda i, j, nu: jnp.where(i < nu[0], j, nj - 1)
    return pl.pallas_call(
        _moe_experts_kernel,
        grid_spec=pltpu.PrefetchScalarGridSpec(
            num_scalar_prefetch=2, grid=(nb, nj),
            in_specs=[pl.BlockSpec((tm, d), lambda i, j, be, nu: (i, 0)),
                      pl.BlockSpec((1, d, tf), lambda i, j, be, nu: (be[i], 0, live_j(i, j, nu))),
                      pl.BlockSpec((1, d, tf), lambda i, j, be, nu: (be[i], 0, live_j(i, j, nu))),
                      pl.BlockSpec((1, tf, d), lambda i, j, be, nu: (be[i], live_j(i, j, nu), 0))],
            out_specs=pl.BlockSpec((tm, d), lambda i, j, be, nu: (i, 0))),
        out_shape=jax.ShapeDtypeStruct((rows, d), F32),
        compiler_params=_cparams(("arbitrary", "arbitrary")), name="moe_experts",
    )(blk_e, n_used, xbuf, w1, w3, w2)


def _moe_combine_kernel(dest_ref, y_hbm, h4_ref, meta_ref, p_ref, wp_ref, wg_ref, pn_ref, fn_ref, o_ref,
                        y0_scr, y1_scr, sem, *, tm):
    base = pl.program_id(0) * tm
    bufs = (y0_scr, y1_scr)

    def start(r, carry):
        for k in range(TOP_K):
            _row_copy(y_hbm, dest_ref[(base + r) * TOP_K + k], bufs[k], r, sem).start()
        return carry

    lax.fori_loop(0, tm, start, 0)

    def wait(r, carry):
        for k in range(TOP_K):
            _row_copy(y_hbm, 0, bufs[k], 0, sem).wait()
        return carry

    lax.fori_loop(0, tm, wait, 0)
    meta = meta_ref[...]
    lane = lax.broadcasted_iota(I32, meta.shape, 1)
    g1 = jnp.sum(jnp.where(lane == 2, meta, 0.0), axis=-1, keepdims=True)
    g2 = jnp.sum(jnp.where(lane == 3, meta, 0.0), axis=-1, keepdims=True)
    h5 = h4_ref[...] + y0_scr[...] * g1 + y1_scr[...] * g2
    h6 = _ple(h5, p_ref, wp_ref, wg_ref, pn_ref)
    o_ref[...] = _rms(h6, fn_ref[...])


def _moe_combine(dest, ybuf, h4, meta, p, wp, wg, pn, fn, *, tm):
    t, d = h4.shape
    row = lambda i, dref: (i, 0)
    full = lambda shape: pl.BlockSpec(shape, lambda i, dref, _n=len(shape): (0,) * _n)
    return pl.pallas_call(
        functools.partial(_moe_combine_kernel, tm=tm),
        grid_spec=pltpu.PrefetchScalarGridSpec(
            num_scalar_prefetch=1, grid=(t // tm,),
            in_specs=[pl.BlockSpec(memory_space=pl.ANY), pl.BlockSpec((tm, d), row), pl.BlockSpec((tm, LANES), row),
                      pl.BlockSpec((tm, p.shape[1]), row), full(wp.shape), full(wg.shape), full((1, d)), full((1, d))],
            out_specs=pl.BlockSpec((tm, d), row),
            scratch_shapes=[pltpu.VMEM((tm, d), F32), pltpu.VMEM((tm, d), F32), pltpu.SemaphoreType.DMA(())]),
        out_shape=jax.ShapeDtypeStruct((t, d), F32),
        compiler_params=_cparams(("arbitrary",)), name="moe_combine",
    )(dest, ybuf, h4, meta, p, wp, wg, pn, fn)


def _rope_tables(pos):
    half = HEAD_DIM // 2
    inv = ROPE_THETA ** (-jnp.arange(half, dtype=F32) * (2.0 / HEAD_DIM))
    ang = pos.astype(F32)[:, None] * inv[None, :]
    cos, sin = jnp.cos(ang), jnp.sin(ang)
    reps = LANES // HEAD_DIM
    cos_t = jnp.tile(jnp.concatenate([cos, cos], axis=1), (1, reps))
    sin_t = jnp.tile(jnp.concatenate([-sin, sin], axis=1), (1, reps))
    return cos_t, sin_t


def _row_tile(t, pref):
    for tm in (pref, 256, 128, 64, 32, 16, 8):
        if tm <= t and t % tm == 0:
            return tm
    return t


def kernel(x_prompt, x_sample, cache_k, cache_v, state_conv, state_pool, page_table, p_prompt, p_sample, norm_mix_e, w_in_e, a_ln_g, a_ln_b, a_ws, a_bs, w_out_e, norm_ffn_e, ffn_w1, ffn_w3, ffn_w2, norm_mix_o, w_in_o, conv_w, conv_b, c_ln_g, c_ln_b, pool_w, pool_scale, w_out_o, norm_ffn_o, router_w, router_b, exp_w1, exp_w3, exp_w2, ple_w, ple_gate_w, ple_norm, final_norm):
    assert x_prompt.shape[0] == 1 and norm_mix_e.shape[0] == 1 and norm_mix_o.shape[0] == 1 and ple_w.shape[0] == 2
    n_tok, d = x_prompt.shape[1], x_prompt.shape[2]
    n_batch, n_new = x_sample.shape[0], x_sample.shape[1]
    n_s = n_batch * n_new
    a_width = a_ln_g.shape[1]
    b_width = (w_in_e.shape[2] - 2 * a_width) // 3
    n_heads = b_width // HEAD_DIM
    gdim = a_width // A_GROUPS
    n_experts = router_w.shape[2]
    page = cache_k.shape[2]
    past_len = page_table.shape[1] * page
    assert n_tok % MOBA_BLOCK == 0 and n_tok // MOBA_BLOCK <= HEAD_DIM and past_len % MOBA_BLOCK == 0
    assert n_new <= CHUNK and CHUNK % n_new == 0 and n_s % CHUNK == 0

    bf = lambda w: w.astype(BF16)
    row2 = lambda v: v.reshape(1, -1)

    w_in = bf(w_in_e[0])
    ws_p = a_ws[0]
    bias_p = jnp.repeat(a_bs[0].T, gdim, axis=1)
    reps = CHUNK // n_new
    ws_s = jnp.tile(a_ws[0][:, :n_new, :n_new], (1, reps, reps))
    bias_s = jnp.tile(jnp.repeat(a_bs[0][:, :n_new].T, gdim, axis=1), (reps, 1))
    cos_p, sin_p = _rope_tables(jnp.arange(n_tok, dtype=I32))
    cos_s, sin_s = _rope_tables(past_len + (jnp.arange(n_s, dtype=I32) % n_new))
    hp0 = x_prompt[0]
    hs0 = x_sample.reshape(n_s, d)
    ev = dict(a_width=a_width, b_width=b_width)
    tm_p = _row_tile(n_tok, 512)
    a_p, q_p, k_p, v_p, kaug, vaug, kmean = _even_in(
        hp0, row2(norm_mix_e[0]), w_in, row2(a_ln_g[0]), row2(a_ln_b[0]), ws_p, bias_p, cos_p, sin_p,
        tm=tm_p, seq_l=CHUNK, prompt=True, **ev)
    a_s, va_s, q_s, k_s, v_s = _even_in(
        hs0, row2(norm_mix_e[0]), w_in, row2(a_ln_g[0]), row2(a_ln_b[0]), ws_s, bias_s, cos_s, sin_s,
        tm=_row_tile(n_s, 256), seq_l=n_new, prompt=False, **ev)

    n_blk = n_tok // MOBA_BLOCK
    km = kmean.reshape(n_blk, n_heads // 2, 2, HEAD_DIM)
    km_t = jnp.zeros((n_heads // 2, 2, LANES, LANES), F32)
    km_t = km_t.at[:, 0, :HEAD_DIM, HEAD_DIM:HEAD_DIM + n_blk].set(km[:, :, 0].transpose(1, 2, 0))
    km_t = km_t.at[:, 1, HEAD_DIM:, :n_blk].set(km[:, :, 1].transpose(1, 2, 0))
    km_hi = km_t.astype(BF16)
    km_lo = (km_t - km_hi.astype(F32)).astype(BF16)
    attn_p = _moba_prompt(q_p, kaug, vaug, km_hi, km_lo)
    attn_s = _moba_sample(q_s, k_s, v_s, cache_k[0], cache_v[0], page_table, n_batch)

    tail = (bf(w_out_e[0]), row2(norm_ffn_e[0]), bf(ffn_w1[0]), bf(ffn_w3[0]), bf(ffn_w2[0]))
    ple0 = (bf(ple_w[0]), bf(ple_gate_w[0]), row2(ple_norm[0]), row2(norm_mix_o[0]))
    h3_p, xn3_p = _even_tail(hp0, a_p, attn_p, *tail, p_prompt[0, 0], *ple0, tm=tm_p)
    h3_s, xn3_s = _even_tail(hs0, a_s, attn_s, *tail, p_sample[0].reshape(n_s, -1), *ple0, tm=_row_tile(n_s, 256))

    def time_major(x):
        return x.reshape(n_batch, n_new, -1).transpose(1, 0, 2).reshape(n_s, -1)

    rw = jnp.zeros((d, LANES), F32).at[:, :n_experts].set(router_w[0])
    rw_hi = rw.astype(BF16)
    rw_lo = (rw - rw_hi.astype(F32)).astype(BF16)
    rb = jnp.zeros((1, LANES), F32).at[0, :n_experts].set(router_b[0])
    common = _odd_common_args(row2(conv_b[0]), row2(c_ln_g[0]), row2(c_ln_b[0]), bf(pool_w[0]), row2(pool_scale[0]),
                              bf(w_out_o[0]), row2(norm_ffn_o[0]), rw_hi, rw_lo, rb)
    w_in1 = bf(w_in_o[0])
    zero_counts = jnp.zeros((1, LANES), F32)
    h4_p, xn5_p, meta_p, cnt_p, cstate_p, pstate_p = _odd_prompt(
        xn3_p, h3_p, w_in1, conv_w[0], common, zero_counts, tm=tm_p, n_experts=n_experts)
    h4_s, xn5_s, meta_s, cnt_all, cstate_s, pstate_s = _odd_sample(
        time_major(xn3_s), time_major(h3_s), w_in1, conv_w[0], common, cnt_p,
        state_conv[0].transpose(1, 0, 2), state_pool[0].transpose(1, 0, 2), n_batch=n_batch, n_experts=n_experts)

    counts = cnt_all[0, :n_experts].astype(I32)
    padded = (counts + MOE_TM - 1) // MOE_TM * MOE_TM
    pends = jnp.cumsum(padded)
    pstarts = (pends - padded).astype(F32)
    n_rows_all = (n_tok + n_s) * TOP_K
    n_blocks = -(-n_rows_all // MOE_TM) + n_experts
    blk_e = jnp.minimum(jnp.searchsorted(pends, jnp.arange(n_blocks, dtype=I32) * MOE_TM, side='right'),
                        n_experts - 1).astype(I32)
    n_used = (pends[-1] // MOE_TM).astype(I32).reshape(1)
    last_e = blk_e[jnp.maximum(n_used[0] - 1, 0)]
    blk_e = jnp.where(jnp.arange(n_blocks) < n_used[0], blk_e, last_e)

    def dests(meta):
        e = meta[:, :TOP_K].astype(I32)
        return (pstarts[e] + meta[:, 4:4 + TOP_K]).astype(I32).reshape(-1)

    dest_p, dest_s = dests(meta_p), dests(meta_s)
    xbuf = jnp.zeros((n_blocks * MOE_TM, d), F32)
    xbuf = _moe_scatter(dest_p, xn5_p, xbuf, tm=_row_tile(n_tok, 512))
    xbuf = _moe_scatter(dest_s, xn5_s, xbuf, tm=_row_tile(n_s, 256))
    ff_e = exp_w1.shape[3]
    tf = next(c for c in (896, 512, 256, 128, ff_e) if ff_e % c == 0)
    ybuf = _moe_experts(blk_e, n_used, xbuf, bf(exp_w1[0]), bf(exp_w3[0]), bf(exp_w2[0]), tm=MOE_TM, tf=tf)
    ple1 = (bf(ple_w[1]), bf(ple_gate_w[1]), row2(ple_norm[1]), row2(final_norm))
    y_p = _moe_combine(dest_p, ybuf, h4_p, meta_p, p_prompt[1, 0], *ple1, tm=_row_tile(n_tok, 256))
    y_s = _moe_combine(dest_s, ybuf, h4_s, meta_s, time_major(p_sample[1].reshape(n_s, -1)), *ple1,
                       tm=_row_tile(n_s, 256))

    def batch_major(x):
        return x.reshape(n_new, n_batch, -1).transpose(1, 0, 2)

    kv = lambda x, b, l: x.reshape(1, b, l, n_heads, HEAD_DIM)
    return (y_p[None], batch_major(y_s),
            kv(k_p, 1, n_tok), kv(v_p, 1, n_tok), kv(k_s, n_batch, n_new), kv(v_s, n_batch, n_new),
            va_s.reshape(1, n_batch, n_new, a_width),
            cstate_p[None, None], cstate_s.transpose(1, 0, 2)[None],
            pstate_p[None, None], pstate_s.transpose(1, 0, 2)[None])
```

```python
import functools
import math

import jax
import jax.numpy as jnp
from jax import lax
from jax.experimental import pallas as pl
from jax.experimental.pallas import tpu as pltpu

F32 = jnp.float32
BF16 = jnp.bfloat16
I32 = jnp.int32

EPS = 1e-6
NEG = -1e30
CHUNK = 128
A_GROUPS = 8
HEAD_DIM = 64
MOBA_BLOCK = 256
MOBA_TOPK = 3
ROPE_THETA = 10000.0
POOL_WINDOWS = (2, 4, 8, 16)
TOP_K = 2

LANES = 128
VMEM_LIMIT_BYTES = 56 * 1024 * 1024
HALO = 32
MOE_TM = 512
MOBA_UNROLL = 4
KV_PAGES_PER_STEP = 16


def _cparams(sem):
    return pltpu.CompilerParams(dimension_semantics=sem, vmem_limit_bytes=VMEM_LIMIT_BYTES)


def _full(shape):
    n = len(shape)
    return pl.BlockSpec(shape, lambda *a, _n=n: (0,) * _n, pipeline_mode=pl.Buffered(1))


def _whole(shape):
    n = len(shape)
    return pl.BlockSpec(shape, lambda *a, _n=n: (0,) * _n)


def _dot(a, b):
    return jnp.dot(a, b, preferred_element_type=F32)


def _dot_nt(a, b):
    return lax.dot_general(a, b, (((1,), (1,)), ((), ())), preferred_element_type=F32)


def _rms(x, g):
    return x * lax.rsqrt(jnp.mean(x * x, axis=-1, keepdims=True) + EPS) * g


def _layernorm(x, g, b):
    xc = x - jnp.mean(x, axis=-1, keepdims=True)
    var = jnp.mean(xc * xc, axis=-1, keepdims=True)
    return xc * lax.rsqrt(var + EPS) * g + b


def _split_bf16(x):
    hi = x.astype(BF16)
    lo = (x - hi.astype(F32)).astype(BF16)
    return hi, lo


def _lane_pick(x, lane, idx):
    return jnp.sum(jnp.where(lane == idx, x, 0.0), axis=-1, keepdims=True)


def _even_in_kernel(h_ref, g_ref, w_ref, lng_ref, lnb_ref, ws_ref, bias_ref, cos_ref, sin_ref, *outs,
                    seq_l, prompt, a_width, b_width):
    tm = h_ref.shape[0]
    i = pl.program_id(0)
    xn = _rms(h_ref[...], g_ref[...]).astype(BF16)
    aw, bw = a_width, b_width

    u = jax.nn.gelu(_dot(xn, w_ref[:, 0:aw]))
    va = _layernorm(jax.nn.gelu(_dot(xn, w_ref[:, aw:2 * aw])), lng_ref[...], lnb_ref[...])
    vab = va.astype(BF16)

    r = lax.broadcasted_iota(I32, (CHUNK, CHUNK), 0)
    c = lax.broadcasted_iota(I32, (CHUNK, CHUNK), 1)
    sh = int(math.log2(seq_l))
    ok = (lax.shift_right_logical(r, sh) == lax.shift_right_logical(c, sh)) & (c <= r)
    wm = [jnp.where(ok, ws_ref[g], 0.0).astype(BF16) for g in range(A_GROUPS)]
    lane = lax.broadcasted_iota(I32, (CHUNK, LANES), 1)
    gdim = aw // A_GROUPS
    gates = []
    for ci in range(tm // CHUNK):
        vc = vab[ci * CHUNK:(ci + 1) * CHUNK]
        parts = []
        for p in range(aw // LANES):
            vp = vc[:, p * LANES:(p + 1) * LANES]
            r0 = _dot(wm[2 * p], vp)
            r1 = _dot(wm[2 * p + 1], vp)
            parts.append(jnp.where(lane < gdim, r0, r1))
        gates.append(jnp.concatenate(parts, axis=1) + bias_ref[...])
    gate = jnp.concatenate(gates, axis=0)
    a_out = u * gate

    cosv = cos_ref[...]
    sinv = sin_ref[...]
    lane_t = lax.broadcasted_iota(I32, (tm, LANES), 1)
    first_half = (lane_t & (HEAD_DIM - 1)) < HEAD_DIM // 2

    def rope(z):
        parts = []
        for p in range(bw // LANES):
            xs = z[:, p * LANES:(p + 1) * LANES]
            rot = jnp.where(first_half, pltpu.roll(xs, LANES - HEAD_DIM // 2, 1), pltpu.roll(xs, HEAD_DIM // 2, 1))
            parts.append(xs * cosv + rot * sinv)
        return parts

    q_parts = rope(_dot(xn, w_ref[:, 2 * aw:2 * aw + bw]))
    k_parts = rope(_dot(xn, w_ref[:, 2 * aw + bw:2 * aw + 2 * bw]))
    v = _dot(xn, w_ref[:, 2 * aw + 2 * bw:2 * aw + 3 * bw])
    k = jnp.concatenate(k_parts, axis=1)
    q = jnp.concatenate(q_parts, axis=1) * (HEAD_DIM ** -0.5)

    if prompt:
        a_ref, q_ref, k_ref, v_ref, kaug_ref, vaug_ref, kmean_ref = outs
        a_ref[...] = a_out.astype(BF16)
        q_ref[...] = q.astype(BF16)
        k_ref[...] = k
        v_ref[...] = v
        row = lax.broadcasted_iota(I32, (tm, LANES), 0)
        blk = i * (tm // MOBA_BLOCK) + lax.shift_right_logical(row, int(math.log2(MOBA_BLOCK)))
        lo = lane_t < HEAD_DIM
        hot_hi = jnp.where(blk == lane_t - HEAD_DIM, 1.0, 0.0)
        hot_lo = jnp.where(blk == lane_t, 1.0, 0.0)
        for p in range(bw // LANES):
            kp = k_parts[p]
            vp = v[:, p * LANES:(p + 1) * LANES]
            kaug_ref[:, (2 * p) * LANES:(2 * p + 1) * LANES] = jnp.where(lo, kp, hot_hi).astype(BF16)
            kaug_ref[:, (2 * p + 1) * LANES:(2 * p + 2) * LANES] = jnp.where(lo, hot_lo, kp).astype(BF16)
            vaug_ref[:, (2 * p) * LANES:(2 * p + 1) * LANES] = jnp.where(lo, vp, 1.0).astype(BF16)
            vaug_ref[:, (2 * p + 1) * LANES:(2 * p + 2) * LANES] = jnp.where(lo, 1.0, vp).astype(BF16)
        for b in range(tm // MOBA_BLOCK):
            kmean_ref[b] = jnp.mean(k[b * MOBA_BLOCK:(b + 1) * MOBA_BLOCK], axis=0, keepdims=True)
    else:
        a_ref, va_ref, q_ref, k_ref, v_ref = outs
        a_ref[...] = a_out.astype(BF16)
        va_ref[...] = va
        q_ref[...] = q
        k_ref[...] = k
        v_ref[...] = v


def _even_in(h, norm_g, w_in, ln_g, ln_b, ws, bias, cos_t, sin_t, *, tm, seq_l, prompt, a_width, b_width):
    t, d = h.shape
    nw = w_in.shape[1]
    row = lambda i: (i, 0)
    in_specs = [pl.BlockSpec((tm, d), row), _full((1, d)), _full((d, nw)), _full((1, a_width)), _full((1, a_width)),
                _full(ws.shape), _full(bias.shape), pl.BlockSpec((tm, LANES), row), pl.BlockSpec((tm, LANES), row)]
    if prompt:
        nb = t // MOBA_BLOCK
        out_shape = [jax.ShapeDtypeStruct((t, a_width), BF16), jax.ShapeDtypeStruct((t, b_width), BF16),
                     jax.ShapeDtypeStruct((t, b_width), F32), jax.ShapeDtypeStruct((t, b_width), F32),
                     jax.ShapeDtypeStruct((t, 2 * b_width), BF16), jax.ShapeDtypeStruct((t, 2 * b_width), BF16),
                     jax.ShapeDtypeStruct((nb, 1, b_width), F32)]
        out_specs = [pl.BlockSpec((tm, a_width), row), pl.BlockSpec((tm, b_width), row),
                     pl.BlockSpec((tm, b_width), row), pl.BlockSpec((tm, b_width), row),
                     pl.BlockSpec((tm, 2 * b_width), row), pl.BlockSpec((tm, 2 * b_width), row),
                     pl.BlockSpec((tm // MOBA_BLOCK, 1, b_width), lambda i: (i, 0, 0))]
    else:
        out_shape = [jax.ShapeDtypeStruct((t, a_width), BF16), jax.ShapeDtypeStruct((t, a_width), F32),
                     jax.ShapeDtypeStruct((t, b_width), F32), jax.ShapeDtypeStruct((t, b_width), F32),
                     jax.ShapeDtypeStruct((t, b_width), F32)]
        out_specs = [pl.BlockSpec((tm, a_width), row), pl.BlockSpec((tm, a_width), row),
                     pl.BlockSpec((tm, b_width), row), pl.BlockSpec((tm, b_width), row),
                     pl.BlockSpec((tm, b_width), row)]
    return pl.pallas_call(
        functools.partial(_even_in_kernel, seq_l=seq_l, prompt=prompt, a_width=a_width, b_width=b_width),
        grid=(t // tm,), in_specs=in_specs, out_specs=out_specs, out_shape=out_shape,
        compiler_params=_cparams(("arbitrary",)), name="even_in_prompt" if prompt else "even_in_sample",
    )(h, norm_g, w_in, ln_g, ln_b, ws, bias, cos_t, sin_t)


def _moba_prompt_kernel(q_ref, k0_ref, k1_ref, v0_ref, v1_ref, kmh_ref, kml_ref, o_ref):
    tq = q_ref.shape[0]
    i = pl.program_id(1)
    q = q_ref[...]
    qf = q.astype(F32)
    lane = lax.broadcasted_iota(I32, (tq, LANES), 1)
    k_refs = (k0_ref, k1_ref)
    v_refs = (v0_ref, v1_ref)
    neg_inf = jnp.float32(-jnp.inf)

    qa, q_own = [], []
    for e in range(2):
        in_blk = (lane >= HEAD_DIM) if e == 0 else (lane < HEAD_DIM)
        blk = lane - HEAD_DIM if e == 0 else lane
        sc = _dot(q, kmh_ref[0, e]) + _dot(q, kml_ref[0, e])
        cand = jnp.where(in_blk & (blk < i), sc, neg_inf)
        sel = jnp.zeros((tq, LANES), jnp.bool_)
        for _ in range(MOBA_TOPK):
            mx = jnp.max(cand, axis=-1, keepdims=True)
            first = jnp.min(jnp.where(cand == mx, blk, jnp.int32(2 ** 30)), axis=-1, keepdims=True)
            pick = (blk == first) & in_blk & (mx > neg_inf)
            sel = sel | pick
            cand = jnp.where(pick, neg_inf, cand)
        qa.append(jnp.where(in_blk, jnp.where(sel, 0.0, NEG), qf).astype(BF16))
        q_own.append(jnp.where(in_blk, 0.0, qf).astype(BF16))

    def block(e, qe, blk_idx, mask):
        off = pl.multiple_of(blk_idx * MOBA_BLOCK, MOBA_BLOCK)
        s = _dot_nt(qe, k_refs[e][pl.ds(off, MOBA_BLOCK), :])
        if mask is not None:
            s = jnp.where(mask, s, NEG)
        m = jnp.max(s, axis=-1, keepdims=True)
        p = jnp.exp(s - m).astype(BF16)
        return m, _dot(p, v_refs[e][pl.ds(off, MOBA_BLOCK), :])

    r = lax.broadcasted_iota(I32, (tq, MOBA_BLOCK), 0)
    c = lax.broadcasted_iota(I32, (tq, MOBA_BLOCK), 1)
    init = [block(e, q_own[e], i, c <= r) for e in range(2)]
    n_blk = k0_ref.shape[0] // MOBA_BLOCK

    def body(g, carry):
        ms, accs = carry
        new_m, new_acc = [], []
        for e in range(2):
            parts = [block(e, qa[e], jnp.minimum(g * MOBA_UNROLL + u, n_blk - 1), None) for u in range(MOBA_UNROLL)]
            m_new = ms[e]
            for m_u, _ in parts:
                m_new = jnp.maximum(m_new, m_u)
            acc = jnp.exp(ms[e] - m_new) * accs[e]
            for m_u, o_u in parts:
                acc = acc + jnp.exp(m_u - m_new) * o_u
            new_m.append(m_new)
            new_acc.append(acc)
        return tuple(new_m), tuple(new_acc)

    n_groups = lax.shift_right_logical(i + (MOBA_UNROLL - 1), int(math.log2(MOBA_UNROLL)))
    _, accs = lax.fori_loop(0, n_groups, body,
                            (tuple(x[0] for x in init), tuple(x[1] for x in init)))
    outs = [a / pltpu.roll(a, HEAD_DIM, 1) for a in accs]
    o_ref[...] = jnp.where(lane < HEAD_DIM, outs[0], outs[1]).astype(o_ref.dtype)


def _moba_prompt(q, kaug, vaug, km_hi, km_lo):
    t, bw = q.shape
    n_pairs = bw // LANES
    tq = MOBA_BLOCK
    col = lambda e: (lambda p, i, _e=e: (0, 2 * p + _e))
    in_specs = [pl.BlockSpec((tq, LANES), lambda p, i: (i, p)),
                pl.BlockSpec((t, LANES), col(0)), pl.BlockSpec((t, LANES), col(1)),
                pl.BlockSpec((t, LANES), col(0)), pl.BlockSpec((t, LANES), col(1)),
                pl.BlockSpec((1, 2, LANES, LANES), lambda p, i: (p, 0, 0, 0)),
                pl.BlockSpec((1, 2, LANES, LANES), lambda p, i: (p, 0, 0, 0))]
    return pl.pallas_call(
        _moba_prompt_kernel, grid=(n_pairs, t // tq), in_specs=in_specs,
        out_specs=pl.BlockSpec((tq, LANES), lambda p, i: (i, p)),
        out_shape=jax.ShapeDtypeStruct((t, bw), BF16),
        compiler_params=_cparams(("arbitrary", "arbitrary")), name="moba_prompt",
    )(q, kaug, kaug, vaug, vaug, km_hi, km_lo)


def _moba_sample_keys_kernel(pt_ref, *refs, n_steps):
    npg = KV_PAGES_PER_STEP
    pages = refs[:npg]
    qbd_ref, knew_ref, p_ref, pown_ref, linv_ref, s_scr, kmean_scr, bmax_scr = refs[npg:]
    c = pl.program_id(1)
    hw, page = pages[0].shape[1], pages[0].shape[2]
    ppb = MOBA_BLOCK // page
    bps = npg // ppb
    qbd = qbd_ref[0]
    n_rows = qbd.shape[0]
    n_blocks = n_steps * bps
    lane = lax.broadcasted_iota(I32, (n_rows, LANES), 1)
    neg_inf = jnp.float32(-jnp.inf)

    @pl.when(c == 0)
    def _():
        bmax_scr[...] = jnp.full((n_rows, LANES), neg_inf, F32)
        kmean_scr[...] = jnp.zeros_like(kmean_scr)

    lane_k = lax.broadcasted_iota(I32, (hw, LANES), 1)
    for j in range(bps):
        kb = jnp.concatenate([pages[j * ppb + x][0] for x in range(ppb)], axis=1)
        blk = c * bps + j
        kmean_scr[...] = jnp.where(lane_k == blk, jnp.mean(kb, axis=1, keepdims=True), kmean_scr[...])
        s = _dot(qbd, kb.astype(BF16))
        s_scr[:, pl.ds(pl.multiple_of(blk * MOBA_BLOCK, MOBA_BLOCK), MOBA_BLOCK)] = s
        bmax_scr[...] = jnp.where(lane == blk, jnp.max(s, axis=-1, keepdims=True), bmax_scr[...])

    @pl.when(c == n_steps - 1)
    def _():
        kmh, kml = _split_bf16(kmean_scr[...])
        sc = _dot(qbd, kmh) + _dot(qbd, kml)
        cand = jnp.where(lane < n_blocks, sc, neg_inf)
        sel = jnp.zeros((n_rows, LANES), jnp.bool_)
        for _ in range(min(MOBA_TOPK, n_blocks)):
            mx = jnp.max(cand, axis=-1, keepdims=True)
            first = jnp.min(jnp.where(cand == mx, lane, jnp.int32(2 ** 30)), axis=-1, keepdims=True)
            pick = lane == first
            sel = sel | pick
            cand = jnp.where(pick, neg_inf, cand)
        so = _dot_nt(qbd, knew_ref[0].astype(BF16))
        n_new = so.shape[1]
        rr = lax.broadcasted_iota(I32, (n_rows, n_new), 0)
        ss = lax.broadcasted_iota(I32, (n_rows, n_new), 1)
        so = jnp.where(ss <= (rr & (n_new - 1)), so, NEG)
        m = jnp.maximum(jnp.max(jnp.where(sel, bmax_scr[...], neg_inf), axis=-1, keepdims=True),
                        jnp.max(so, axis=-1, keepdims=True))
        po = jnp.exp(so - m)
        l = jnp.sum(po, axis=-1, keepdims=True)
        self32 = sel.astype(F32)
        for b in range(n_blocks):
            on = jnp.sum(jnp.where(lane == b, self32, 0.0), axis=-1, keepdims=True)
            sb = s_scr[:, b * MOBA_BLOCK:(b + 1) * MOBA_BLOCK]
            pb = (jnp.exp(sb - m) * on).astype(BF16)
            l = l + jnp.sum(pb.astype(F32), axis=-1, keepdims=True)
            p_ref[0, :, b * MOBA_BLOCK:(b + 1) * MOBA_BLOCK] = pb
        pown_ref[0] = jnp.concatenate([po, jnp.zeros((n_rows, LANES - n_new), F32)], axis=1)
        linv_ref[0] = jnp.broadcast_to(1.0 / l, (n_rows, LANES))


def _moba_sample_values_kernel(pt_ref, *refs, n_steps, n_heads):
    npg = KV_PAGES_PER_STEP
    pages = refs[:npg]
    p_ref, pown_ref, linv_ref, vnew_ref, o_ref, acc_scr = refs[npg:]
    c = pl.program_id(1)
    vb = jnp.concatenate([pg[0] for pg in pages], axis=1).astype(BF16)
    part = _dot_nt(p_ref[0], vb)

    @pl.when(c == 0)
    def _():
        acc_scr[...] = part

    @pl.when(c > 0)
    def _():
        acc_scr[...] += part

    @pl.when(c == n_steps - 1)
    def _():
        vnew = vnew_ref[0]
        n_new = vnew.shape[0]
        o = (acc_scr[...] + _dot(pown_ref[0][:, :n_new].astype(BF16), vnew.astype(BF16))) * linv_ref[0][:, :1]
        n_rows, hw = o.shape
        rr = lax.broadcasted_iota(I32, (n_rows, hw), 0)
        ll = lax.broadcasted_iota(I32, (n_rows, hw), 1)
        keep = (lax.shift_right_logical(rr, int(math.log2(n_new)))
                == lax.shift_right_logical(ll, int(math.log2(HEAD_DIM))))
        o = jnp.where(keep, o, 0.0).reshape(n_heads, n_new, hw)
        o_ref[0] = jnp.sum(o, axis=0).astype(o_ref.dtype)


def _moba_sample(q, k, v, cache_k, cache_v, page_table, n_batch):
    hw = q.shape[1]
    n_heads = hw // HEAD_DIM
    n_new = q.shape[0] // n_batch
    n_phys, page = cache_k.shape[0], cache_k.shape[1]
    n_pages = page_table.shape[1]
    npg = KV_PAGES_PER_STEP
    assert n_pages % npg == 0 and (n_pages * page) % MOBA_BLOCK == 0 and MOBA_BLOCK % page == 0
    n_steps = n_pages // npg
    past = n_pages * page
    n_blocks = past // MOBA_BLOCK
    assert n_blocks <= LANES and n_new & (n_new - 1) == 0
    n_rows = n_heads * n_new
    ck = cache_k.transpose(0, 2, 3, 1).reshape(n_phys, hw, page)
    cv = cache_v.transpose(0, 2, 3, 1).reshape(n_phys, hw, page)
    pt = page_table.reshape(-1).astype(I32)
    q3 = q.reshape(n_batch, n_new, n_heads, HEAD_DIM)
    eye = jnp.eye(n_heads, dtype=F32)
    qbd = (q3.transpose(0, 2, 1, 3)[:, :, :, None, :] * eye[None, :, None, :, None]).reshape(n_batch, n_rows, hw)
    qbd = qbd.astype(BF16)
    k3 = k.reshape(n_batch, n_new, hw)
    v3 = v.reshape(n_batch, n_new, hw)

    def page_spec(s):
        return pl.BlockSpec((1, hw, page), lambda b, c, pt_ref, _s=s: (pt_ref[b * n_pages + c * npg + _s], 0, 0))

    per_b = lambda shape: pl.BlockSpec(shape, lambda b, c, pt_ref: (b,) + (0,) * (len(shape) - 1))
    p, pown, linv = pl.pallas_call(
        functools.partial(_moba_sample_keys_kernel, n_steps=n_steps),
        grid_spec=pltpu.PrefetchScalarGridSpec(
            num_scalar_prefetch=1, grid=(n_batch, n_steps),
            in_specs=[page_spec(s) for s in range(npg)] + [per_b((1, n_rows, hw)), per_b((1, n_new, hw))],
            out_specs=[per_b((1, n_rows, past)), per_b((1, n_rows, LANES)), per_b((1, n_rows, LANES))],
            scratch_shapes=[pltpu.VMEM((n_rows, past), F32), pltpu.VMEM((hw, LANES), F32),
                            pltpu.VMEM((n_rows, LANES), F32)]),
        out_shape=[jax.ShapeDtypeStruct((n_batch, n_rows, past), BF16),
                   jax.ShapeDtypeStruct((n_batch, n_rows, LANES), F32),
                   jax.ShapeDtypeStruct((n_batch, n_rows, LANES), F32)],
        compiler_params=_cparams(("arbitrary", "arbitrary")), name="moba_sample_keys",
    )(pt, *([ck] * npg), qbd, k3)

    out = pl.pallas_call(
        functools.partial(_moba_sample_values_kernel, n_steps=n_steps, n_heads=n_heads),
        grid_spec=pltpu.PrefetchScalarGridSpec(
            num_scalar_prefetch=1, grid=(n_batch, n_steps),
            in_specs=[page_spec(s) for s in range(npg)]
            + [pl.BlockSpec((1, n_rows, npg * page), lambda b, c, pt_ref: (b, 0, c)),
               per_b((1, n_rows, LANES)), per_b((1, n_rows, LANES)), per_b((1, n_new, hw))],
            out_specs=per_b((1, n_new, hw)),
            scratch_shapes=[pltpu.VMEM((n_rows, hw), F32)]),
        out_shape=jax.ShapeDtypeStruct((n_batch, n_new, hw), BF16),
        compiler_params=_cparams(("arbitrary", "arbitrary")), name="moba_sample_values",
    )(pt, *([cv] * npg), p, pown, linv, v3)
    return out.reshape(n_batch * n_new, hw)


def _ple(h, p_ref, wp_ref, wg_ref, pn_ref):
    gate = jax.nn.sigmoid(_dot(_rms(h, pn_ref[...]).astype(BF16), wg_ref[...]))
    return h + _dot(p_ref[...].astype(BF16), wp_ref[...]) * gate


def _even_tail_kernel(h_ref, a_ref, b_ref, wo_ref, nf_ref, w1_ref, w3_ref, w2_ref, p_ref, wp_ref, wg_ref, pn_ref,
                      nn_ref, h_out_ref, xn_out_ref, *, ff_chunks):
    aw = a_ref.shape[1]
    h1 = h_ref[...] + _dot(a_ref[...], wo_ref[0:aw, :]) + _dot(b_ref[...], wo_ref[aw:, :])
    xn = _rms(h1, nf_ref[...]).astype(BF16)
    ff = w1_ref.shape[1]
    cw = ff // ff_chunks
    hid = []
    for ci in range(ff_chunks):
        sl = slice(ci * cw, (ci + 1) * cw)
        hid.append((jax.nn.silu(_dot(xn, w1_ref[:, sl])) * _dot(xn, w3_ref[:, sl])).astype(BF16))
    acc = h1 + _dot(jnp.concatenate(hid, axis=1), w2_ref[...])
    h3 = _ple(acc, p_ref, wp_ref, wg_ref, pn_ref)
    h_out_ref[...] = h3
    xn_out_ref[...] = _rms(h3, nn_ref[...]).astype(BF16)


def _even_tail(h, a, b, w_out, norm_ffn, w1, w3, w2, p, wp, wg, pn, next_norm, *, tm):
    t, d = h.shape
    ff = w1.shape[1]
    ff_chunks = next(n for n in (4, 2, 1, 11, 22) if ff % (n * LANES) == 0) if ff % LANES == 0 else 1
    row = lambda i: (i, 0)
    in_specs = [pl.BlockSpec((tm, d), row), pl.BlockSpec((tm, a.shape[1]), row), pl.BlockSpec((tm, b.shape[1]), row),
                _full(w_out.shape), _full((1, d)), _full(w1.shape), _full(w3.shape), _full(w2.shape),
                pl.BlockSpec((tm, p.shape[1]), row), _full(wp.shape), _full(wg.shape), _full((1, d)), _full((1, d))]
    return pl.pallas_call(
        functools.partial(_even_tail_kernel, ff_chunks=ff_chunks),
        grid=(t // tm,), in_specs=in_specs,
        out_specs=[pl.BlockSpec((tm, d), row), pl.BlockSpec((tm, d), row)],
        out_shape=[jax.ShapeDtypeStruct((t, d), F32), jax.ShapeDtypeStruct((t, d), BF16)],
        compiler_params=_cparams(("arbitrary",)), name="even_tail",
    )(h, a, b, w_out, norm_ffn, w1, w3, w2, p, wp, wg, pn, next_norm)


def _odd_tail(conv, d_in, h3, refs, count_scr, n_experts):
    (cb_ref, clg_ref, clb_ref, pw_ref, ps_ref, wo_ref, nf_ref, rwh_ref, rwl_ref, rb_ref) = refs
    rows = conv.shape[0]
    cw = conv.shape[1]
    cact = jax.nn.silu(_layernorm(conv + cb_ref[...], clg_ref[...], clb_ref[...]))
    db = d_in.astype(BF16)
    gdim = d_in.shape[1] // len(POOL_WINDOWS)
    dparts = [_dot(db[:, g * gdim:(g + 1) * gdim], pw_ref[g]) for g in range(len(POOL_WINDOWS))]
    dmix = jnp.concatenate(dparts, axis=1) * ps_ref[...]
    h4 = h3 + _dot(cact.astype(BF16), wo_ref[0:cw, :]) + _dot(dmix.astype(BF16), wo_ref[cw:, :])
    xn = _rms(h4, nf_ref[...])
    xh, xl = _split_bf16(xn)
    logits = _dot(xh, rwh_ref[...]) + _dot(xl, rwh_ref[...]) + _dot(xh, rwl_ref[...]) + rb_ref[...]
    lane = lax.broadcasted_iota(I32, (rows, LANES), 1)
    neg_inf = jnp.float32(-jnp.inf)
    cand = jnp.where(lane < n_experts, logits, neg_inf)
    v1 = jnp.max(cand, axis=-1, keepdims=True)
    e1 = jnp.min(jnp.where(cand == v1, lane, jnp.int32(2 ** 30)), axis=-1, keepdims=True)
    cand2 = jnp.where(lane == e1, neg_inf, cand)
    v2 = jnp.max(cand2, axis=-1, keepdims=True)
    e2 = jnp.min(jnp.where(cand2 == v2, lane, jnp.int32(2 ** 30)), axis=-1, keepdims=True)
    ex = jnp.exp(v2 - v1)
    g1 = 1.0 / (1.0 + ex)
    g2 = ex / (1.0 + ex)
    chosen = jnp.where((lane == e1) | (lane == e2), 1.0, 0.0).astype(BF16)
    rr = lax.broadcasted_iota(I32, (rows, rows), 0)
    cc = lax.broadcasted_iota(I32, (rows, rows), 1)
    before = jnp.where(cc < rr, 1.0, 0.0).astype(BF16)
    prefix = _dot(before, chosen) + count_scr[...]
    r1 = _lane_pick(prefix, lane, e1)
    r2 = _lane_pick(prefix, lane, e2)
    count_scr[...] = count_scr[...] + jnp.sum(chosen.astype(F32), axis=0, keepdims=True)
    meta = jnp.zeros((rows, LANES), F32)
    for idx, val in enumerate((e1.astype(F32), e2.astype(F32), g1, g2, r1, r2)):
        meta = jnp.where(lane == idx, val, meta)
    return h4, xn, meta


def _odd_prompt_kernel(xm_ref, xh_ref, h3_ref, wi_ref, cw_ref, *refs, n_experts, conv_hist, pool_hist):
    tail_refs = refs[:10]
    cnt_in_ref, h4_ref, xn_ref, meta_ref, cnt_ref, cstate_ref, pstate_ref, count_scr = refs[10:]
    i = pl.program_id(0)
    tm = xm_ref.shape[0]
    cwid = cw_ref.shape[1]

    @pl.when(i == 0)
    def _():
        count_scr[...] = cnt_in_ref[...]

    x = jnp.concatenate([xh_ref[...], xm_ref[...]], axis=0)
    z = _dot(x, wi_ref[...])
    row = lax.broadcasted_iota(I32, (HALO + tm, 1), 0)
    z = jnp.where((row >= HALO) | (i > 0), z, 0.0)
    c_in = z[:, :cwid] * jax.nn.sigmoid(z[:, cwid:2 * cwid])
    dx = z[:, 2 * cwid:]

    cwv = cw_ref[...]
    width = cwv.shape[0]
    lead = HALO - (width - 1)

    def tap(kp):
        k = kp - lead
        return cwv[k:k + 1, :] if 0 <= k < width else None

    ext = c_in
    conv = None
    for b in range(8):
        n_a = (HALO // 8 + 1) if b == 0 else HALO // 8
        rows_b = tm if b == 0 else tm + 8
        ub = None
        for a in range(n_a):
            w = tap(8 * a + b)
            if w is None:
                continue
            term = ext[8 * a:8 * a + rows_b] * w
            ub = term if ub is None else ub + term
        if ub is None:
            continue
        piece = ub if b == 0 else pltpu.roll(ub, rows_b - b, 0)[:tm]
        conv = piece if conv is None else conv + piece

    s = dx
    sums = {}
    span = 1
    while span < max(POOL_WINDOWS):
        s = s + pltpu.roll(s, span, 0)
        span *= 2
        sums[span] = s
    gdim = dx.shape[1] // len(POOL_WINDOWS)
    pos = i * tm + lax.broadcasted_iota(I32, (tm, 1), 0)
    pooled = []
    for g, w in enumerate(POOL_WINDOWS):
        win = sums[w][HALO:, g * gdim:(g + 1) * gdim]
        cnt = jnp.minimum(pos + 1, w).astype(F32)
        pooled.append(win / cnt)
    dmain = dx[HALO:]
    d_in = jnp.concatenate(pooled, axis=1) - dmain

    h4, xn, meta = _odd_tail(conv, d_in, h3_ref[...], tail_refs, count_scr, n_experts)
    h4_ref[...] = h4
    _to_row_tiles(xn_ref, xn)
    meta_ref[...] = meta
    cnt_ref[...] = count_scr[...]
    ctail = c_in[tm:]
    cstate_ref[...] = pltpu.roll(ctail, conv_hist, 0)[:conv_hist] if conv_hist < HALO else ctail
    ptail_rows = 8 * (-(-pool_hist // 8))
    ptail = dmain[tm - ptail_rows:]
    pstate_ref[...] = pltpu.roll(ptail, pool_hist, 0)[:pool_hist] if pool_hist < ptail_rows else ptail


def _odd_sample_kernel(x_ref, h3_ref, wi_ref, cw_ref, *refs, n_experts, n_batch):
    tail_refs = refs[:10]
    cnt_in_ref, chist_ref, phist_ref, h4_ref, xn_ref, meta_ref, cnt_ref, cstate_ref, pstate_ref, count_scr = refs[10:]
    rows = x_ref.shape[0]
    n_new = rows // n_batch
    cwid = cw_ref.shape[1]
    count_scr[...] = cnt_in_ref[...]
    z = _dot(x_ref[...], wi_ref[...])
    c_in = z[:, :cwid] * jax.nn.sigmoid(z[:, cwid:2 * cwid])
    dx = z[:, 2 * cwid:]
    cwv = cw_ref[...]
    width = cwv.shape[0]
    hist_c = chist_ref[...]
    xp = [hist_c[j] for j in range(width - 1)] + [c_in[t * n_batch:(t + 1) * n_batch] for t in range(n_new)]
    conv = []
    for t in range(n_new):
        acc = xp[t] * cwv[0:1, :]
        for k in range(1, width):
            acc = acc + xp[t + k] * cwv[k:k + 1, :]
        conv.append(acc)
    conv = jnp.concatenate(conv, axis=0)
    cstate_ref[...] = jnp.stack(xp[n_new:], axis=0)

    hist_p = phist_ref[...]
    n_ph = hist_p.shape[0]
    xq = [hist_p[j] for j in range(n_ph)] + [dx[t * n_batch:(t + 1) * n_batch] for t in range(n_new)]
    gdim = dx.shape[1] // len(POOL_WINDOWS)
    lane = lax.broadcasted_iota(I32, (n_batch, dx.shape[1]), 1)
    d_in = []
    for t in range(n_new):
        run = xq[n_ph + t]
        tot = jnp.zeros_like(run)
        done = 1
        for g, w in enumerate(POOL_WINDOWS):
            for j in range(done, w):
                run = run + xq[n_ph + t - j]
            done = w
            in_g = (lane >= g * gdim) & (lane < (g + 1) * gdim)
            tot = jnp.where(in_g, run / float(w), tot)
        d_in.append(tot - xq[n_ph + t])
    d_in = jnp.concatenate(d_in, axis=0)
    pstate_ref[...] = jnp.stack(xq[n_new:], axis=0)

    h4, xn, meta = _odd_tail(conv, d_in, h3_ref[...], tail_refs, count_scr, n_experts)
    h4_ref[...] = h4
    _to_row_tiles(xn_ref, xn)
    meta_ref[...] = meta
    cnt_ref[...] = count_scr[...]


def _odd_common_args(conv_b, c_ln_g, c_ln_b, pool_w, pool_scale, w_out, norm_ffn, rw_hi, rw_lo, rb):
    args = (conv_b, c_ln_g, c_ln_b, pool_w, pool_scale, w_out, norm_ffn, rw_hi, rw_lo, rb)
    return args, [_full(a.shape) for a in args]


def _odd_prompt(xn3, h3, w_in, conv_w, common, counts_in, *, tm, n_experts):
    t, d = h3.shape
    cwid = conv_w.shape[1]
    conv_hist = conv_w.shape[0] - 1
    pool_hist = max(POOL_WINDOWS) - 1
    args, specs = common
    row = lambda i: (i, 0)
    hpb = tm // HALO
    in_specs = [pl.BlockSpec((tm, d), row), pl.BlockSpec((HALO, d), lambda i: (jnp.maximum(i * hpb - 1, 0), 0)),
                pl.BlockSpec((tm, d), row), _full(w_in.shape), _full(conv_w.shape)] + specs + [_full((1, LANES))]
    dwid = w_in.shape[1] - 2 * cwid
    return pl.pallas_call(
        functools.partial(_odd_prompt_kernel, n_experts=n_experts, conv_hist=conv_hist, pool_hist=pool_hist),
        grid=(t // tm,), in_specs=in_specs,
        out_specs=[pl.BlockSpec((tm, d), row), pl.BlockSpec((tm, d // LANES, LANES), lambda i: (i, 0, 0)),
                   pl.BlockSpec((tm, LANES), row),
                   _whole((1, LANES)), _whole((conv_hist, cwid)), _whole((pool_hist, dwid))],
        out_shape=[jax.ShapeDtypeStruct((t, d), F32), jax.ShapeDtypeStruct((t, d // LANES, LANES), F32),
                   jax.ShapeDtypeStruct((t, LANES), F32), jax.ShapeDtypeStruct((1, LANES), F32),
                   jax.ShapeDtypeStruct((conv_hist, cwid), F32), jax.ShapeDtypeStruct((pool_hist, dwid), F32)],
        scratch_shapes=[pltpu.VMEM((1, LANES), F32)],
        compiler_params=_cparams(("arbitrary",)), name="odd_prompt",
    )(xn3, xn3, h3, w_in, conv_w, *args, counts_in)


def _odd_sample(xn3, h3, w_in, conv_w, common, counts_in, conv_hist_t, pool_hist_t, *, n_batch, n_experts):
    t, d = h3.shape
    cwid = conv_w.shape[1]
    dwid = w_in.shape[1] - 2 * cwid
    args, specs = common
    in_specs = [_full((t, d)), _full((t, d)), _full(w_in.shape), _full(conv_w.shape)] + specs + \
               [_full((1, LANES)), _full(conv_hist_t.shape), _full(pool_hist_t.shape)]
    return pl.pallas_call(
        functools.partial(_odd_sample_kernel, n_experts=n_experts, n_batch=n_batch),
        grid=(1,), in_specs=in_specs,
        out_specs=[_whole((t, d)), _whole((t, d // LANES, LANES)), _whole((t, LANES)), _whole((1, LANES)),
                   _whole(conv_hist_t.shape), _whole(pool_hist_t.shape)],
        out_shape=[jax.ShapeDtypeStruct((t, d), F32), jax.ShapeDtypeStruct((t, d // LANES, LANES), F32),
                   jax.ShapeDtypeStruct((t, LANES), F32), jax.ShapeDtypeStruct((1, LANES), F32),
                   jax.ShapeDtypeStruct(conv_hist_t.shape, F32), jax.ShapeDtypeStruct(pool_hist_t.shape, F32)],
        scratch_shapes=[pltpu.VMEM((1, LANES), F32)],
        compiler_params=_cparams(("arbitrary",)), name="odd_sample",
    )(xn3, h3, w_in, conv_w, *args, counts_in, conv_hist_t, pool_hist_t)


def _row_copy(src, s, dst, d, sem):
    return pltpu.make_async_copy(src.at[pl.ds(s, 1)], dst.at[pl.ds(d, 1)], sem)


def _rows_wait(ref, n, sem):
    pltpu.make_async_copy(ref.at[pl.ds(0, n)], ref.at[pl.ds(0, n)], sem).wait()


def _to_row_tiles(ref, x):
    for s in range(ref.shape[1]):
        ref[:, s, :] = x[:, s * LANES:(s + 1) * LANES]


def _from_row_tiles(ref):
    return jnp.concatenate([ref[:, s, :] for s in range(ref.shape[1])], axis=1)


def _moe_scatter_kernel(dest_ref, x_hbm, buf_in_hbm, buf_hbm, sem, *, tm):
    del buf_in_hbm
    base = pl.program_id(0) * tm

    def start(r, carry):
        for k in range(TOP_K):
            _row_copy(x_hbm, base + r, buf_hbm, dest_ref[(base + r) * TOP_K + k], sem).start()
        return carry

    lax.fori_loop(0, tm, start, 0)
    _rows_wait(buf_hbm, tm * TOP_K, sem)


def _moe_scatter(dest, x, buf, *, tm):
    t = x.shape[0]
    any_spec = pl.BlockSpec(memory_space=pl.ANY)
    return pl.pallas_call(
        functools.partial(_moe_scatter_kernel, tm=tm),
        grid_spec=pltpu.PrefetchScalarGridSpec(
            num_scalar_prefetch=1, grid=(t // tm,), in_specs=[any_spec, any_spec], out_specs=any_spec,
            scratch_shapes=[pltpu.SemaphoreType.DMA(())]),
        out_shape=jax.ShapeDtypeStruct(buf.shape, buf.dtype),
        input_output_aliases={2: 0},
        compiler_params=_cparams(("arbitrary",)), name="moe_scatter",
    )(dest, x, buf)


def _moe_experts_kernel(be_ref, nu_ref, x_ref, w1_ref, w3_ref, w2_ref, y_ref, xb_scr, acc_scr, *, nj):
    i = pl.program_id(0)
    j = pl.program_id(1)
    used = i < nu_ref[0]

    @pl.when(used)
    def _():
        @pl.when(j == 0)
        def _():
            xb_scr[...] = _from_row_tiles(x_ref).astype(BF16)

        xb = xb_scr[...]
        hid = jax.nn.silu(_dot(xb, w1_ref[0])) * _dot(xb, w3_ref[0])
        part = _dot(hid.astype(BF16), w2_ref[0])

        @pl.when(j == 0)
        def _():
            acc_scr[...] = part

        @pl.when(j > 0)
        def _():
            acc_scr[...] += part

        @pl.when(j == nj - 1)
        def _():
            _to_row_tiles(y_ref, acc_scr[...])

    @pl.when(jnp.logical_not(used) & (j == nj - 1))
    def _():
        y_ref[...] = jnp.zeros_like(y_ref)


def _moe_experts(blk_e, n_used, xbuf, w1, w3, w2, *, tm, tf):
    rows, ns, _ = xbuf.shape
    d = ns * LANES
    ff = w1.shape[2]
    nj = ff // tf
    nb = rows // tm
    live_j = lambda i, j, nu: jnp.where(i < nu[0], j, nj - 1)
    return pl.pallas_call(
        functools.partial(_moe_experts_kernel, nj=nj),
        grid_spec=pltpu.PrefetchScalarGridSpec(
            num_scalar_prefetch=2, grid=(nb, nj),
            in_specs=[pl.BlockSpec((tm, ns, LANES), lambda i, j, be, nu: (i, 0, 0)),
                      pl.BlockSpec((1, d, tf), lambda i, j, be, nu: (be[i], 0, live_j(i, j, nu))),
                      pl.BlockSpec((1, d, tf), lambda i, j, be, nu: (be[i], 0, live_j(i, j, nu))),
                      pl.BlockSpec((1, tf, d), lambda i, j, be, nu: (be[i], live_j(i, j, nu), 0))],
            out_specs=pl.BlockSpec((tm, ns, LANES), lambda i, j, be, nu: (i, 0, 0)),
            scratch_shapes=[pltpu.VMEM((tm, d), BF16), pltpu.VMEM((tm, d), F32)]),
        out_shape=jax.ShapeDtypeStruct((rows, ns, LANES), F32),
        compiler_params=_cparams(("arbitrary", "arbitrary")), name="moe_experts",
    )(blk_e, n_used, xbuf, w1, w3, w2)


def _moe_combine_kernel(dest_ref, y_hbm, h4_ref, meta_ref, p_ref, wp_ref, wg_ref, pn_ref, fn_ref, o_ref,
                        y0_scr, y1_scr, sem, *, tm):
    base = pl.program_id(0) * tm
    bufs = (y0_scr, y1_scr)

    def start(r, carry):
        for k in range(TOP_K):
            _row_copy(y_hbm, dest_ref[(base + r) * TOP_K + k], bufs[k], r, sem).start()
        return carry

    lax.fori_loop(0, tm, start, 0)
    for k in range(TOP_K):
        _rows_wait(bufs[k], tm, sem)
    meta = meta_ref[...]
    lane = lax.broadcasted_iota(I32, meta.shape, 1)
    g1 = jnp.sum(jnp.where(lane == 2, meta, 0.0), axis=-1, keepdims=True)
    g2 = jnp.sum(jnp.where(lane == 3, meta, 0.0), axis=-1, keepdims=True)
    h5 = h4_ref[...] + _from_row_tiles(y0_scr) * g1 + _from_row_tiles(y1_scr) * g2
    h6 = _ple(h5, p_ref, wp_ref, wg_ref, pn_ref)
    o_ref[...] = _rms(h6, fn_ref[...])


def _moe_combine(dest, ybuf, h4, meta, p, wp, wg, pn, fn, *, tm):
    t, d = h4.shape
    row = lambda i, dref: (i, 0)
    full = lambda shape: pl.BlockSpec(shape, lambda i, dref, _n=len(shape): (0,) * _n)
    return pl.pallas_call(
        functools.partial(_moe_combine_kernel, tm=tm),
        grid_spec=pltpu.PrefetchScalarGridSpec(
            num_scalar_prefetch=1, grid=(t // tm,),
            in_specs=[pl.BlockSpec(memory_space=pl.ANY), pl.BlockSpec((tm, d), row), pl.BlockSpec((tm, LANES), row),
                      pl.BlockSpec((tm, p.shape[1]), row), full(wp.shape), full(wg.shape), full((1, d)), full((1, d))],
            out_specs=pl.BlockSpec((tm, d), row),
            scratch_shapes=[pltpu.VMEM((tm, d // LANES, LANES), F32), pltpu.VMEM((tm, d // LANES, LANES), F32),
                            pltpu.SemaphoreType.DMA(())]),
        out_shape=jax.ShapeDtypeStruct((t, d), F32),
        compiler_params=_cparams(("arbitrary",)), name="moe_combine",
    )(dest, ybuf, h4, meta, p, wp, wg, pn, fn)


def _rope_tables(pos):
    half = HEAD_DIM // 2
    inv = ROPE_THETA ** (-jnp.arange(half, dtype=F32) * (2.0 / HEAD_DIM))
    ang = pos.astype(F32)[:, None] * inv[None, :]
    cos, sin = jnp.cos(ang), jnp.sin(ang)
    reps = LANES // HEAD_DIM
    cos_t = jnp.tile(jnp.concatenate([cos, cos], axis=1), (1, reps))
    sin_t = jnp.tile(jnp.concatenate([-sin, sin], axis=1), (1, reps))
    return cos_t, sin_t


def _row_tile(t, pref):
    for tm in (pref, 256, 128, 64, 32, 16, 8):
        if tm <= t and t % tm == 0:
            return tm
    return t


def kernel(x_prompt, x_sample, cache_k, cache_v, state_conv, state_pool, page_table, p_prompt, p_sample, norm_mix_e, w_in_e, a_ln_g, a_ln_b, a_ws, a_bs, w_out_e, norm_ffn_e, ffn_w1, ffn_w3, ffn_w2, norm_mix_o, w_in_o, conv_w, conv_b, c_ln_g, c_ln_b, pool_w, pool_scale, w_out_o, norm_ffn_o, router_w, router_b, exp_w1, exp_w3, exp_w2, ple_w, ple_gate_w, ple_norm, final_norm):
    assert x_prompt.shape[0] == 1 and norm_mix_e.shape[0] == 1 and norm_mix_o.shape[0] == 1 and ple_w.shape[0] == 2
    n_tok, d = x_prompt.shape[1], x_prompt.shape[2]
    n_batch, n_new = x_sample.shape[0], x_sample.shape[1]
    n_s = n_batch * n_new
    a_width = a_ln_g.shape[1]
    b_width = (w_in_e.shape[2] - 2 * a_width) // 3
    n_heads = b_width // HEAD_DIM
    gdim = a_width // A_GROUPS
    n_experts = router_w.shape[2]
    page = cache_k.shape[2]
    past_len = page_table.shape[1] * page
    assert n_tok % MOBA_BLOCK == 0 and n_tok // MOBA_BLOCK <= HEAD_DIM and past_len % MOBA_BLOCK == 0
    assert n_new <= CHUNK and CHUNK % n_new == 0 and n_s % CHUNK == 0

    bf = lambda w: w.astype(BF16)
    row2 = lambda v: v.reshape(1, -1)

    w_in = bf(w_in_e[0])
    ws_p = a_ws[0]
    bias_p = jnp.repeat(a_bs[0].T, gdim, axis=1)
    reps = CHUNK // n_new
    ws_s = jnp.tile(a_ws[0][:, :n_new, :n_new], (1, reps, reps))
    bias_s = jnp.tile(jnp.repeat(a_bs[0][:, :n_new].T, gdim, axis=1), (reps, 1))
    cos_p, sin_p = _rope_tables(jnp.arange(n_tok, dtype=I32))
    cos_s, sin_s = _rope_tables(past_len + (jnp.arange(n_s, dtype=I32) % n_new))
    hp0 = x_prompt[0]
    hs0 = x_sample.reshape(n_s, d)
    ev = dict(a_width=a_width, b_width=b_width)
    tm_p = _row_tile(n_tok, 512)
    a_p, q_p, k_p, v_p, kaug, vaug, kmean = _even_in(
        hp0, row2(norm_mix_e[0]), w_in, row2(a_ln_g[0]), row2(a_ln_b[0]), ws_p, bias_p, cos_p, sin_p,
        tm=tm_p, seq_l=CHUNK, prompt=True, **ev)
    a_s, va_s, q_s, k_s, v_s = _even_in(
        hs0, row2(norm_mix_e[0]), w_in, row2(a_ln_g[0]), row2(a_ln_b[0]), ws_s, bias_s, cos_s, sin_s,
        tm=_row_tile(n_s, 256), seq_l=n_new, prompt=False, **ev)

    n_blk = n_tok // MOBA_BLOCK
    km = kmean.reshape(n_blk, n_heads // 2, 2, HEAD_DIM)
    km_t = jnp.zeros((n_heads // 2, 2, LANES, LANES), F32)
    km_t = km_t.at[:, 0, :HEAD_DIM, HEAD_DIM:HEAD_DIM + n_blk].set(km[:, :, 0].transpose(1, 2, 0))
    km_t = km_t.at[:, 1, HEAD_DIM:, :n_blk].set(km[:, :, 1].transpose(1, 2, 0))
    km_hi = km_t.astype(BF16)
    km_lo = (km_t - km_hi.astype(F32)).astype(BF16)
    attn_p = _moba_prompt(q_p, kaug, vaug, km_hi, km_lo)
    attn_s = _moba_sample(q_s, k_s, v_s, cache_k[0], cache_v[0], page_table, n_batch)

    tail = (bf(w_out_e[0]), row2(norm_ffn_e[0]), bf(ffn_w1[0]), bf(ffn_w3[0]), bf(ffn_w2[0]))
    ple0 = (bf(ple_w[0]), bf(ple_gate_w[0]), row2(ple_norm[0]), row2(norm_mix_o[0]))
    h3_p, xn3_p = _even_tail(hp0, a_p, attn_p, *tail, p_prompt[0, 0], *ple0, tm=tm_p)
    h3_s, xn3_s = _even_tail(hs0, a_s, attn_s, *tail, p_sample[0].reshape(n_s, -1), *ple0, tm=_row_tile(n_s, 256))

    def time_major(x):
        return x.reshape(n_batch, n_new, -1).transpose(1, 0, 2).reshape(n_s, -1)

    rw = jnp.zeros((d, LANES), F32).at[:, :n_experts].set(router_w[0])
    rw_hi = rw.astype(BF16)
    rw_lo = (rw - rw_hi.astype(F32)).astype(BF16)
    rb = jnp.zeros((1, LANES), F32).at[0, :n_experts].set(router_b[0])
    common = _odd_common_args(row2(conv_b[0]), row2(c_ln_g[0]), row2(c_ln_b[0]), bf(pool_w[0]), row2(pool_scale[0]),
                              bf(w_out_o[0]), row2(norm_ffn_o[0]), rw_hi, rw_lo, rb)
    w_in1 = bf(w_in_o[0])
    zero_counts = jnp.zeros((1, LANES), F32)
    h4_p, xn5_p, meta_p, cnt_p, cstate_p, pstate_p = _odd_prompt(
        xn3_p, h3_p, w_in1, conv_w[0], common, zero_counts, tm=tm_p, n_experts=n_experts)
    h4_s, xn5_s, meta_s, cnt_all, cstate_s, pstate_s = _odd_sample(
        time_major(xn3_s), time_major(h3_s), w_in1, conv_w[0], common, cnt_p,
        state_conv[0].transpose(1, 0, 2), state_pool[0].transpose(1, 0, 2), n_batch=n_batch, n_experts=n_experts)

    counts = cnt_all[0, :n_experts].astype(I32)
    padded = (counts + MOE_TM - 1) // MOE_TM * MOE_TM
    pends = jnp.cumsum(padded)
    pstarts = (pends - padded).astype(F32)
    n_rows_all = (n_tok + n_s) * TOP_K
    n_blocks = -(-n_rows_all // MOE_TM) + n_experts
    blk_row0 = jnp.arange(n_blocks, dtype=I32) * MOE_TM
    blk_e = jnp.minimum(jnp.sum((pends[None, :] <= blk_row0[:, None]).astype(I32), axis=1), n_experts - 1)
    n_used = (pends[-1] // MOE_TM).astype(I32).reshape(1)
    last_e = blk_e[jnp.maximum(n_used[0] - 1, 0)]
    blk_e = jnp.where(jnp.arange(n_blocks) < n_used[0], blk_e, last_e)

    def dests(meta):
        e = meta[:, :TOP_K].astype(I32)
        return (pstarts[e] + meta[:, 4:4 + TOP_K]).astype(I32).reshape(-1)

    dest_p, dest_s = dests(meta_p), dests(meta_s)
    xbuf = jnp.zeros((n_blocks * MOE_TM, d // LANES, LANES), F32)
    xbuf = _moe_scatter(dest_p, xn5_p, xbuf, tm=_row_tile(n_tok, 512))
    xbuf = _moe_scatter(dest_s, xn5_s, xbuf, tm=_row_tile(n_s, 256))
    ff_e = exp_w1.shape[3]
    tf = next(c for c in (896, 512, 256, 128, ff_e) if ff_e % c == 0)
    ybuf = _moe_experts(blk_e, n_used, xbuf, bf(exp_w1[0]), bf(exp_w3[0]), bf(exp_w2[0]), tm=MOE_TM, tf=tf)
    ple1 = (bf(ple_w[1]), bf(ple_gate_w[1]), row2(ple_norm[1]), row2(final_norm))
    y_p = _moe_combine(dest_p, ybuf, h4_p, meta_p, p_prompt[1, 0], *ple1, tm=_row_tile(n_tok, 256))
    y_s = _moe_combine(dest_s, ybuf, h4_s, meta_s, time_major(p_sample[1].reshape(n_s, -1)), *ple1,
                       tm=_row_tile(n_s, 256))

    def batch_major(x):
        return x.reshape(n_new, n_batch, -1).transpose(1, 0, 2)

    kv = lambda x, b, l: x.reshape(1, b, l, n_heads, HEAD_DIM)
    return (y_p[None], batch_major(y_s),
            kv(k_p, 1, n_tok), kv(v_p, 1, n_tok), kv(k_s, n_batch, n_new), kv(v_s, n_batch, n_new),
            va_s.reshape(1, n_batch, n_new, a_width),
            cstate_p[None, None], cstate_s.transpose(1, 0, 2)[None],
            pstate_p[None, None], pstate_s.transpose(1, 0, 2)[None])
```

```python
import functools
import math

import jax
import jax.numpy as jnp
from jax import lax
from jax.experimental import pallas as pl
from jax.experimental.pallas import tpu as pltpu

F32 = jnp.float32
BF16 = jnp.bfloat16
I32 = jnp.int32

EPS = 1e-6
NEG = -1e30
LOG2E = 1.4426950408889634
CHUNK = 128
A_GROUPS = 8
HEAD_DIM = 64
MOBA_BLOCK = 256
MOBA_TOPK = 3
ROPE_THETA = 10000.0
POOL_WINDOWS = (2, 4, 8, 16)
TOP_K = 2

LANES = 128
VMEM_LIMIT_BYTES = 56 * 1024 * 1024
HALO = 32
MOE_TM = 512
MOBA_UNROLL = 4
KV_PAGES_PER_STEP = 16


def _cparams(sem):
    return pltpu.CompilerParams(dimension_semantics=sem, vmem_limit_bytes=VMEM_LIMIT_BYTES)


def _full(shape):
    n = len(shape)
    return pl.BlockSpec(shape, lambda *a, _n=n: (0,) * _n, pipeline_mode=pl.Buffered(1))


def _whole(shape):
    n = len(shape)
    return pl.BlockSpec(shape, lambda *a, _n=n: (0,) * _n)


def _dot(a, b):
    return jnp.dot(a, b, preferred_element_type=F32)


def _dot_nt(a, b):
    return lax.dot_general(a, b, (((1,), (1,)), ((), ())), preferred_element_type=F32)


def _rms(x, g):
    return x * lax.rsqrt(jnp.mean(x * x, axis=-1, keepdims=True) + EPS) * g


def _layernorm(x, g, b):
    xc = x - jnp.mean(x, axis=-1, keepdims=True)
    var = jnp.mean(xc * xc, axis=-1, keepdims=True)
    return xc * lax.rsqrt(var + EPS) * g + b


def _split_bf16(x):
    hi = x.astype(BF16)
    lo = (x - hi.astype(F32)).astype(BF16)
    return hi, lo


def _lane_pick(x, lane, idx):
    return jnp.sum(jnp.where(lane == idx, x, 0.0), axis=-1, keepdims=True)


def _even_in_kernel(h_ref, g_ref, w_ref, lng_ref, lnb_ref, ws_ref, bias_ref, cos_ref, sin_ref, *outs,
                    seq_l, prompt, a_width, b_width):
    tm = h_ref.shape[0]
    i = pl.program_id(0)
    xn = _rms(h_ref[...], g_ref[...]).astype(BF16)
    aw, bw = a_width, b_width

    u = jax.nn.gelu(_dot(xn, w_ref[:, 0:aw]))
    va = _layernorm(jax.nn.gelu(_dot(xn, w_ref[:, aw:2 * aw])), lng_ref[...], lnb_ref[...])
    vab = va.astype(BF16)

    r = lax.broadcasted_iota(I32, (CHUNK, CHUNK), 0)
    c = lax.broadcasted_iota(I32, (CHUNK, CHUNK), 1)
    sh = int(math.log2(seq_l))
    ok = (lax.shift_right_logical(r, sh) == lax.shift_right_logical(c, sh)) & (c <= r)
    wm = [jnp.where(ok, ws_ref[g], 0.0).astype(BF16) for g in range(A_GROUPS)]
    lane = lax.broadcasted_iota(I32, (CHUNK, LANES), 1)
    gdim = aw // A_GROUPS
    gates = []
    for ci in range(tm // CHUNK):
        vc = vab[ci * CHUNK:(ci + 1) * CHUNK]
        parts = []
        for p in range(aw // LANES):
            vp = vc[:, p * LANES:(p + 1) * LANES]
            r0 = _dot(wm[2 * p], vp)
            r1 = _dot(wm[2 * p + 1], vp)
            parts.append(jnp.where(lane < gdim, r0, r1))
        gates.append(jnp.concatenate(parts, axis=1) + bias_ref[...])
    gate = jnp.concatenate(gates, axis=0)
    a_out = u * gate

    cosv = cos_ref[...]
    sinv = sin_ref[...]
    lane_t = lax.broadcasted_iota(I32, (tm, LANES), 1)
    first_half = (lane_t & (HEAD_DIM - 1)) < HEAD_DIM // 2

    def rope(z):
        parts = []
        for p in range(bw // LANES):
            xs = z[:, p * LANES:(p + 1) * LANES]
            rot = jnp.where(first_half, pltpu.roll(xs, LANES - HEAD_DIM // 2, 1), pltpu.roll(xs, HEAD_DIM // 2, 1))
            parts.append(xs * cosv + rot * sinv)
        return parts

    q_parts = rope(_dot(xn, w_ref[:, 2 * aw:2 * aw + bw]))
    k_parts = rope(_dot(xn, w_ref[:, 2 * aw + bw:2 * aw + 2 * bw]))
    v = _dot(xn, w_ref[:, 2 * aw + 2 * bw:2 * aw + 3 * bw])
    k = jnp.concatenate(k_parts, axis=1)
    q = jnp.concatenate(q_parts, axis=1) * (HEAD_DIM ** -0.5)

    if prompt:
        a_ref, q_ref, k_ref, v_ref, kaug_ref, vaug_ref, kmean_ref = outs
        a_ref[...] = a_out.astype(BF16)
        q_ref[...] = (q * LOG2E).astype(BF16)
        k_ref[...] = k
        v_ref[...] = v
        row = lax.broadcasted_iota(I32, (tm, LANES), 0)
        blk = i * (tm // MOBA_BLOCK) + lax.shift_right_logical(row, int(math.log2(MOBA_BLOCK)))
        lo = lane_t < HEAD_DIM
        hot_hi = jnp.where(blk == lane_t - HEAD_DIM, 1.0, 0.0)
        hot_lo = jnp.where(blk == lane_t, 1.0, 0.0)
        for p in range(bw // LANES):
            kp = k_parts[p]
            vp = v[:, p * LANES:(p + 1) * LANES]
            kaug_ref[:, (2 * p) * LANES:(2 * p + 1) * LANES] = jnp.where(lo, kp, hot_hi).astype(BF16)
            kaug_ref[:, (2 * p + 1) * LANES:(2 * p + 2) * LANES] = jnp.where(lo, hot_lo, kp).astype(BF16)
            vaug_ref[:, (2 * p) * LANES:(2 * p + 1) * LANES] = jnp.where(lo, vp, 1.0).astype(BF16)
            vaug_ref[:, (2 * p + 1) * LANES:(2 * p + 2) * LANES] = jnp.where(lo, 1.0, vp).astype(BF16)
        for b in range(tm // MOBA_BLOCK):
            kmean_ref[b] = jnp.mean(k[b * MOBA_BLOCK:(b + 1) * MOBA_BLOCK], axis=0, keepdims=True)
    else:
        a_ref, va_ref, q_ref, k_ref, v_ref = outs
        a_ref[...] = a_out.astype(BF16)
        va_ref[...] = va
        q_ref[...] = q
        k_ref[...] = k
        v_ref[...] = v


def _even_in(h, norm_g, w_in, ln_g, ln_b, ws, bias, cos_t, sin_t, *, tm, seq_l, prompt, a_width, b_width):
    t, d = h.shape
    nw = w_in.shape[1]
    row = lambda i: (i, 0)
    in_specs = [pl.BlockSpec((tm, d), row), _full((1, d)), _full((d, nw)), _full((1, a_width)), _full((1, a_width)),
                _full(ws.shape), _full(bias.shape), pl.BlockSpec((tm, LANES), row), pl.BlockSpec((tm, LANES), row)]
    if prompt:
        nb = t // MOBA_BLOCK
        out_shape = [jax.ShapeDtypeStruct((t, a_width), BF16), jax.ShapeDtypeStruct((t, b_width), BF16),
                     jax.ShapeDtypeStruct((t, b_width), F32), jax.ShapeDtypeStruct((t, b_width), F32),
                     jax.ShapeDtypeStruct((t, 2 * b_width), BF16), jax.ShapeDtypeStruct((t, 2 * b_width), BF16),
                     jax.ShapeDtypeStruct((nb, 1, b_width), F32)]
        out_specs = [pl.BlockSpec((tm, a_width), row), pl.BlockSpec((tm, b_width), row),
                     pl.BlockSpec((tm, b_width), row), pl.BlockSpec((tm, b_width), row),
                     pl.BlockSpec((tm, 2 * b_width), row), pl.BlockSpec((tm, 2 * b_width), row),
                     pl.BlockSpec((tm // MOBA_BLOCK, 1, b_width), lambda i: (i, 0, 0))]
    else:
        out_shape = [jax.ShapeDtypeStruct((t, a_width), BF16), jax.ShapeDtypeStruct((t, a_width), F32),
                     jax.ShapeDtypeStruct((t, b_width), F32), jax.ShapeDtypeStruct((t, b_width), F32),
                     jax.ShapeDtypeStruct((t, b_width), F32)]
        out_specs = [pl.BlockSpec((tm, a_width), row), pl.BlockSpec((tm, a_width), row),
                     pl.BlockSpec((tm, b_width), row), pl.BlockSpec((tm, b_width), row),
                     pl.BlockSpec((tm, b_width), row)]
    return pl.pallas_call(
        functools.partial(_even_in_kernel, seq_l=seq_l, prompt=prompt, a_width=a_width, b_width=b_width),
        grid=(t // tm,), in_specs=in_specs, out_specs=out_specs, out_shape=out_shape,
        compiler_params=_cparams(("arbitrary",)), name="even_in_prompt" if prompt else "even_in_sample",
    )(h, norm_g, w_in, ln_g, ln_b, ws, bias, cos_t, sin_t)


def _moba_prompt_kernel(q_ref, k0_ref, k1_ref, v0_ref, v1_ref, kmh_ref, kml_ref, o_ref, p_scr, m_scr):
    tq = q_ref.shape[0]
    i = pl.program_id(1)
    q = q_ref[...]
    qf = q.astype(F32)
    lane = lax.broadcasted_iota(I32, (tq, LANES), 1)
    k_refs = (k0_ref, k1_ref)
    v_refs = (v0_ref, v1_ref)
    neg_inf = jnp.float32(-jnp.inf)

    qa, q_own = [], []
    for e in range(2):
        in_blk = (lane >= HEAD_DIM) if e == 0 else (lane < HEAD_DIM)
        blk = lane - HEAD_DIM if e == 0 else lane
        sc = _dot(q, kmh_ref[0, e]) + _dot(q, kml_ref[0, e])
        cand = jnp.where(in_blk & (blk < i), sc, neg_inf)
        sel = jnp.zeros((tq, LANES), jnp.bool_)
        for _ in range(MOBA_TOPK):
            mx = jnp.max(cand, axis=-1, keepdims=True)
            first = jnp.min(jnp.where(cand == mx, blk, jnp.int32(2 ** 30)), axis=-1, keepdims=True)
            pick = (blk == first) & in_blk & (mx > neg_inf)
            sel = sel | pick
            cand = jnp.where(pick, neg_inf, cand)
        qa.append(jnp.where(in_blk, jnp.where(sel, 0.0, NEG), qf).astype(BF16))
        q_own.append(jnp.where(in_blk, 0.0, qf).astype(BF16))

    n_blk = k0_ref.shape[0] // MOBA_BLOCK

    def offset(blk_idx):
        return pl.multiple_of(jnp.minimum(blk_idx, n_blk - 1) * MOBA_BLOCK, MOBA_BLOCK)

    def probs(e, qe, blk_idx, mask=None):
        s = _dot_nt(qe, k_refs[e][pl.ds(offset(blk_idx), MOBA_BLOCK), :])
        if mask is not None:
            s = jnp.where(mask, s, NEG)
        m = jnp.max(s, axis=-1, keepdims=True)
        return m, jnp.exp2(s - m).astype(BF16)

    def values(e, p, blk_idx):
        return _dot(p, v_refs[e][pl.ds(offset(blk_idx), MOBA_BLOCK), :])

    def stash(g):
        slot = g & 1
        for e in range(2):
            for u in range(MOBA_UNROLL):
                m_u, p_u = probs(e, qa[e], g * MOBA_UNROLL + u)
                p_scr[slot, e * MOBA_UNROLL + u] = p_u
                m_scr[slot, e * MOBA_UNROLL + u] = m_u

    def merge(g, ms, accs):
        slot = g & 1
        new_m, new_acc = [], []
        for e in range(2):
            m_us = [m_scr[slot, e * MOBA_UNROLL + u] for u in range(MOBA_UNROLL)]
            m_new = ms[e]
            for m_u in m_us:
                m_new = jnp.maximum(m_new, m_u)
            acc = jnp.exp2(ms[e] - m_new) * accs[e]
            for u, m_u in enumerate(m_us):
                o_u = values(e, p_scr[slot, e * MOBA_UNROLL + u], g * MOBA_UNROLL + u)
                acc = acc + jnp.exp2(m_u - m_new) * o_u
            new_m.append(m_new)
            new_acc.append(acc)
        return tuple(new_m), tuple(new_acc)

    r = lax.broadcasted_iota(I32, (tq, MOBA_BLOCK), 0)
    c = lax.broadcasted_iota(I32, (tq, MOBA_BLOCK), 1)
    own = [probs(e, q_own[e], i, c <= r) for e in range(2)]
    ms = tuple(m for m, _ in own)
    accs = tuple(values(e, own[e][1], i) for e in range(2))

    def body(g, carry):
        ms, accs = merge(g - 1, *carry)
        stash(g)
        return ms, accs

    n_groups = lax.shift_right_logical(i + (MOBA_UNROLL - 1), int(math.log2(MOBA_UNROLL)))
    stash(0)
    ms, accs = lax.fori_loop(1, n_groups, body, (ms, accs))
    _, accs = merge(jnp.maximum(n_groups - 1, 0), ms, accs)
    outs = [a / pltpu.roll(a, HEAD_DIM, 1) for a in accs]
    o_ref[...] = jnp.where(lane < HEAD_DIM, outs[0], outs[1]).astype(o_ref.dtype)


def _moba_prompt(q, kaug, vaug, km_hi, km_lo):
    t, bw = q.shape
    n_pairs = bw // LANES
    tq = MOBA_BLOCK
    col = lambda e: (lambda p, i, _e=e: (0, 2 * p + _e))
    in_specs = [pl.BlockSpec((tq, LANES), lambda p, i: (i, p)),
                pl.BlockSpec((t, LANES), col(0)), pl.BlockSpec((t, LANES), col(1)),
                pl.BlockSpec((t, LANES), col(0)), pl.BlockSpec((t, LANES), col(1)),
                pl.BlockSpec((1, 2, LANES, LANES), lambda p, i: (p, 0, 0, 0)),
                pl.BlockSpec((1, 2, LANES, LANES), lambda p, i: (p, 0, 0, 0))]
    return pl.pallas_call(
        _moba_prompt_kernel, grid=(n_pairs, t // tq), in_specs=in_specs,
        out_specs=pl.BlockSpec((tq, LANES), lambda p, i: (i, p)),
        out_shape=jax.ShapeDtypeStruct((t, bw), BF16),
        scratch_shapes=[pltpu.VMEM((2, 2 * MOBA_UNROLL, tq, MOBA_BLOCK), BF16),
                        pltpu.VMEM((2, 2 * MOBA_UNROLL, tq, 1), F32)],
        compiler_params=_cparams(("arbitrary", "arbitrary")), name="moba_prompt",
    )(q, kaug, kaug, vaug, vaug, km_hi, km_lo)


def _moba_sample_keys_kernel(pt_ref, *refs, n_steps):
    npg = KV_PAGES_PER_STEP
    pages = refs[:npg]
    qbd_ref, knew_ref, p_ref, pown_ref, linv_ref, s_scr, kmean_scr, bmax_scr = refs[npg:]
    c = pl.program_id(1)
    hw, page = pages[0].shape[1], pages[0].shape[2]
    ppb = MOBA_BLOCK // page
    bps = npg // ppb
    qbd = qbd_ref[0]
    n_rows = qbd.shape[0]
    n_blocks = n_steps * bps
    lane = lax.broadcasted_iota(I32, (n_rows, LANES), 1)
    neg_inf = jnp.float32(-jnp.inf)

    @pl.when(c == 0)
    def _():
        bmax_scr[...] = jnp.full((n_rows, LANES), neg_inf, F32)
        kmean_scr[...] = jnp.zeros_like(kmean_scr)

    lane_k = lax.broadcasted_iota(I32, (hw, LANES), 1)
    for j in range(bps):
        kb = jnp.concatenate([pages[j * ppb + x][0] for x in range(ppb)], axis=1)
        blk = c * bps + j
        kmean_scr[...] = jnp.where(lane_k == blk, jnp.mean(kb, axis=1, keepdims=True), kmean_scr[...])
        s = _dot(qbd, kb.astype(BF16))
        s_scr[:, pl.ds(pl.multiple_of(blk * MOBA_BLOCK, MOBA_BLOCK), MOBA_BLOCK)] = s
        bmax_scr[...] = jnp.where(lane == blk, jnp.max(s, axis=-1, keepdims=True), bmax_scr[...])

    @pl.when(c == n_steps - 1)
    def _():
        kmh, kml = _split_bf16(kmean_scr[...])
        sc = _dot(qbd, kmh) + _dot(qbd, kml)
        cand = jnp.where(lane < n_blocks, sc, neg_inf)
        sel = jnp.zeros((n_rows, LANES), jnp.bool_)
        for _ in range(min(MOBA_TOPK, n_blocks)):
            mx = jnp.max(cand, axis=-1, keepdims=True)
            first = jnp.min(jnp.where(cand == mx, lane, jnp.int32(2 ** 30)), axis=-1, keepdims=True)
            pick = lane == first
            sel = sel | pick
            cand = jnp.where(pick, neg_inf, cand)
        so = _dot_nt(qbd, knew_ref[0].astype(BF16))
        n_new = so.shape[1]
        rr = lax.broadcasted_iota(I32, (n_rows, n_new), 0)
        ss = lax.broadcasted_iota(I32, (n_rows, n_new), 1)
        so = jnp.where(ss <= (rr & (n_new - 1)), so, NEG)
        m = jnp.maximum(jnp.max(jnp.where(sel, bmax_scr[...], neg_inf), axis=-1, keepdims=True),
                        jnp.max(so, axis=-1, keepdims=True))
        po = jnp.exp(so - m)
        l = jnp.sum(po, axis=-1, keepdims=True)
        self32 = sel.astype(F32)
        for b in range(n_blocks):
            on = jnp.sum(jnp.where(lane == b, self32, 0.0), axis=-1, keepdims=True)
            sb = s_scr[:, b * MOBA_BLOCK:(b + 1) * MOBA_BLOCK]
            pb = (jnp.exp(sb - m) * on).astype(BF16)
            l = l + jnp.sum(pb.astype(F32), axis=-1, keepdims=True)
            p_ref[0, :, b * MOBA_BLOCK:(b + 1) * MOBA_BLOCK] = pb
        pown_ref[0] = jnp.concatenate([po, jnp.zeros((n_rows, LANES - n_new), F32)], axis=1)
        linv_ref[0] = jnp.broadcast_to(1.0 / l, (n_rows, LANES))


def _moba_sample_values_kernel(pt_ref, *refs, n_steps, n_heads):
    npg = KV_PAGES_PER_STEP
    pages = refs[:npg]
    p_ref, pown_ref, linv_ref, vnew_ref, o_ref, acc_scr = refs[npg:]
    c = pl.program_id(1)
    vb = jnp.concatenate([pg[0] for pg in pages], axis=1).astype(BF16)
    part = _dot_nt(p_ref[0], vb)

    @pl.when(c == 0)
    def _():
        acc_scr[...] = part

    @pl.when(c > 0)
    def _():
        acc_scr[...] += part

    @pl.when(c == n_steps - 1)
    def _():
        vnew = vnew_ref[0]
        n_new = vnew.shape[0]
        o = (acc_scr[...] + _dot(pown_ref[0][:, :n_new].astype(BF16), vnew.astype(BF16))) * linv_ref[0][:, :1]
        n_rows, hw = o.shape
        rr = lax.broadcasted_iota(I32, (n_rows, hw), 0)
        ll = lax.broadcasted_iota(I32, (n_rows, hw), 1)
        keep = (lax.shift_right_logical(rr, int(math.log2(n_new)))
                == lax.shift_right_logical(ll, int(math.log2(HEAD_DIM))))
        o = jnp.where(keep, o, 0.0).reshape(n_heads, n_new, hw)
        o_ref[0] = jnp.sum(o, axis=0).astype(o_ref.dtype)


def _moba_sample(q, k, v, cache_k, cache_v, page_table, n_batch):
    hw = q.shape[1]
    n_heads = hw // HEAD_DIM
    n_new = q.shape[0] // n_batch
    n_phys, page = cache_k.shape[0], cache_k.shape[1]
    n_pages = page_table.shape[1]
    npg = KV_PAGES_PER_STEP
    assert n_pages % npg == 0 and (n_pages * page) % MOBA_BLOCK == 0 and MOBA_BLOCK % page == 0
    n_steps = n_pages // npg
    past = n_pages * page
    n_blocks = past // MOBA_BLOCK
    assert n_blocks <= LANES and n_new & (n_new - 1) == 0
    n_rows = n_heads * n_new
    ck = cache_k.transpose(0, 2, 3, 1).reshape(n_phys, hw, page)
    cv = cache_v.transpose(0, 2, 3, 1).reshape(n_phys, hw, page)
    pt = page_table.reshape(-1).astype(I32)
    q3 = q.reshape(n_batch, n_new, n_heads, HEAD_DIM)
    eye = jnp.eye(n_heads, dtype=F32)
    qbd = (q3.transpose(0, 2, 1, 3)[:, :, :, None, :] * eye[None, :, None, :, None]).reshape(n_batch, n_rows, hw)
    qbd = qbd.astype(BF16)
    k3 = k.reshape(n_batch, n_new, hw)
    v3 = v.reshape(n_batch, n_new, hw)

    def page_spec(s):
        return pl.BlockSpec((1, hw, page), lambda b, c, pt_ref, _s=s: (pt_ref[b * n_pages + c * npg + _s], 0, 0))

    per_b = lambda shape: pl.BlockSpec(shape, lambda b, c, pt_ref: (b,) + (0,) * (len(shape) - 1))
    p, pown, linv = pl.pallas_call(
        functools.partial(_moba_sample_keys_kernel, n_steps=n_steps),
        grid_spec=pltpu.PrefetchScalarGridSpec(
            num_scalar_prefetch=1, grid=(n_batch, n_steps),
            in_specs=[page_spec(s) for s in range(npg)] + [per_b((1, n_rows, hw)), per_b((1, n_new, hw))],
            out_specs=[per_b((1, n_rows, past)), per_b((1, n_rows, LANES)), per_b((1, n_rows, LANES))],
            scratch_shapes=[pltpu.VMEM((n_rows, past), F32), pltpu.VMEM((hw, LANES), F32),
                            pltpu.VMEM((n_rows, LANES), F32)]),
        out_shape=[jax.ShapeDtypeStruct((n_batch, n_rows, past), BF16),
                   jax.ShapeDtypeStruct((n_batch, n_rows, LANES), F32),
                   jax.ShapeDtypeStruct((n_batch, n_rows, LANES), F32)],
        compiler_params=_cparams(("arbitrary", "arbitrary")), name="moba_sample_keys",
    )(pt, *([ck] * npg), qbd, k3)

    out = pl.pallas_call(
        functools.partial(_moba_sample_values_kernel, n_steps=n_steps, n_heads=n_heads),
        grid_spec=pltpu.PrefetchScalarGridSpec(
            num_scalar_prefetch=1, grid=(n_batch, n_steps),
            in_specs=[page_spec(s) for s in range(npg)]
            + [pl.BlockSpec((1, n_rows, npg * page), lambda b, c, pt_ref: (b, 0, c)),
               per_b((1, n_rows, LANES)), per_b((1, n_rows, LANES)), per_b((1, n_new, hw))],
            out_specs=per_b((1, n_new, hw)),
            scratch_shapes=[pltpu.VMEM((n_rows, hw), F32)]),
        out_shape=jax.ShapeDtypeStruct((n_batch, n_new, hw), BF16),
        compiler_params=_cparams(("arbitrary", "arbitrary")), name="moba_sample_values",
    )(pt, *([cv] * npg), p, pown, linv, v3)
    return out.reshape(n_batch * n_new, hw)


def _ple(h, p_ref, wp_ref, wg_ref, pn_ref):
    gate = jax.nn.sigmoid(_dot(_rms(h, pn_ref[...]).astype(BF16), wg_ref[...]))
    return h + _dot(p_ref[...].astype(BF16), wp_ref[...]) * gate


def _even_tail_kernel(h_ref, a_ref, b_ref, wo_ref, nf_ref, w1_ref, w3_ref, w2_ref, p_ref, wp_ref, wg_ref, pn_ref,
                      nn_ref, h_out_ref, xn_out_ref, *, ff_chunks):
    aw = a_ref.shape[1]
    h1 = h_ref[...] + _dot(a_ref[...], wo_ref[0:aw, :]) + _dot(b_ref[...], wo_ref[aw:, :])
    xn = _rms(h1, nf_ref[...]).astype(BF16)
    ff = w1_ref.shape[1]
    cw = ff // ff_chunks
    hid = []
    for ci in range(ff_chunks):
        sl = slice(ci * cw, (ci + 1) * cw)
        hid.append((jax.nn.silu(_dot(xn, w1_ref[:, sl])) * _dot(xn, w3_ref[:, sl])).astype(BF16))
    acc = h1 + _dot(jnp.concatenate(hid, axis=1), w2_ref[...])
    h3 = _ple(acc, p_ref, wp_ref, wg_ref, pn_ref)
    h_out_ref[...] = h3
    xn_out_ref[...] = _rms(h3, nn_ref[...]).astype(BF16)


def _even_tail(h, a, b, w_out, norm_ffn, w1, w3, w2, p, wp, wg, pn, next_norm, *, tm):
    t, d = h.shape
    ff = w1.shape[1]
    ff_chunks = next(n for n in (4, 2, 1, 11, 22) if ff % (n * LANES) == 0) if ff % LANES == 0 else 1
    row = lambda i: (i, 0)
    in_specs = [pl.BlockSpec((tm, d), row), pl.BlockSpec((tm, a.shape[1]), row), pl.BlockSpec((tm, b.shape[1]), row),
                _full(w_out.shape), _full((1, d)), _full(w1.shape), _full(w3.shape), _full(w2.shape),
                pl.BlockSpec((tm, p.shape[1]), row), _full(wp.shape), _full(wg.shape), _full((1, d)), _full((1, d))]
    return pl.pallas_call(
        functools.partial(_even_tail_kernel, ff_chunks=ff_chunks),
        grid=(t // tm,), in_specs=in_specs,
        out_specs=[pl.BlockSpec((tm, d), row), pl.BlockSpec((tm, d), row)],
        out_shape=[jax.ShapeDtypeStruct((t, d), F32), jax.ShapeDtypeStruct((t, d), BF16)],
        compiler_params=_cparams(("arbitrary",)), name="even_tail",
    )(h, a, b, w_out, norm_ffn, w1, w3, w2, p, wp, wg, pn, next_norm)


def _odd_tail(conv, d_in, h3, refs, count_scr, n_experts):
    (cb_ref, clg_ref, clb_ref, pw_ref, ps_ref, wo_ref, nf_ref, rwh_ref, rwl_ref, rb_ref) = refs
    rows = conv.shape[0]
    cw = conv.shape[1]
    cact = jax.nn.silu(_layernorm(conv + cb_ref[...], clg_ref[...], clb_ref[...]))
    db = d_in.astype(BF16)
    gdim = d_in.shape[1] // len(POOL_WINDOWS)
    dparts = [_dot(db[:, g * gdim:(g + 1) * gdim], pw_ref[g]) for g in range(len(POOL_WINDOWS))]
    dmix = jnp.concatenate(dparts, axis=1) * ps_ref[...]
    h4 = h3 + _dot(cact.astype(BF16), wo_ref[0:cw, :]) + _dot(dmix.astype(BF16), wo_ref[cw:, :])
    xn = _rms(h4, nf_ref[...])
    xh, xl = _split_bf16(xn)
    logits = _dot(xh, rwh_ref[...]) + _dot(xl, rwh_ref[...]) + _dot(xh, rwl_ref[...]) + rb_ref[...]
    lane = lax.broadcasted_iota(I32, (rows, LANES), 1)
    neg_inf = jnp.float32(-jnp.inf)
    cand = jnp.where(lane < n_experts, logits, neg_inf)
    v1 = jnp.max(cand, axis=-1, keepdims=True)
    e1 = jnp.min(jnp.where(cand == v1, lane, jnp.int32(2 ** 30)), axis=-1, keepdims=True)
    cand2 = jnp.where(lane == e1, neg_inf, cand)
    v2 = jnp.max(cand2, axis=-1, keepdims=True)
    e2 = jnp.min(jnp.where(cand2 == v2, lane, jnp.int32(2 ** 30)), axis=-1, keepdims=True)
    ex = jnp.exp(v2 - v1)
    g1 = 1.0 / (1.0 + ex)
    g2 = ex / (1.0 + ex)
    chosen = jnp.where((lane == e1) | (lane == e2), 1.0, 0.0).astype(BF16)
    rr = lax.broadcasted_iota(I32, (rows, rows), 0)
    cc = lax.broadcasted_iota(I32, (rows, rows), 1)
    before = jnp.where(cc < rr, 1.0, 0.0).astype(BF16)
    prefix = _dot(before, chosen) + count_scr[...]
    r1 = _lane_pick(prefix, lane, e1)
    r2 = _lane_pick(prefix, lane, e2)
    count_scr[...] = count_scr[...] + jnp.sum(chosen.astype(F32), axis=0, keepdims=True)
    meta = jnp.zeros((rows, LANES), F32)
    for idx, val in enumerate((e1.astype(F32), e2.astype(F32), g1, g2, r1, r2)):
        meta = jnp.where(lane == idx, val, meta)
    return h4, xn, meta


def _odd_prompt_kernel(xm_ref, xh_ref, h3_ref, wi_ref, cw_ref, *refs, n_experts, conv_hist, pool_hist):
    tail_refs = refs[:10]
    cnt_in_ref, h4_ref, xn_ref, meta_ref, cnt_ref, cstate_ref, pstate_ref, count_scr = refs[10:]
    i = pl.program_id(0)
    tm = xm_ref.shape[0]
    cwid = cw_ref.shape[1]

    @pl.when(i == 0)
    def _():
        count_scr[...] = cnt_in_ref[...]

    x = jnp.concatenate([xh_ref[...], xm_ref[...]], axis=0)
    z = _dot(x, wi_ref[...])
    row = lax.broadcasted_iota(I32, (HALO + tm, 1), 0)
    z = jnp.where((row >= HALO) | (i > 0), z, 0.0)
    c_in = z[:, :cwid] * jax.nn.sigmoid(z[:, cwid:2 * cwid])
    dx = z[:, 2 * cwid:]

    cwv = cw_ref[...]
    width = cwv.shape[0]
    lead = HALO - (width - 1)

    def tap(kp):
        k = kp - lead
        return cwv[k:k + 1, :] if 0 <= k < width else None

    ext = c_in
    conv = None
    for b in range(8):
        n_a = (HALO // 8 + 1) if b == 0 else HALO // 8
        rows_b = tm if b == 0 else tm + 8
        ub = None
        for a in range(n_a):
            w = tap(8 * a + b)
            if w is None:
                continue
            term = ext[8 * a:8 * a + rows_b] * w
            ub = term if ub is None else ub + term
        if ub is None:
            continue
        piece = ub if b == 0 else pltpu.roll(ub, rows_b - b, 0)[:tm]
        conv = piece if conv is None else conv + piece

    s = dx
    sums = {}
    span = 1
    while span < max(POOL_WINDOWS):
        s = s + pltpu.roll(s, span, 0)
        span *= 2
        sums[span] = s
    gdim = dx.shape[1] // len(POOL_WINDOWS)
    pos = i * tm + lax.broadcasted_iota(I32, (tm, 1), 0)
    pooled = []
    for g, w in enumerate(POOL_WINDOWS):
        win = sums[w][HALO:, g * gdim:(g + 1) * gdim]
        cnt = jnp.minimum(pos + 1, w).astype(F32)
        pooled.append(win / cnt)
    dmain = dx[HALO:]
    d_in = jnp.concatenate(pooled, axis=1) - dmain

    h4, xn, meta = _odd_tail(conv, d_in, h3_ref[...], tail_refs, count_scr, n_experts)
    h4_ref[...] = h4
    _to_row_tiles(xn_ref, xn)
    meta_ref[...] = meta
    cnt_ref[...] = count_scr[...]
    ctail = c_in[tm:]
    cstate_ref[...] = pltpu.roll(ctail, conv_hist, 0)[:conv_hist] if conv_hist < HALO else ctail
    ptail_rows = 8 * (-(-pool_hist // 8))
    ptail = dmain[tm - ptail_rows:]
    pstate_ref[...] = pltpu.roll(ptail, pool_hist, 0)[:pool_hist] if pool_hist < ptail_rows else ptail


def _odd_sample_kernel(x_ref, h3_ref, wi_ref, cw_ref, *refs, n_experts, n_batch):
    tail_refs = refs[:10]
    cnt_in_ref, chist_ref, phist_ref, h4_ref, xn_ref, meta_ref, cnt_ref, cstate_ref, pstate_ref, count_scr = refs[10:]
    rows = x_ref.shape[0]
    n_new = rows // n_batch
    cwid = cw_ref.shape[1]
    count_scr[...] = cnt_in_ref[...]
    z = _dot(x_ref[...], wi_ref[...])
    c_in = z[:, :cwid] * jax.nn.sigmoid(z[:, cwid:2 * cwid])
    dx = z[:, 2 * cwid:]
    cwv = cw_ref[...]
    width = cwv.shape[0]
    hist_c = chist_ref[...]
    xp = [hist_c[j] for j in range(width - 1)] + [c_in[t * n_batch:(t + 1) * n_batch] for t in range(n_new)]
    conv = []
    for t in range(n_new):
        acc = xp[t] * cwv[0:1, :]
        for k in range(1, width):
            acc = acc + xp[t + k] * cwv[k:k + 1, :]
        conv.append(acc)
    conv = jnp.concatenate(conv, axis=0)
    cstate_ref[...] = jnp.stack(xp[n_new:], axis=0)

    hist_p = phist_ref[...]
    n_ph = hist_p.shape[0]
    xq = [hist_p[j] for j in range(n_ph)] + [dx[t * n_batch:(t + 1) * n_batch] for t in range(n_new)]
    gdim = dx.shape[1] // len(POOL_WINDOWS)
    lane = lax.broadcasted_iota(I32, (n_batch, dx.shape[1]), 1)
    d_in = []
    for t in range(n_new):
        run = xq[n_ph + t]
        tot = jnp.zeros_like(run)
        done = 1
        for g, w in enumerate(POOL_WINDOWS):
            for j in range(done, w):
                run = run + xq[n_ph + t - j]
            done = w
            in_g = (lane >= g * gdim) & (lane < (g + 1) * gdim)
            tot = jnp.where(in_g, run / float(w), tot)
        d_in.append(tot - xq[n_ph + t])
    d_in = jnp.concatenate(d_in, axis=0)
    pstate_ref[...] = jnp.stack(xq[n_new:], axis=0)

    h4, xn, meta = _odd_tail(conv, d_in, h3_ref[...], tail_refs, count_scr, n_experts)
    h4_ref[...] = h4
    _to_row_tiles(xn_ref, xn)
    meta_ref[...] = meta
    cnt_ref[...] = count_scr[...]


def _odd_common_args(conv_b, c_ln_g, c_ln_b, pool_w, pool_scale, w_out, norm_ffn, rw_hi, rw_lo, rb):
    args = (conv_b, c_ln_g, c_ln_b, pool_w, pool_scale, w_out, norm_ffn, rw_hi, rw_lo, rb)
    return args, [_full(a.shape) for a in args]


def _odd_prompt(xn3, h3, w_in, conv_w, common, counts_in, *, tm, n_experts):
    t, d = h3.shape
    cwid = conv_w.shape[1]
    conv_hist = conv_w.shape[0] - 1
    pool_hist = max(POOL_WINDOWS) - 1
    args, specs = common
    row = lambda i: (i, 0)
    hpb = tm // HALO
    in_specs = [pl.BlockSpec((tm, d), row), pl.BlockSpec((HALO, d), lambda i: (jnp.maximum(i * hpb - 1, 0), 0)),
                pl.BlockSpec((tm, d), row), _full(w_in.shape), _full(conv_w.shape)] + specs + [_full((1, LANES))]
    dwid = w_in.shape[1] - 2 * cwid
    return pl.pallas_call(
        functools.partial(_odd_prompt_kernel, n_experts=n_experts, conv_hist=conv_hist, pool_hist=pool_hist),
        grid=(t // tm,), in_specs=in_specs,
        out_specs=[pl.BlockSpec((tm, d), row), pl.BlockSpec((tm, d // LANES, LANES), lambda i: (i, 0, 0)),
                   pl.BlockSpec((tm, LANES), row),
                   _whole((1, LANES)), _whole((conv_hist, cwid)), _whole((pool_hist, dwid))],
        out_shape=[jax.ShapeDtypeStruct((t, d), F32), jax.ShapeDtypeStruct((t, d // LANES, LANES), F32),
                   jax.ShapeDtypeStruct((t, LANES), F32), jax.ShapeDtypeStruct((1, LANES), F32),
                   jax.ShapeDtypeStruct((conv_hist, cwid), F32), jax.ShapeDtypeStruct((pool_hist, dwid), F32)],
        scratch_shapes=[pltpu.VMEM((1, LANES), F32)],
        compiler_params=_cparams(("arbitrary",)), name="odd_prompt",
    )(xn3, xn3, h3, w_in, conv_w, *args, counts_in)


def _odd_sample(xn3, h3, w_in, conv_w, common, counts_in, conv_hist_t, pool_hist_t, *, n_batch, n_experts):
    t, d = h3.shape
    cwid = conv_w.shape[1]
    dwid = w_in.shape[1] - 2 * cwid
    args, specs = common
    in_specs = [_full((t, d)), _full((t, d)), _full(w_in.shape), _full(conv_w.shape)] + specs + \
               [_full((1, LANES)), _full(conv_hist_t.shape), _full(pool_hist_t.shape)]
    return pl.pallas_call(
        functools.partial(_odd_sample_kernel, n_experts=n_experts, n_batch=n_batch),
        grid=(1,), in_specs=in_specs,
        out_specs=[_whole((t, d)), _whole((t, d // LANES, LANES)), _whole((t, LANES)), _whole((1, LANES)),
                   _whole(conv_hist_t.shape), _whole(pool_hist_t.shape)],
        out_shape=[jax.ShapeDtypeStruct((t, d), F32), jax.ShapeDtypeStruct((t, d // LANES, LANES), F32),
                   jax.ShapeDtypeStruct((t, LANES), F32), jax.ShapeDtypeStruct((1, LANES), F32),
                   jax.ShapeDtypeStruct(conv_hist_t.shape, F32), jax.ShapeDtypeStruct(pool_hist_t.shape, F32)],
        scratch_shapes=[pltpu.VMEM((1, LANES), F32)],
        compiler_params=_cparams(("arbitrary",)), name="odd_sample",
    )(xn3, h3, w_in, conv_w, *args, counts_in, conv_hist_t, pool_hist_t)


def _row_copy(src, s, dst, d, sem):
    return pltpu.make_async_copy(src.at[pl.ds(s, 1)], dst.at[pl.ds(d, 1)], sem)


def _rows_wait(ref, n, sem):
    pltpu.make_async_copy(ref.at[pl.ds(0, n)], ref.at[pl.ds(0, n)], sem).wait()


def _to_row_tiles(ref, x):
    for s in range(ref.shape[1]):
        ref[:, s, :] = x[:, s * LANES:(s + 1) * LANES]


def _from_row_tiles(ref):
    return jnp.concatenate([ref[:, s, :] for s in range(ref.shape[1])], axis=1)


def _moe_scatter_kernel(dest_ref, x_ref, buf_in_hbm, buf_hbm, sem, *, tm):
    del buf_in_hbm
    base = pl.program_id(0) * tm

    def start(r, carry):
        for k in range(TOP_K):
            _row_copy(x_ref, r, buf_hbm, dest_ref[(base + r) * TOP_K + k], sem).start()
        return carry

    lax.fori_loop(0, tm, start, 0)
    _rows_wait(buf_hbm, tm * TOP_K, sem)


def _moe_scatter(dest, x, buf, *, tm):
    t = x.shape[0]
    any_spec = pl.BlockSpec(memory_space=pl.ANY)
    return pl.pallas_call(
        functools.partial(_moe_scatter_kernel, tm=tm),
        grid_spec=pltpu.PrefetchScalarGridSpec(
            num_scalar_prefetch=1, grid=(t // tm,),
            in_specs=[pl.BlockSpec((tm,) + x.shape[1:], lambda i, dref: (i, 0, 0)), any_spec], out_specs=any_spec,
            scratch_shapes=[pltpu.SemaphoreType.DMA(())]),
        out_shape=jax.ShapeDtypeStruct(buf.shape, buf.dtype),
        input_output_aliases={2: 0},
        compiler_params=_cparams(("arbitrary",)), name="moe_scatter",
    )(dest, x, buf)


def _moe_experts_kernel(be_ref, nu_ref, x_ref, w1_ref, w3_ref, w2_ref, y_ref, xb_scr, acc_scr, *, nj):
    i = pl.program_id(0)
    j = pl.program_id(1)
    used = i < nu_ref[0]

    @pl.when(used)
    def _():
        @pl.when(j == 0)
        def _():
            xb_scr[...] = _from_row_tiles(x_ref).astype(BF16)

        xb = xb_scr[...]
        hid = jax.nn.silu(_dot(xb, w1_ref[0])) * _dot(xb, w3_ref[0])
        part = _dot(hid.astype(BF16), w2_ref[0])

        @pl.when(j == 0)
        def _():
            acc_scr[...] = part

        @pl.when(j > 0)
        def _():
            acc_scr[...] += part

        @pl.when(j == nj - 1)
        def _():
            _to_row_tiles(y_ref, acc_scr[...])

    @pl.when(jnp.logical_not(used) & (j == nj - 1))
    def _():
        y_ref[...] = jnp.zeros_like(y_ref)


def _moe_experts(blk_e, n_used, xbuf, w1, w3, w2, *, tm, tf):
    rows, ns, _ = xbuf.shape
    d = ns * LANES
    ff = w1.shape[2]
    nj = ff // tf
    nb = rows // tm
    live_j = lambda i, j, nu: jnp.where(i < nu[0], j, nj - 1)
    return pl.pallas_call(
        functools.partial(_moe_experts_kernel, nj=nj),
        grid_spec=pltpu.PrefetchScalarGridSpec(
            num_scalar_prefetch=2, grid=(nb, nj),
            in_specs=[pl.BlockSpec((tm, ns, LANES), lambda i, j, be, nu: (i, 0, 0)),
                      pl.BlockSpec((1, d, tf), lambda i, j, be, nu: (be[i], 0, live_j(i, j, nu))),
                      pl.BlockSpec((1, d, tf), lambda i, j, be, nu: (be[i], 0, live_j(i, j, nu))),
                      pl.BlockSpec((1, tf, d), lambda i, j, be, nu: (be[i], live_j(i, j, nu), 0))],
            out_specs=pl.BlockSpec((tm, ns, LANES), lambda i, j, be, nu: (i, 0, 0)),
            scratch_shapes=[pltpu.VMEM((tm, d), BF16), pltpu.VMEM((tm, d), F32)]),
        out_shape=jax.ShapeDtypeStruct((rows, ns, LANES), F32),
        compiler_params=_cparams(("arbitrary", "arbitrary")), name="moe_experts",
    )(blk_e, n_used, xbuf, w1, w3, w2)


def _moe_combine_kernel(dest_ref, y_hbm, h4_ref, meta_ref, p_ref, wp_ref, wg_ref, pn_ref, fn_ref, o_ref,
                        y0_scr, y1_scr, sem, *, tm):
    base = pl.program_id(0) * tm
    bufs = (y0_scr, y1_scr)

    def start(r, carry):
        for k in range(TOP_K):
            _row_copy(y_hbm, dest_ref[(base + r) * TOP_K + k], bufs[k], r, sem).start()
        return carry

    lax.fori_loop(0, tm, start, 0)
    for k in range(TOP_K):
        _rows_wait(bufs[k], tm, sem)
    meta = meta_ref[...]
    lane = lax.broadcasted_iota(I32, meta.shape, 1)
    g1 = jnp.sum(jnp.where(lane == 2, meta, 0.0), axis=-1, keepdims=True)
    g2 = jnp.sum(jnp.where(lane == 3, meta, 0.0), axis=-1, keepdims=True)
    h5 = h4_ref[...] + _from_row_tiles(y0_scr) * g1 + _from_row_tiles(y1_scr) * g2
    h6 = _ple(h5, p_ref, wp_ref, wg_ref, pn_ref)
    o_ref[...] = _rms(h6, fn_ref[...])


def _moe_combine(dest, ybuf, h4, meta, p, wp, wg, pn, fn, *, tm):
    t, d = h4.shape
    row = lambda i, dref: (i, 0)
    full = lambda shape: pl.BlockSpec(shape, lambda i, dref, _n=len(shape): (0,) * _n)
    return pl.pallas_call(
        functools.partial(_moe_combine_kernel, tm=tm),
        grid_spec=pltpu.PrefetchScalarGridSpec(
            num_scalar_prefetch=1, grid=(t // tm,),
            in_specs=[pl.BlockSpec(memory_space=pl.ANY), pl.BlockSpec((tm, d), row), pl.BlockSpec((tm, LANES), row),
                      pl.BlockSpec((tm, p.shape[1]), row), full(wp.shape), full(wg.shape), full((1, d)), full((1, d))],
            out_specs=pl.BlockSpec((tm, d), row),
            scratch_shapes=[pltpu.VMEM((tm, d // LANES, LANES), F32), pltpu.VMEM((tm, d // LANES, LANES), F32),
                            pltpu.SemaphoreType.DMA(())]),
        out_shape=jax.ShapeDtypeStruct((t, d), F32),
        compiler_params=_cparams(("arbitrary",)), name="moe_combine",
    )(dest, ybuf, h4, meta, p, wp, wg, pn, fn)


def _rope_tables(pos):
    half = HEAD_DIM // 2
    inv = ROPE_THETA ** (-jnp.arange(half, dtype=F32) * (2.0 / HEAD_DIM))
    ang = pos.astype(F32)[:, None] * inv[None, :]
    cos, sin = jnp.cos(ang), jnp.sin(ang)
    reps = LANES // HEAD_DIM
    cos_t = jnp.tile(jnp.concatenate([cos, cos], axis=1), (1, reps))
    sin_t = jnp.tile(jnp.concatenate([-sin, sin], axis=1), (1, reps))
    return cos_t, sin_t


def _row_tile(t, pref):
    for tm in (pref, 256, 128, 64, 32, 16, 8):
        if tm <= t and t % tm == 0:
            return tm
    return t


def kernel(x_prompt, x_sample, cache_k, cache_v, state_conv, state_pool, page_table, p_prompt, p_sample, norm_mix_e, w_in_e, a_ln_g, a_ln_b, a_ws, a_bs, w_out_e, norm_ffn_e, ffn_w1, ffn_w3, ffn_w2, norm_mix_o, w_in_o, conv_w, conv_b, c_ln_g, c_ln_b, pool_w, pool_scale, w_out_o, norm_ffn_o, router_w, router_b, exp_w1, exp_w3, exp_w2, ple_w, ple_gate_w, ple_norm, final_norm):
    assert x_prompt.shape[0] == 1 and norm_mix_e.shape[0] == 1 and norm_mix_o.shape[0] == 1 and ple_w.shape[0] == 2
    n_tok, d = x_prompt.shape[1], x_prompt.shape[2]
    n_batch, n_new = x_sample.shape[0], x_sample.shape[1]
    n_s = n_batch * n_new
    a_width = a_ln_g.shape[1]
    b_width = (w_in_e.shape[2] - 2 * a_width) // 3
    n_heads = b_width // HEAD_DIM
    gdim = a_width // A_GROUPS
    n_experts = router_w.shape[2]
    page = cache_k.shape[2]
    past_len = page_table.shape[1] * page
    assert n_tok % MOBA_BLOCK == 0 and n_tok // MOBA_BLOCK <= HEAD_DIM and past_len % MOBA_BLOCK == 0
    assert n_new <= CHUNK and CHUNK % n_new == 0 and n_s % CHUNK == 0

    bf = lambda w: w.astype(BF16)
    row2 = lambda v: v.reshape(1, -1)

    w_in = bf(w_in_e[0])
    ws_p = a_ws[0]
    bias_p = jnp.repeat(a_bs[0].T, gdim, axis=1)
    reps = CHUNK // n_new
    ws_s = jnp.tile(a_ws[0][:, :n_new, :n_new], (1, reps, reps))
    bias_s = jnp.tile(jnp.repeat(a_bs[0][:, :n_new].T, gdim, axis=1), (reps, 1))
    cos_p, sin_p = _rope_tables(jnp.arange(n_tok, dtype=I32))
    cos_s, sin_s = _rope_tables(past_len + (jnp.arange(n_s, dtype=I32) % n_new))
    hp0 = x_prompt[0]
    hs0 = x_sample.reshape(n_s, d)
    ev = dict(a_width=a_width, b_width=b_width)
    tm_p = _row_tile(n_tok, 512)
    a_p, q_p, k_p, v_p, kaug, vaug, kmean = _even_in(
        hp0, row2(norm_mix_e[0]), w_in, row2(a_ln_g[0]), row2(a_ln_b[0]), ws_p, bias_p, cos_p, sin_p,
        tm=tm_p, seq_l=CHUNK, prompt=True, **ev)
    a_s, va_s, q_s, k_s, v_s = _even_in(
        hs0, row2(norm_mix_e[0]), w_in, row2(a_ln_g[0]), row2(a_ln_b[0]), ws_s, bias_s, cos_s, sin_s,
        tm=_row_tile(n_s, 256), seq_l=n_new, prompt=False, **ev)

    n_blk = n_tok // MOBA_BLOCK
    km = kmean.reshape(n_blk, n_heads // 2, 2, HEAD_DIM)
    km_t = jnp.zeros((n_heads // 2, 2, LANES, LANES), F32)
    km_t = km_t.at[:, 0, :HEAD_DIM, HEAD_DIM:HEAD_DIM + n_blk].set(km[:, :, 0].transpose(1, 2, 0))
    km_t = km_t.at[:, 1, HEAD_DIM:, :n_blk].set(km[:, :, 1].transpose(1, 2, 0))
    km_hi = km_t.astype(BF16)
    km_lo = (km_t - km_hi.astype(F32)).astype(BF16)
    attn_p = _moba_prompt(q_p, kaug, vaug, km_hi, km_lo)
    attn_s = _moba_sample(q_s, k_s, v_s, cache_k[0], cache_v[0], page_table, n_batch)

    tail = (bf(w_out_e[0]), row2(norm_ffn_e[0]), bf(ffn_w1[0]), bf(ffn_w3[0]), bf(ffn_w2[0]))
    ple0 = (bf(ple_w[0]), bf(ple_gate_w[0]), row2(ple_norm[0]), row2(norm_mix_o[0]))
    h3_p, xn3_p = _even_tail(hp0, a_p, attn_p, *tail, p_prompt[0, 0], *ple0, tm=tm_p)
    h3_s, xn3_s = _even_tail(hs0, a_s, attn_s, *tail, p_sample[0].reshape(n_s, -1), *ple0, tm=_row_tile(n_s, 256))

    def time_major(x):
        return x.reshape(n_batch, n_new, -1).transpose(1, 0, 2).reshape(n_s, -1)

    rw = jnp.zeros((d, LANES), F32).at[:, :n_experts].set(router_w[0])
    rw_hi = rw.astype(BF16)
    rw_lo = (rw - rw_hi.astype(F32)).astype(BF16)
    rb = jnp.zeros((1, LANES), F32).at[0, :n_experts].set(router_b[0])
    common = _odd_common_args(row2(conv_b[0]), row2(c_ln_g[0]), row2(c_ln_b[0]), bf(pool_w[0]), row2(pool_scale[0]),
                              bf(w_out_o[0]), row2(norm_ffn_o[0]), rw_hi, rw_lo, rb)
    w_in1 = bf(w_in_o[0])
    zero_counts = jnp.zeros((1, LANES), F32)
    h4_p, xn5_p, meta_p, cnt_p, cstate_p, pstate_p = _odd_prompt(
        xn3_p, h3_p, w_in1, conv_w[0], common, zero_counts, tm=tm_p, n_experts=n_experts)
    h4_s, xn5_s, meta_s, cnt_all, cstate_s, pstate_s = _odd_sample(
        time_major(xn3_s), time_major(h3_s), w_in1, conv_w[0], common, cnt_p,
        state_conv[0].transpose(1, 0, 2), state_pool[0].transpose(1, 0, 2), n_batch=n_batch, n_experts=n_experts)

    counts = cnt_all[0, :n_experts].astype(I32)
    padded = (counts + MOE_TM - 1) // MOE_TM * MOE_TM
    pends = jnp.cumsum(padded)
    pstarts = (pends - padded).astype(F32)
    n_rows_all = (n_tok + n_s) * TOP_K
    n_blocks = -(-n_rows_all // MOE_TM) + n_experts
    blk_row0 = jnp.arange(n_blocks, dtype=I32) * MOE_TM
    blk_e = jnp.minimum(jnp.sum((pends[None, :] <= blk_row0[:, None]).astype(I32), axis=1), n_experts - 1)
    n_used = (pends[-1] // MOE_TM).astype(I32).reshape(1)
    last_e = blk_e[jnp.maximum(n_used[0] - 1, 0)]
    blk_e = jnp.where(jnp.arange(n_blocks) < n_used[0], blk_e, last_e)

    def dests(meta):
        e = meta[:, :TOP_K].astype(I32)
        return (pstarts[e] + meta[:, 4:4 + TOP_K]).astype(I32).reshape(-1)

    dest_p, dest_s = dests(meta_p), dests(meta_s)
    xbuf = jnp.zeros((n_blocks * MOE_TM, d // LANES, LANES), F32)
    xbuf = _moe_scatter(dest_p, xn5_p, xbuf, tm=_row_tile(n_tok, 512))
    xbuf = _moe_scatter(dest_s, xn5_s, xbuf, tm=_row_tile(n_s, 256))
    ff_e = exp_w1.shape[3]
    tf = next(c for c in (896, 512, 256, 128, ff_e) if ff_e % c == 0)
    ybuf = _moe_experts(blk_e, n_used, xbuf, bf(exp_w1[0]), bf(exp_w3[0]), bf(exp_w2[0]), tm=MOE_TM, tf=tf)
    ple1 = (bf(ple_w[1]), bf(ple_gate_w[1]), row2(ple_norm[1]), row2(final_norm))
    y_p = _moe_combine(dest_p, ybuf, h4_p, meta_p, p_prompt[1, 0], *ple1, tm=_row_tile(n_tok, 256))
    y_s = _moe_combine(dest_s, ybuf, h4_s, meta_s, time_major(p_sample[1].reshape(n_s, -1)), *ple1,
                       tm=_row_tile(n_s, 256))

    def batch_major(x):
        return x.reshape(n_new, n_batch, -1).transpose(1, 0, 2)

    kv = lambda x, b, l: x.reshape(1, b, l, n_heads, HEAD_DIM)
    return (y_p[None], batch_major(y_s),
            kv(k_p, 1, n_tok), kv(v_p, 1, n_tok), kv(k_s, n_batch, n_new), kv(v_s, n_batch, n_new),
            va_s.reshape(1, n_batch, n_new, a_width),
            cstate_p[None, None], cstate_s.transpose(1, 0, 2)[None],
            pstate_p[None, None], pstate_s.transpose(1, 0, 2)[None])
```

```python
import functools
import math

import jax
import jax.numpy as jnp
from jax import lax
from jax.experimental import pallas as pl
from jax.experimental.pallas import tpu as pltpu

F32 = jnp.float32
BF16 = jnp.bfloat16
I32 = jnp.int32

EPS = 1e-6
NEG = -1e30
LOG2E = 1.4426950408889634
CHUNK = 128
A_GROUPS = 8
HEAD_DIM = 64
MOBA_BLOCK = 256
MOBA_TOPK = 3
ROPE_THETA = 10000.0
POOL_WINDOWS = (2, 4, 8, 16)
TOP_K = 2

LANES = 128
MXU_WIDTH = 256
VMEM_LIMIT_BYTES = 56 * 1024 * 1024
HALO = 32
MOE_TM = 512
MOBA_UNROLL = 4
KV_PAGES_PER_STEP = 32


def _cparams(sem):
    return pltpu.CompilerParams(dimension_semantics=sem, vmem_limit_bytes=VMEM_LIMIT_BYTES)


def _full(shape):
    n = len(shape)
    return pl.BlockSpec(shape, lambda *a, _n=n: (0,) * _n, pipeline_mode=pl.Buffered(1))


def _whole(shape):
    n = len(shape)
    return pl.BlockSpec(shape, lambda *a, _n=n: (0,) * _n)


def _dot(a, b):
    return jnp.dot(a, b, preferred_element_type=F32)


def _dot_nt(a, b):
    return lax.dot_general(a, b, (((1,), (1,)), ((), ())), preferred_element_type=F32)


def _rms(x, g):
    return x * lax.rsqrt(jnp.mean(x * x, axis=-1, keepdims=True) + EPS) * g


def _layernorm(x, g, b):
    xc = x - jnp.mean(x, axis=-1, keepdims=True)
    var = jnp.mean(xc * xc, axis=-1, keepdims=True)
    return xc * lax.rsqrt(var + EPS) * g + b


def _split_bf16(x):
    hi = x.astype(BF16)
    lo = (x - hi.astype(F32)).astype(BF16)
    return hi, lo


def _lane_pick(x, lane, idx):
    return jnp.sum(jnp.where(lane == idx, x, 0.0), axis=-1, keepdims=True)


def _even_in_kernel(h_ref, g_ref, w_ref, lng_ref, lnb_ref, ws_ref, bias_ref, cos_ref, sin_ref, *outs,
                    seq_l, prompt, a_width, b_width):
    tm = h_ref.shape[0]
    i = pl.program_id(0)
    xn = _rms(h_ref[...], g_ref[...]).astype(BF16)
    aw, bw = a_width, b_width

    u = jax.nn.gelu(_dot(xn, w_ref[:, 0:aw]))
    va = _layernorm(jax.nn.gelu(_dot(xn, w_ref[:, aw:2 * aw])), lng_ref[...], lnb_ref[...])
    vab = va.astype(BF16)

    r = lax.broadcasted_iota(I32, (CHUNK, CHUNK), 0)
    c = lax.broadcasted_iota(I32, (CHUNK, CHUNK), 1)
    sh = int(math.log2(seq_l))
    ok = (lax.shift_right_logical(r, sh) == lax.shift_right_logical(c, sh)) & (c <= r)
    wm = [jnp.where(ok, ws_ref[g], 0.0).astype(BF16) for g in range(A_GROUPS)]
    lane = lax.broadcasted_iota(I32, (CHUNK, LANES), 1)
    gdim = aw // A_GROUPS
    gates = []
    for ci in range(tm // CHUNK):
        vc = vab[ci * CHUNK:(ci + 1) * CHUNK]
        parts = []
        for p in range(aw // LANES):
            vp = vc[:, p * LANES:(p + 1) * LANES]
            r0 = _dot(wm[2 * p], vp)
            r1 = _dot(wm[2 * p + 1], vp)
            parts.append(jnp.where(lane < gdim, r0, r1))
        gates.append(jnp.concatenate(parts, axis=1) + bias_ref[...])
    gate = jnp.concatenate(gates, axis=0)
    a_out = u * gate

    cosv = cos_ref[...]
    sinv = sin_ref[...]
    lane_t = lax.broadcasted_iota(I32, (tm, LANES), 1)
    first_half = (lane_t & (HEAD_DIM - 1)) < HEAD_DIM // 2

    def rope(z):
        parts = []
        for p in range(bw // LANES):
            xs = z[:, p * LANES:(p + 1) * LANES]
            rot = jnp.where(first_half, pltpu.roll(xs, LANES - HEAD_DIM // 2, 1), pltpu.roll(xs, HEAD_DIM // 2, 1))
            parts.append(xs * cosv + rot * sinv)
        return parts

    q_parts = rope(_dot(xn, w_ref[:, 2 * aw:2 * aw + bw]))
    k_parts = rope(_dot(xn, w_ref[:, 2 * aw + bw:2 * aw + 2 * bw]))
    v = _dot(xn, w_ref[:, 2 * aw + 2 * bw:2 * aw + 3 * bw])
    k = jnp.concatenate(k_parts, axis=1)
    q = jnp.concatenate(q_parts, axis=1) * (HEAD_DIM ** -0.5)

    if prompt:
        a_ref, q_ref, k_ref, v_ref, kaug_ref, vaug_ref, kmean_ref = outs
        a_ref[...] = a_out.astype(BF16)
        q_ref[...] = (q * LOG2E).astype(BF16)
        k_ref[...] = k
        v_ref[...] = v
        row = lax.broadcasted_iota(I32, (tm, LANES), 0)
        blk = i * (tm // MOBA_BLOCK) + lax.shift_right_logical(row, int(math.log2(MOBA_BLOCK)))
        lo = lane_t < HEAD_DIM
        hot_hi = jnp.where(blk == lane_t - HEAD_DIM, 1.0, 0.0)
        hot_lo = jnp.where(blk == lane_t, 1.0, 0.0)
        for p in range(bw // LANES):
            kp = k_parts[p]
            vp = v[:, p * LANES:(p + 1) * LANES]
            kaug_ref[:, (2 * p) * LANES:(2 * p + 1) * LANES] = jnp.where(lo, kp, hot_hi).astype(BF16)
            kaug_ref[:, (2 * p + 1) * LANES:(2 * p + 2) * LANES] = jnp.where(lo, hot_lo, kp).astype(BF16)
            vaug_ref[:, (2 * p) * LANES:(2 * p + 1) * LANES] = jnp.where(lo, vp, 1.0).astype(BF16)
            vaug_ref[:, (2 * p + 1) * LANES:(2 * p + 2) * LANES] = jnp.where(lo, 1.0, vp).astype(BF16)
        for b in range(tm // MOBA_BLOCK):
            kmean_ref[b] = jnp.mean(k[b * MOBA_BLOCK:(b + 1) * MOBA_BLOCK], axis=0, keepdims=True)
    else:
        a_ref, va_ref, q_ref, k_ref, v_ref = outs
        a_ref[...] = a_out.astype(BF16)
        va_ref[...] = va
        q_ref[...] = q
        k_ref[...] = k
        v_ref[...] = v


def _even_in(h, norm_g, w_in, ln_g, ln_b, ws, bias, cos_t, sin_t, *, tm, seq_l, prompt, a_width, b_width):
    t, d = h.shape
    nw = w_in.shape[1]
    row = lambda i: (i, 0)
    in_specs = [pl.BlockSpec((tm, d), row), _full((1, d)), _full((d, nw)), _full((1, a_width)), _full((1, a_width)),
                _full(ws.shape), _full(bias.shape), pl.BlockSpec((tm, LANES), row), pl.BlockSpec((tm, LANES), row)]
    if prompt:
        nb = t // MOBA_BLOCK
        out_shape = [jax.ShapeDtypeStruct((t, a_width), BF16), jax.ShapeDtypeStruct((t, b_width), BF16),
                     jax.ShapeDtypeStruct((t, b_width), F32), jax.ShapeDtypeStruct((t, b_width), F32),
                     jax.ShapeDtypeStruct((t, 2 * b_width), BF16), jax.ShapeDtypeStruct((t, 2 * b_width), BF16),
                     jax.ShapeDtypeStruct((nb, 1, b_width), F32)]
        out_specs = [pl.BlockSpec((tm, a_width), row), pl.BlockSpec((tm, b_width), row),
                     pl.BlockSpec((tm, b_width), row), pl.BlockSpec((tm, b_width), row),
                     pl.BlockSpec((tm, 2 * b_width), row), pl.BlockSpec((tm, 2 * b_width), row),
                     pl.BlockSpec((tm // MOBA_BLOCK, 1, b_width), lambda i: (i, 0, 0))]
    else:
        out_shape = [jax.ShapeDtypeStruct((t, a_width), BF16), jax.ShapeDtypeStruct((t, a_width), F32),
                     jax.ShapeDtypeStruct((t, b_width), F32), jax.ShapeDtypeStruct((t, b_width), F32),
                     jax.ShapeDtypeStruct((t, b_width), F32)]
        out_specs = [pl.BlockSpec((tm, a_width), row), pl.BlockSpec((tm, a_width), row),
                     pl.BlockSpec((tm, b_width), row), pl.BlockSpec((tm, b_width), row),
                     pl.BlockSpec((tm, b_width), row)]
    return pl.pallas_call(
        functools.partial(_even_in_kernel, seq_l=seq_l, prompt=prompt, a_width=a_width, b_width=b_width),
        grid=(t // tm,), in_specs=in_specs, out_specs=out_specs, out_shape=out_shape,
        compiler_params=_cparams(("arbitrary",)), name="even_in_prompt" if prompt else "even_in_sample",
    )(h, norm_g, w_in, ln_g, ln_b, ws, bias, cos_t, sin_t)


def _moba_prompt_kernel(q_ref, k0_ref, k1_ref, v0_ref, v1_ref, kmh_ref, kml_ref, o_ref, p_scr, m_scr):
    tq = q_ref.shape[0]
    i = pl.program_id(1)
    q = q_ref[...]
    qf = q.astype(F32)
    lane = lax.broadcasted_iota(I32, (tq, LANES), 1)
    k_refs = (k0_ref, k1_ref)
    v_refs = (v0_ref, v1_ref)
    neg_inf = jnp.float32(-jnp.inf)

    qa, q_own = [], []
    for e in range(2):
        in_blk = (lane >= HEAD_DIM) if e == 0 else (lane < HEAD_DIM)
        blk = lane - HEAD_DIM if e == 0 else lane
        sc = _dot(q, kmh_ref[0, e]) + _dot(q, kml_ref[0, e])
        cand = jnp.where(in_blk & (blk < i), sc, neg_inf)
        sel = jnp.zeros((tq, LANES), jnp.bool_)
        for _ in range(MOBA_TOPK):
            mx = jnp.max(cand, axis=-1, keepdims=True)
            first = jnp.min(jnp.where(cand == mx, blk, jnp.int32(2 ** 30)), axis=-1, keepdims=True)
            pick = (blk == first) & in_blk & (mx > neg_inf)
            sel = sel | pick
            cand = jnp.where(pick, neg_inf, cand)
        qa.append(jnp.where(in_blk, jnp.where(sel, 0.0, NEG), qf).astype(BF16))
        q_own.append(jnp.where(in_blk, 0.0, qf).astype(BF16))

    n_blk = k0_ref.shape[0] // MOBA_BLOCK

    def offset(blk_idx):
        return pl.multiple_of(jnp.minimum(blk_idx, n_blk - 1) * MOBA_BLOCK, MOBA_BLOCK)

    def probs(e, qe, blk_idx, mask=None):
        s = _dot_nt(qe, k_refs[e][pl.ds(offset(blk_idx), MOBA_BLOCK), :])
        if mask is not None:
            s = jnp.where(mask, s, NEG)
        m = jnp.max(s, axis=-1, keepdims=True)
        return m, jnp.exp2(s - m).astype(BF16)

    def values(e, p, blk_idx):
        return _dot(p, v_refs[e][pl.ds(offset(blk_idx), MOBA_BLOCK), :])

    def stash(g):
        slot = g & 1
        for e in range(2):
            for u in range(MOBA_UNROLL):
                m_u, p_u = probs(e, qa[e], g * MOBA_UNROLL + u)
                p_scr[slot, e * MOBA_UNROLL + u] = p_u
                m_scr[slot, e * MOBA_UNROLL + u] = m_u

    def merge(g, ms, accs):
        slot = g & 1
        new_m, new_acc = [], []
        for e in range(2):
            m_us = [m_scr[slot, e * MOBA_UNROLL + u] for u in range(MOBA_UNROLL)]
            m_new = ms[e]
            for m_u in m_us:
                m_new = jnp.maximum(m_new, m_u)
            acc = jnp.exp2(ms[e] - m_new) * accs[e]
            for u, m_u in enumerate(m_us):
                o_u = values(e, p_scr[slot, e * MOBA_UNROLL + u], g * MOBA_UNROLL + u)
                acc = acc + jnp.exp2(m_u - m_new) * o_u
            new_m.append(m_new)
            new_acc.append(acc)
        return tuple(new_m), tuple(new_acc)

    r = lax.broadcasted_iota(I32, (tq, MOBA_BLOCK), 0)
    c = lax.broadcasted_iota(I32, (tq, MOBA_BLOCK), 1)
    own = [probs(e, q_own[e], i, c <= r) for e in range(2)]
    ms = tuple(m for m, _ in own)
    accs = tuple(values(e, own[e][1], i) for e in range(2))

    def body(g, carry):
        ms, accs = merge(g - 1, *carry)
        stash(g)
        return ms, accs

    n_groups = lax.shift_right_logical(i + (MOBA_UNROLL - 1), int(math.log2(MOBA_UNROLL)))
    stash(0)
    ms, accs = lax.fori_loop(1, n_groups, body, (ms, accs))
    _, accs = merge(jnp.maximum(n_groups - 1, 0), ms, accs)
    outs = [a / pltpu.roll(a, HEAD_DIM, 1) for a in accs]
    o_ref[...] = jnp.where(lane < HEAD_DIM, outs[0], outs[1]).astype(o_ref.dtype)


def _moba_prompt(q, kaug, vaug, km_hi, km_lo):
    t, bw = q.shape
    n_pairs = bw // LANES
    tq = MOBA_BLOCK
    col = lambda e: (lambda p, i, _e=e: (0, 2 * p + _e))
    in_specs = [pl.BlockSpec((tq, LANES), lambda p, i: (i, p)),
                pl.BlockSpec((t, LANES), col(0)), pl.BlockSpec((t, LANES), col(1)),
                pl.BlockSpec((t, LANES), col(0)), pl.BlockSpec((t, LANES), col(1)),
                pl.BlockSpec((1, 2, LANES, LANES), lambda p, i: (p, 0, 0, 0)),
                pl.BlockSpec((1, 2, LANES, LANES), lambda p, i: (p, 0, 0, 0))]
    return pl.pallas_call(
        _moba_prompt_kernel, grid=(n_pairs, t // tq), in_specs=in_specs,
        out_specs=pl.BlockSpec((tq, LANES), lambda p, i: (i, p)),
        out_shape=jax.ShapeDtypeStruct((t, bw), BF16),
        scratch_shapes=[pltpu.VMEM((2, 2 * MOBA_UNROLL, tq, MOBA_BLOCK), BF16),
                        pltpu.VMEM((2, 2 * MOBA_UNROLL, tq, 1), F32)],
        compiler_params=_cparams(("arbitrary", "arbitrary")), name="moba_prompt",
    )(q, kaug, kaug, vaug, vaug, km_hi, km_lo)


def _moba_sample_keys_kernel(pt_ref, *refs, n_steps):
    npg = KV_PAGES_PER_STEP
    pages = refs[:npg]
    qbd_ref, knew_ref, p_ref, pown_ref, linv_ref, s_scr, kmean_scr, bmax_scr = refs[npg:]
    c = pl.program_id(1)
    hw, page = pages[0].shape[1], pages[0].shape[2]
    ppb = MOBA_BLOCK // page
    bps = npg // ppb
    qbd = qbd_ref[0]
    n_rows = qbd.shape[0]
    n_blocks = n_steps * bps
    lane = lax.broadcasted_iota(I32, (n_rows, LANES), 1)
    neg_inf = jnp.float32(-jnp.inf)

    @pl.when(c == 0)
    def _():
        bmax_scr[...] = jnp.full((n_rows, LANES), neg_inf, F32)
        kmean_scr[...] = jnp.zeros_like(kmean_scr)

    lane_k = lax.broadcasted_iota(I32, (hw, LANES), 1)
    for j in range(bps):
        kb = jnp.concatenate([pages[j * ppb + x][0] for x in range(ppb)], axis=1)
        blk = c * bps + j
        kmean_scr[...] = jnp.where(lane_k == blk, jnp.mean(kb, axis=1, keepdims=True), kmean_scr[...])
        s = _dot(qbd, kb.astype(BF16))
        s_scr[:, pl.ds(pl.multiple_of(blk * MOBA_BLOCK, MOBA_BLOCK), MOBA_BLOCK)] = s
        bmax_scr[...] = jnp.where(lane == blk, jnp.max(s, axis=-1, keepdims=True), bmax_scr[...])

    @pl.when(c == n_steps - 1)
    def _():
        kmh, kml = _split_bf16(kmean_scr[...])
        sc = _dot(qbd, kmh) + _dot(qbd, kml)
        cand = jnp.where(lane < n_blocks, sc, neg_inf)
        sel = jnp.zeros((n_rows, LANES), jnp.bool_)
        for _ in range(min(MOBA_TOPK, n_blocks)):
            mx = jnp.max(cand, axis=-1, keepdims=True)
            first = jnp.min(jnp.where(cand == mx, lane, jnp.int32(2 ** 30)), axis=-1, keepdims=True)
            pick = lane == first
            sel = sel | pick
            cand = jnp.where(pick, neg_inf, cand)
        so = _dot_nt(qbd, knew_ref[0].astype(BF16))
        n_new = so.shape[1]
        rr = lax.broadcasted_iota(I32, (n_rows, n_new), 0)
        ss = lax.broadcasted_iota(I32, (n_rows, n_new), 1)
        so = jnp.where(ss <= (rr & (n_new - 1)), so, NEG)
        m = jnp.maximum(jnp.max(jnp.where(sel, bmax_scr[...], neg_inf), axis=-1, keepdims=True),
                        jnp.max(so, axis=-1, keepdims=True))
        po = jnp.exp(so - m)
        l = jnp.sum(po, axis=-1, keepdims=True)
        self32 = sel.astype(F32)
        for b in range(n_blocks):
            on = jnp.sum(jnp.where(lane == b, self32, 0.0), axis=-1, keepdims=True)
            sb = s_scr[:, b * MOBA_BLOCK:(b + 1) * MOBA_BLOCK]
            pb = (jnp.exp(sb - m) * on).astype(BF16)
            l = l + jnp.sum(pb.astype(F32), axis=-1, keepdims=True)
            p_ref[0, :, b * MOBA_BLOCK:(b + 1) * MOBA_BLOCK] = pb
        pown_ref[0] = jnp.concatenate([po, jnp.zeros((n_rows, LANES - n_new), F32)], axis=1)
        linv_ref[0] = jnp.broadcast_to(1.0 / l, (n_rows, LANES))


def _moba_sample_values_kernel(pt_ref, *refs, n_steps, n_heads):
    npg = KV_PAGES_PER_STEP
    pages = refs[:npg]
    p_ref, pown_ref, linv_ref, vnew_ref, o_ref, acc_scr = refs[npg:]
    c = pl.program_id(1)
    vb = jnp.concatenate([pg[0] for pg in pages], axis=1).astype(BF16)
    part = _dot_nt(p_ref[0], vb)

    @pl.when(c == 0)
    def _():
        acc_scr[...] = part

    @pl.when(c > 0)
    def _():
        acc_scr[...] += part

    @pl.when(c == n_steps - 1)
    def _():
        vnew = vnew_ref[0]
        n_new = vnew.shape[0]
        o = (acc_scr[...] + _dot(pown_ref[0][:, :n_new].astype(BF16), vnew.astype(BF16))) * linv_ref[0][:, :1]
        n_rows, hw = o.shape
        rr = lax.broadcasted_iota(I32, (n_rows, hw), 0)
        ll = lax.broadcasted_iota(I32, (n_rows, hw), 1)
        keep = (lax.shift_right_logical(rr, int(math.log2(n_new)))
                == lax.shift_right_logical(ll, int(math.log2(HEAD_DIM))))
        o = jnp.where(keep, o, 0.0).reshape(n_heads, n_new, hw)
        o_ref[0] = jnp.sum(o, axis=0).astype(o_ref.dtype)


def _moba_sample(q, k, v, cache_k, cache_v, page_table, n_batch):
    hw = q.shape[1]
    n_heads = hw // HEAD_DIM
    n_new = q.shape[0] // n_batch
    n_phys, page = cache_k.shape[0], cache_k.shape[1]
    n_pages = page_table.shape[1]
    npg = KV_PAGES_PER_STEP
    assert n_pages % npg == 0 and (n_pages * page) % MOBA_BLOCK == 0 and MOBA_BLOCK % page == 0
    n_steps = n_pages // npg
    past = n_pages * page
    n_blocks = past // MOBA_BLOCK
    assert n_blocks <= LANES and n_new & (n_new - 1) == 0
    n_rows = n_heads * n_new
    ck = cache_k.transpose(0, 2, 3, 1).reshape(n_phys, hw, page)
    cv = cache_v.transpose(0, 2, 3, 1).reshape(n_phys, hw, page)
    pt = page_table.reshape(-1).astype(I32)
    q3 = q.reshape(n_batch, n_new, n_heads, HEAD_DIM)
    eye = jnp.eye(n_heads, dtype=F32)
    qbd = (q3.transpose(0, 2, 1, 3)[:, :, :, None, :] * eye[None, :, None, :, None]).reshape(n_batch, n_rows, hw)
    qbd = qbd.astype(BF16)
    k3 = k.reshape(n_batch, n_new, hw)
    v3 = v.reshape(n_batch, n_new, hw)

    def page_spec(s):
        return pl.BlockSpec((1, hw, page), lambda b, c, pt_ref, _s=s: (pt_ref[b * n_pages + c * npg + _s], 0, 0))

    per_b = lambda shape: pl.BlockSpec(shape, lambda b, c, pt_ref: (b,) + (0,) * (len(shape) - 1))
    p, pown, linv = pl.pallas_call(
        functools.partial(_moba_sample_keys_kernel, n_steps=n_steps),
        grid_spec=pltpu.PrefetchScalarGridSpec(
            num_scalar_prefetch=1, grid=(n_batch, n_steps),
            in_specs=[page_spec(s) for s in range(npg)] + [per_b((1, n_rows, hw)), per_b((1, n_new, hw))],
            out_specs=[per_b((1, n_rows, past)), per_b((1, n_rows, LANES)), per_b((1, n_rows, LANES))],
            scratch_shapes=[pltpu.VMEM((n_rows, past), F32), pltpu.VMEM((hw, LANES), F32),
                            pltpu.VMEM((n_rows, LANES), F32)]),
        out_shape=[jax.ShapeDtypeStruct((n_batch, n_rows, past), BF16),
                   jax.ShapeDtypeStruct((n_batch, n_rows, LANES), F32),
                   jax.ShapeDtypeStruct((n_batch, n_rows, LANES), F32)],
        compiler_params=_cparams(("arbitrary", "arbitrary")), name="moba_sample_keys",
    )(pt, *([ck] * npg), qbd, k3)

    out = pl.pallas_call(
        functools.partial(_moba_sample_values_kernel, n_steps=n_steps, n_heads=n_heads),
        grid_spec=pltpu.PrefetchScalarGridSpec(
            num_scalar_prefetch=1, grid=(n_batch, n_steps),
            in_specs=[page_spec(s) for s in range(npg)]
            + [pl.BlockSpec((1, n_rows, npg * page), lambda b, c, pt_ref: (b, 0, c)),
               per_b((1, n_rows, LANES)), per_b((1, n_rows, LANES)), per_b((1, n_new, hw))],
            out_specs=per_b((1, n_new, hw)),
            scratch_shapes=[pltpu.VMEM((n_rows, hw), F32)]),
        out_shape=jax.ShapeDtypeStruct((n_batch, n_new, hw), BF16),
        compiler_params=_cparams(("arbitrary", "arbitrary")), name="moba_sample_values",
    )(pt, *([cv] * npg), p, pown, linv, v3)
    return out.reshape(n_batch * n_new, hw)


def _ple(h, p_ref, wp_ref, wg_ref, pn_ref):
    gate = jax.nn.sigmoid(_dot(_rms(h, pn_ref[...]).astype(BF16), wg_ref[...]))
    return h + _dot(p_ref[...].astype(BF16), wp_ref[...]) * gate


def _even_tail_kernel(h_ref, a_ref, b_ref, wo_ref, nf_ref, w1_ref, w3_ref, w2_ref, p_ref, wp_ref, wg_ref, pn_ref,
                      nn_ref, h_out_ref, xn_out_ref, *, ff_chunks):
    aw = a_ref.shape[1]
    h1 = h_ref[...] + _dot(a_ref[...], wo_ref[0:aw, :]) + _dot(b_ref[...], wo_ref[aw:, :])
    xn = _rms(h1, nf_ref[...]).astype(BF16)
    ff = w1_ref.shape[1]
    cw = ff // ff_chunks
    hid = []
    for ci in range(ff_chunks):
        sl = slice(ci * cw, (ci + 1) * cw)
        hid.append((jax.nn.silu(_dot(xn, w1_ref[:, sl])) * _dot(xn, w3_ref[:, sl])).astype(BF16))
    acc = h1 + _dot(jnp.concatenate(hid, axis=1), w2_ref[...])
    h3 = _ple(acc, p_ref, wp_ref, wg_ref, pn_ref)
    h_out_ref[...] = h3
    xn_out_ref[...] = _rms(h3, nn_ref[...]).astype(BF16)


def _even_tail(h, a, b, w_out, norm_ffn, w1, w3, w2, p, wp, wg, pn, next_norm, *, tm):
    t, d = h.shape
    ff = w1.shape[1]
    ff_chunks = max((n for n in range(1, ff // MXU_WIDTH + 1) if ff % (n * MXU_WIDTH) == 0 and ff // n >= 1024),
                    default=1)
    row = lambda i: (i, 0)
    in_specs = [pl.BlockSpec((tm, d), row), pl.BlockSpec((tm, a.shape[1]), row), pl.BlockSpec((tm, b.shape[1]), row),
                _full(w_out.shape), _full((1, d)), _full(w1.shape), _full(w3.shape), _full(w2.shape),
                pl.BlockSpec((tm, p.shape[1]), row), _full(wp.shape), _full(wg.shape), _full((1, d)), _full((1, d))]
    return pl.pallas_call(
        functools.partial(_even_tail_kernel, ff_chunks=ff_chunks),
        grid=(t // tm,), in_specs=in_specs,
        out_specs=[pl.BlockSpec((tm, d), row), pl.BlockSpec((tm, d), row)],
        out_shape=[jax.ShapeDtypeStruct((t, d), F32), jax.ShapeDtypeStruct((t, d), BF16)],
        compiler_params=_cparams(("arbitrary",)), name="even_tail",
    )(h, a, b, w_out, norm_ffn, w1, w3, w2, p, wp, wg, pn, next_norm)


def _odd_tail(conv, d_in, h3, refs, count_scr, n_experts):
    (cb_ref, clg_ref, clb_ref, pw_ref, ps_ref, wo_ref, nf_ref, rwh_ref, rwl_ref, rb_ref) = refs
    rows = conv.shape[0]
    cw = conv.shape[1]
    cact = jax.nn.silu(_layernorm(conv + cb_ref[...], clg_ref[...], clb_ref[...]))
    db = d_in.astype(BF16)
    gdim = d_in.shape[1] // len(POOL_WINDOWS)
    dparts = [_dot(db[:, g * gdim:(g + 1) * gdim], pw_ref[g]) for g in range(len(POOL_WINDOWS))]
    dmix = jnp.concatenate(dparts, axis=1) * ps_ref[...]
    h4 = h3 + _dot(cact.astype(BF16), wo_ref[0:cw, :]) + _dot(dmix.astype(BF16), wo_ref[cw:, :])
    xn = _rms(h4, nf_ref[...])
    xh, xl = _split_bf16(xn)
    logits = _dot(xh, rwh_ref[...]) + _dot(xl, rwh_ref[...]) + _dot(xh, rwl_ref[...]) + rb_ref[...]
    lane = lax.broadcasted_iota(I32, (rows, LANES), 1)
    neg_inf = jnp.float32(-jnp.inf)
    cand = jnp.where(lane < n_experts, logits, neg_inf)
    v1 = jnp.max(cand, axis=-1, keepdims=True)
    e1 = jnp.min(jnp.where(cand == v1, lane, jnp.int32(2 ** 30)), axis=-1, keepdims=True)
    cand2 = jnp.where(lane == e1, neg_inf, cand)
    v2 = jnp.max(cand2, axis=-1, keepdims=True)
    e2 = jnp.min(jnp.where(cand2 == v2, lane, jnp.int32(2 ** 30)), axis=-1, keepdims=True)
    ex = jnp.exp(v2 - v1)
    g1 = 1.0 / (1.0 + ex)
    g2 = ex / (1.0 + ex)
    chosen = jnp.where((lane == e1) | (lane == e2), 1.0, 0.0).astype(BF16)
    rr = lax.broadcasted_iota(I32, (rows, rows), 0)
    cc = lax.broadcasted_iota(I32, (rows, rows), 1)
    before = jnp.where(cc < rr, 1.0, 0.0).astype(BF16)
    prefix = _dot(before, chosen) + count_scr[...]
    r1 = _lane_pick(prefix, lane, e1)
    r2 = _lane_pick(prefix, lane, e2)
    count_scr[...] = count_scr[...] + jnp.sum(chosen.astype(F32), axis=0, keepdims=True)
    meta = jnp.zeros((rows, LANES), F32)
    for idx, val in enumerate((e1.astype(F32), e2.astype(F32), g1, g2, r1, r2)):
        meta = jnp.where(lane == idx, val, meta)
    return h4, xn, meta


def _odd_prompt_kernel(xm_ref, xh_ref, h3_ref, wi_ref, cw_ref, *refs, n_experts, conv_hist, pool_hist):
    tail_refs = refs[:10]
    cnt_in_ref, h4_ref, xn_ref, meta_ref, cnt_ref, cstate_ref, pstate_ref, count_scr = refs[10:]
    i = pl.program_id(0)
    tm = xm_ref.shape[0]
    cwid = cw_ref.shape[1]

    @pl.when(i == 0)
    def _():
        count_scr[...] = cnt_in_ref[...]

    x = jnp.concatenate([xh_ref[...], xm_ref[...]], axis=0)
    z = _dot(x, wi_ref[...])
    row = lax.broadcasted_iota(I32, (HALO + tm, 1), 0)
    z = jnp.where((row >= HALO) | (i > 0), z, 0.0)
    c_in = z[:, :cwid] * jax.nn.sigmoid(z[:, cwid:2 * cwid])
    dx = z[:, 2 * cwid:]

    cwv = cw_ref[...]
    width = cwv.shape[0]
    lead = HALO - (width - 1)

    def tap(kp):
        k = kp - lead
        return cwv[k:k + 1, :] if 0 <= k < width else None

    ext = c_in
    conv = None
    for b in range(8):
        n_a = (HALO // 8 + 1) if b == 0 else HALO // 8
        rows_b = tm if b == 0 else tm + 8
        ub = None
        for a in range(n_a):
            w = tap(8 * a + b)
            if w is None:
                continue
            term = ext[8 * a:8 * a + rows_b] * w
            ub = term if ub is None else ub + term
        if ub is None:
            continue
        piece = ub if b == 0 else pltpu.roll(ub, rows_b - b, 0)[:tm]
        conv = piece if conv is None else conv + piece

    s = dx
    sums = {}
    span = 1
    while span < max(POOL_WINDOWS):
        s = s + pltpu.roll(s, span, 0)
        span *= 2
        sums[span] = s
    gdim = dx.shape[1] // len(POOL_WINDOWS)
    pos = i * tm + lax.broadcasted_iota(I32, (tm, 1), 0)
    pooled = []
    for g, w in enumerate(POOL_WINDOWS):
        win = sums[w][HALO:, g * gdim:(g + 1) * gdim]
        cnt = jnp.minimum(pos + 1, w).astype(F32)
        pooled.append(win / cnt)
    dmain = dx[HALO:]
    d_in = jnp.concatenate(pooled, axis=1) - dmain

    h4, xn, meta = _odd_tail(conv, d_in, h3_ref[...], tail_refs, count_scr, n_experts)
    h4_ref[...] = h4
    _to_row_tiles(xn_ref, xn)
    meta_ref[...] = meta
    cnt_ref[...] = count_scr[...]
    ctail = c_in[tm:]
    cstate_ref[...] = pltpu.roll(ctail, conv_hist, 0)[:conv_hist] if conv_hist < HALO else ctail
    ptail_rows = 8 * (-(-pool_hist // 8))
    ptail = dmain[tm - ptail_rows:]
    pstate_ref[...] = pltpu.roll(ptail, pool_hist, 0)[:pool_hist] if pool_hist < ptail_rows else ptail


def _odd_sample_kernel(x_ref, h3_ref, wi_ref, cw_ref, *refs, n_experts, n_batch):
    tail_refs = refs[:10]
    cnt_in_ref, chist_ref, phist_ref, h4_ref, xn_ref, meta_ref, cnt_ref, cstate_ref, pstate_ref, count_scr = refs[10:]
    rows = x_ref.shape[0]
    n_new = rows // n_batch
    cwid = cw_ref.shape[1]
    count_scr[...] = cnt_in_ref[...]
    z = _dot(x_ref[...], wi_ref[...])
    c_in = z[:, :cwid] * jax.nn.sigmoid(z[:, cwid:2 * cwid])
    dx = z[:, 2 * cwid:]
    cwv = cw_ref[...]
    width = cwv.shape[0]
    hist_c = chist_ref[...]
    xp = [hist_c[j] for j in range(width - 1)] + [c_in[t * n_batch:(t + 1) * n_batch] for t in range(n_new)]
    conv = []
    for t in range(n_new):
        acc = xp[t] * cwv[0:1, :]
        for k in range(1, width):
            acc = acc + xp[t + k] * cwv[k:k + 1, :]
        conv.append(acc)
    conv = jnp.concatenate(conv, axis=0)
    cstate_ref[...] = jnp.stack(xp[n_new:], axis=0)

    hist_p = phist_ref[...]
    n_ph = hist_p.shape[0]
    xq = [hist_p[j] for j in range(n_ph)] + [dx[t * n_batch:(t + 1) * n_batch] for t in range(n_new)]
    gdim = dx.shape[1] // len(POOL_WINDOWS)
    lane = lax.broadcasted_iota(I32, (n_batch, dx.shape[1]), 1)
    d_in = []
    for t in range(n_new):
        run = xq[n_ph + t]
        tot = jnp.zeros_like(run)
        done = 1
        for g, w in enumerate(POOL_WINDOWS):
            for j in range(done, w):
                run = run + xq[n_ph + t - j]
            done = w
            in_g = (lane >= g * gdim) & (lane < (g + 1) * gdim)
            tot = jnp.where(in_g, run / float(w), tot)
        d_in.append(tot - xq[n_ph + t])
    d_in = jnp.concatenate(d_in, axis=0)
    pstate_ref[...] = jnp.stack(xq[n_new:], axis=0)

    h4, xn, meta = _odd_tail(conv, d_in, h3_ref[...], tail_refs, count_scr, n_experts)
    h4_ref[...] = h4
    _to_row_tiles(xn_ref, xn)
    meta_ref[...] = meta
    cnt_ref[...] = count_scr[...]


def _odd_common_args(conv_b, c_ln_g, c_ln_b, pool_w, pool_scale, w_out, norm_ffn, rw_hi, rw_lo, rb):
    args = (conv_b, c_ln_g, c_ln_b, pool_w, pool_scale, w_out, norm_ffn, rw_hi, rw_lo, rb)
    return args, [_full(a.shape) for a in args]


def _odd_prompt(xn3, h3, w_in, conv_w, common, counts_in, *, tm, n_experts):
    t, d = h3.shape
    cwid = conv_w.shape[1]
    conv_hist = conv_w.shape[0] - 1
    pool_hist = max(POOL_WINDOWS) - 1
    args, specs = common
    row = lambda i: (i, 0)
    hpb = tm // HALO
    in_specs = [pl.BlockSpec((tm, d), row), pl.BlockSpec((HALO, d), lambda i: (jnp.maximum(i * hpb - 1, 0), 0)),
                pl.BlockSpec((tm, d), row), _full(w_in.shape), _full(conv_w.shape)] + specs + [_full((1, LANES))]
    dwid = w_in.shape[1] - 2 * cwid
    return pl.pallas_call(
        functools.partial(_odd_prompt_kernel, n_experts=n_experts, conv_hist=conv_hist, pool_hist=pool_hist),
        grid=(t // tm,), in_specs=in_specs,
        out_specs=[pl.BlockSpec((tm, d), row), pl.BlockSpec((tm, d // LANES, LANES), lambda i: (i, 0, 0)),
                   pl.BlockSpec((tm, LANES), row),
                   _whole((1, LANES)), _whole((conv_hist, cwid)), _whole((pool_hist, dwid))],
        out_shape=[jax.ShapeDtypeStruct((t, d), F32), jax.ShapeDtypeStruct((t, d // LANES, LANES), F32),
                   jax.ShapeDtypeStruct((t, LANES), F32), jax.ShapeDtypeStruct((1, LANES), F32),
                   jax.ShapeDtypeStruct((conv_hist, cwid), F32), jax.ShapeDtypeStruct((pool_hist, dwid), F32)],
        scratch_shapes=[pltpu.VMEM((1, LANES), F32)],
        compiler_params=_cparams(("arbitrary",)), name="odd_prompt",
    )(xn3, xn3, h3, w_in, conv_w, *args, counts_in)


def _odd_sample(xn3, h3, w_in, conv_w, common, counts_in, conv_hist_t, pool_hist_t, *, n_batch, n_experts):
    t, d = h3.shape
    cwid = conv_w.shape[1]
    dwid = w_in.shape[1] - 2 * cwid
    args, specs = common
    in_specs = [_full((t, d)), _full((t, d)), _full(w_in.shape), _full(conv_w.shape)] + specs + \
               [_full((1, LANES)), _full(conv_hist_t.shape), _full(pool_hist_t.shape)]
    return pl.pallas_call(
        functools.partial(_odd_sample_kernel, n_experts=n_experts, n_batch=n_batch),
        grid=(1,), in_specs=in_specs,
        out_specs=[_whole((t, d)), _whole((t, d // LANES, LANES)), _whole((t, LANES)), _whole((1, LANES)),
                   _whole(conv_hist_t.shape), _whole(pool_hist_t.shape)],
        out_shape=[jax.ShapeDtypeStruct((t, d), F32), jax.ShapeDtypeStruct((t, d // LANES, LANES), F32),
                   jax.ShapeDtypeStruct((t, LANES), F32), jax.ShapeDtypeStruct((1, LANES), F32),
                   jax.ShapeDtypeStruct(conv_hist_t.shape, F32), jax.ShapeDtypeStruct(pool_hist_t.shape, F32)],
        scratch_shapes=[pltpu.VMEM((1, LANES), F32)],
        compiler_params=_cparams(("arbitrary",)), name="odd_sample",
    )(xn3, h3, w_in, conv_w, *args, counts_in, conv_hist_t, pool_hist_t)


def _row_copy(src, s, dst, d, sem):
    return pltpu.make_async_copy(src.at[pl.ds(s, 1)], dst.at[pl.ds(d, 1)], sem)


def _rows_wait(ref, n, sem):
    pltpu.make_async_copy(ref.at[pl.ds(0, n)], ref.at[pl.ds(0, n)], sem).wait()


def _to_row_tiles(ref, x):
    for s in range(ref.shape[1]):
        ref[:, s, :] = x[:, s * LANES:(s + 1) * LANES]


def _from_row_tiles(ref):
    return jnp.concatenate([ref[:, s, :] for s in range(ref.shape[1])], axis=1)


def _moe_scatter_kernel(dest_ref, x_ref, buf_in_hbm, buf_hbm, sem, *, tm):
    del buf_in_hbm
    base = pl.program_id(0) * tm

    def start(r, carry):
        for k in range(TOP_K):
            _row_copy(x_ref, r, buf_hbm, dest_ref[(base + r) * TOP_K + k], sem).start()
        return carry

    lax.fori_loop(0, tm, start, 0)
    _rows_wait(buf_hbm, tm * TOP_K, sem)


def _moe_scatter(dest, x, buf, *, tm):
    t = x.shape[0]
    any_spec = pl.BlockSpec(memory_space=pl.ANY)
    return pl.pallas_call(
        functools.partial(_moe_scatter_kernel, tm=tm),
        grid_spec=pltpu.PrefetchScalarGridSpec(
            num_scalar_prefetch=1, grid=(t // tm,),
            in_specs=[pl.BlockSpec((tm,) + x.shape[1:], lambda i, dref: (i, 0, 0)), any_spec], out_specs=any_spec,
            scratch_shapes=[pltpu.SemaphoreType.DMA(())]),
        out_shape=jax.ShapeDtypeStruct(buf.shape, buf.dtype),
        input_output_aliases={2: 0},
        compiler_params=_cparams(("arbitrary",)), name="moe_scatter",
    )(dest, x, buf)


def _moe_experts_kernel(be_ref, nu_ref, x_ref, w1_ref, w3_ref, w2_ref, y_ref, xb_scr, acc_scr, *, nj):
    i = pl.program_id(0)
    j = pl.program_id(1)
    used = i < nu_ref[0]

    @pl.when(used)
    def _():
        @pl.when(j == 0)
        def _():
            xb_scr[...] = _from_row_tiles(x_ref).astype(BF16)

        xb = xb_scr[...]
        hid = jax.nn.silu(_dot(xb, w1_ref[0])) * _dot(xb, w3_ref[0])
        part = _dot(hid.astype(BF16), w2_ref[0])

        @pl.when(j == 0)
        def _():
            acc_scr[...] = part

        @pl.when(j > 0)
        def _():
            acc_scr[...] += part

        @pl.when(j == nj - 1)
        def _():
            _to_row_tiles(y_ref, acc_scr[...])

    @pl.when(jnp.logical_not(used) & (j == nj - 1))
    def _():
        y_ref[...] = jnp.zeros_like(y_ref)


def _moe_experts(blk_e, n_used, xbuf, w1, w3, w2, *, tm, tf):
    rows, ns, _ = xbuf.shape
    d = ns * LANES
    ff = w1.shape[2]
    nj = ff // tf
    nb = rows // tm
    live_j = lambda i, j, nu: jnp.where(i < nu[0], j, nj - 1)
    return pl.pallas_call(
        functools.partial(_moe_experts_kernel, nj=nj),
        grid_spec=pltpu.PrefetchScalarGridSpec(
            num_scalar_prefetch=2, grid=(nb, nj),
            in_specs=[pl.BlockSpec((tm, ns, LANES), lambda i, j, be, nu: (i, 0, 0)),
                      pl.BlockSpec((1, d, tf), lambda i, j, be, nu: (be[i], 0, live_j(i, j, nu))),
                      pl.BlockSpec((1, d, tf), lambda i, j, be, nu: (be[i], 0, live_j(i, j, nu))),
                      pl.BlockSpec((1, tf, d), lambda i, j, be, nu: (be[i], live_j(i, j, nu), 0))],
            out_specs=pl.BlockSpec((tm, ns, LANES), lambda i, j, be, nu: (i, 0, 0)),
            scratch_shapes=[pltpu.VMEM((tm, d), BF16), pltpu.VMEM((tm, d), F32)]),
        out_shape=jax.ShapeDtypeStruct((rows, ns, LANES), F32),
        compiler_params=_cparams(("arbitrary", "arbitrary")), name="moe_experts",
    )(blk_e, n_used, xbuf, w1, w3, w2)


def _moe_combine_kernel(dest_ref, y_hbm, h4_ref, meta_ref, p_ref, wp_ref, wg_ref, pn_ref, fn_ref, o_ref,
                        y0_scr, y1_scr, sem, *, tm):
    base = pl.program_id(0) * tm
    bufs = (y0_scr, y1_scr)

    def start(r, carry):
        for k in range(TOP_K):
            _row_copy(y_hbm, dest_ref[(base + r) * TOP_K + k], bufs[k], r, sem).start()
        return carry

    lax.fori_loop(0, tm, start, 0)
    for k in range(TOP_K):
        _rows_wait(bufs[k], tm, sem)
    meta = meta_ref[...]
    lane = lax.broadcasted_iota(I32, meta.shape, 1)
    g1 = jnp.sum(jnp.where(lane == 2, meta, 0.0), axis=-1, keepdims=True)
    g2 = jnp.sum(jnp.where(lane == 3, meta, 0.0), axis=-1, keepdims=True)
    h5 = h4_ref[...] + _from_row_tiles(y0_scr) * g1 + _from_row_tiles(y1_scr) * g2
    h6 = _ple(h5, p_ref, wp_ref, wg_ref, pn_ref)
    o_ref[...] = _rms(h6, fn_ref[...])


def _moe_combine(dest, ybuf, h4, meta, p, wp, wg, pn, fn, *, tm):
    t, d = h4.shape
    row = lambda i, dref: (i, 0)
    full = lambda shape: pl.BlockSpec(shape, lambda i, dref, _n=len(shape): (0,) * _n)
    return pl.pallas_call(
        functools.partial(_moe_combine_kernel, tm=tm),
        grid_spec=pltpu.PrefetchScalarGridSpec(
            num_scalar_prefetch=1, grid=(t // tm,),
            in_specs=[pl.BlockSpec(memory_space=pl.ANY), pl.BlockSpec((tm, d), row), pl.BlockSpec((tm, LANES), row),
                      pl.BlockSpec((tm, p.shape[1]), row), full(wp.shape), full(wg.shape), full((1, d)), full((1, d))],
            out_specs=pl.BlockSpec((tm, d), row),
            scratch_shapes=[pltpu.VMEM((tm, d // LANES, LANES), F32), pltpu.VMEM((tm, d // LANES, LANES), F32),
                            pltpu.SemaphoreType.DMA(())]),
        out_shape=jax.ShapeDtypeStruct((t, d), F32),
        compiler_params=_cparams(("arbitrary",)), name="moe_combine",
    )(dest, ybuf, h4, meta, p, wp, wg, pn, fn)


def _rope_tables(pos):
    half = HEAD_DIM // 2
    inv = ROPE_THETA ** (-jnp.arange(half, dtype=F32) * (2.0 / HEAD_DIM))
    ang = pos.astype(F32)[:, None] * inv[None, :]
    cos, sin = jnp.cos(ang), jnp.sin(ang)
    reps = LANES // HEAD_DIM
    cos_t = jnp.tile(jnp.concatenate([cos, cos], axis=1), (1, reps))
    sin_t = jnp.tile(jnp.concatenate([-sin, sin], axis=1), (1, reps))
    return cos_t, sin_t


def _row_tile(t, pref):
    for tm in (pref, 256, 128, 64, 32, 16, 8):
        if tm <= t and t % tm == 0:
            return tm
    return t


def kernel(x_prompt, x_sample, cache_k, cache_v, state_conv, state_pool, page_table, p_prompt, p_sample, norm_mix_e, w_in_e, a_ln_g, a_ln_b, a_ws, a_bs, w_out_e, norm_ffn_e, ffn_w1, ffn_w3, ffn_w2, norm_mix_o, w_in_o, conv_w, conv_b, c_ln_g, c_ln_b, pool_w, pool_scale, w_out_o, norm_ffn_o, router_w, router_b, exp_w1, exp_w3, exp_w2, ple_w, ple_gate_w, ple_norm, final_norm):
    assert x_prompt.shape[0] == 1 and norm_mix_e.shape[0] == 1 and norm_mix_o.shape[0] == 1 and ple_w.shape[0] == 2
    n_tok, d = x_prompt.shape[1], x_prompt.shape[2]
    n_batch, n_new = x_sample.shape[0], x_sample.shape[1]
    n_s = n_batch * n_new
    a_width = a_ln_g.shape[1]
    b_width = (w_in_e.shape[2] - 2 * a_width) // 3
    n_heads = b_width // HEAD_DIM
    gdim = a_width // A_GROUPS
    n_experts = router_w.shape[2]
    page = cache_k.shape[2]
    past_len = page_table.shape[1] * page
    assert n_tok % (MOBA_BLOCK * MOBA_UNROLL) == 0 and n_tok // MOBA_BLOCK <= HEAD_DIM and past_len % MOBA_BLOCK == 0
    assert n_new <= CHUNK and CHUNK % n_new == 0 and n_s % CHUNK == 0

    bf = lambda w: w.astype(BF16)
    row2 = lambda v: v.reshape(1, -1)

    w_in = bf(w_in_e[0])
    ws_p = a_ws[0]
    bias_p = jnp.repeat(a_bs[0].T, gdim, axis=1)
    reps = CHUNK // n_new
    ws_s = jnp.tile(a_ws[0][:, :n_new, :n_new], (1, reps, reps))
    bias_s = jnp.tile(jnp.repeat(a_bs[0][:, :n_new].T, gdim, axis=1), (reps, 1))
    cos_p, sin_p = _rope_tables(jnp.arange(n_tok, dtype=I32))
    cos_s, sin_s = _rope_tables(past_len + (jnp.arange(n_s, dtype=I32) % n_new))
    hp0 = x_prompt[0]
    hs0 = x_sample.reshape(n_s, d)
    ev = dict(a_width=a_width, b_width=b_width)
    tm_p = _row_tile(n_tok, 512)
    a_p, q_p, k_p, v_p, kaug, vaug, kmean = _even_in(
        hp0, row2(norm_mix_e[0]), w_in, row2(a_ln_g[0]), row2(a_ln_b[0]), ws_p, bias_p, cos_p, sin_p,
        tm=tm_p, seq_l=CHUNK, prompt=True, **ev)
    a_s, va_s, q_s, k_s, v_s = _even_in(
        hs0, row2(norm_mix_e[0]), w_in, row2(a_ln_g[0]), row2(a_ln_b[0]), ws_s, bias_s, cos_s, sin_s,
        tm=_row_tile(n_s, 256), seq_l=n_new, prompt=False, **ev)

    n_blk = n_tok // MOBA_BLOCK
    km = kmean.reshape(n_blk, n_heads // 2, 2, HEAD_DIM)
    km_t = jnp.zeros((n_heads // 2, 2, LANES, LANES), F32)
    km_t = km_t.at[:, 0, :HEAD_DIM, HEAD_DIM:HEAD_DIM + n_blk].set(km[:, :, 0].transpose(1, 2, 0))
    km_t = km_t.at[:, 1, HEAD_DIM:, :n_blk].set(km[:, :, 1].transpose(1, 2, 0))
    km_hi = km_t.astype(BF16)
    km_lo = (km_t - km_hi.astype(F32)).astype(BF16)
    attn_p = _moba_prompt(q_p, kaug, vaug, km_hi, km_lo)
    attn_s = _moba_sample(q_s, k_s, v_s, cache_k[0], cache_v[0], page_table, n_batch)

    tail = (bf(w_out_e[0]), row2(norm_ffn_e[0]), bf(ffn_w1[0]), bf(ffn_w3[0]), bf(ffn_w2[0]))
    ple0 = (bf(ple_w[0]), bf(ple_gate_w[0]), row2(ple_norm[0]), row2(norm_mix_o[0]))
    h3_p, xn3_p = _even_tail(hp0, a_p, attn_p, *tail, p_prompt[0, 0], *ple0, tm=tm_p)
    h3_s, xn3_s = _even_tail(hs0, a_s, attn_s, *tail, p_sample[0].reshape(n_s, -1), *ple0, tm=_row_tile(n_s, 256))

    def time_major(x):
        return x.reshape(n_batch, n_new, -1).transpose(1, 0, 2).reshape(n_s, -1)

    rw = jnp.zeros((d, LANES), F32).at[:, :n_experts].set(router_w[0])
    rw_hi = rw.astype(BF16)
    rw_lo = (rw - rw_hi.astype(F32)).astype(BF16)
    rb = jnp.zeros((1, LANES), F32).at[0, :n_experts].set(router_b[0])
    common = _odd_common_args(row2(conv_b[0]), row2(c_ln_g[0]), row2(c_ln_b[0]), bf(pool_w[0]), row2(pool_scale[0]),
                              bf(w_out_o[0]), row2(norm_ffn_o[0]), rw_hi, rw_lo, rb)
    w_in1 = bf(w_in_o[0])
    zero_counts = jnp.zeros((1, LANES), F32)
    h4_p, xn5_p, meta_p, cnt_p, cstate_p, pstate_p = _odd_prompt(
        xn3_p, h3_p, w_in1, conv_w[0], common, zero_counts, tm=tm_p, n_experts=n_experts)
    h4_s, xn5_s, meta_s, cnt_all, cstate_s, pstate_s = _odd_sample(
        time_major(xn3_s), time_major(h3_s), w_in1, conv_w[0], common, cnt_p,
        state_conv[0].transpose(1, 0, 2), state_pool[0].transpose(1, 0, 2), n_batch=n_batch, n_experts=n_experts)

    counts = cnt_all[0, :n_experts].astype(I32)
    padded = (counts + MOE_TM - 1) // MOE_TM * MOE_TM
    pends = jnp.cumsum(padded)
    pstarts = (pends - padded).astype(F32)
    n_rows_all = (n_tok + n_s) * TOP_K
    n_blocks = -(-n_rows_all // MOE_TM) + n_experts
    blk_row0 = jnp.arange(n_blocks, dtype=I32) * MOE_TM
    blk_e = jnp.minimum(jnp.sum((pends[None, :] <= blk_row0[:, None]).astype(I32), axis=1), n_experts - 1)
    n_used = (pends[-1] // MOE_TM).astype(I32).reshape(1)
    last_e = blk_e[jnp.maximum(n_used[0] - 1, 0)]
    blk_e = jnp.where(jnp.arange(n_blocks) < n_used[0], blk_e, last_e)

    def dests(meta):
        e = meta[:, :TOP_K].astype(I32)
        return (pstarts[e] + meta[:, 4:4 + TOP_K]).astype(I32).reshape(-1)

    dest_p, dest_s = dests(meta_p), dests(meta_s)
    xbuf = jnp.zeros((n_blocks * MOE_TM, d // LANES, LANES), F32)
    xbuf = _moe_scatter(dest_p, xn5_p, xbuf, tm=_row_tile(n_tok, 512))
    xbuf = _moe_scatter(dest_s, xn5_s, xbuf, tm=_row_tile(n_s, 256))
    ff_e = exp_w1.shape[3]
    tf = next((c for c in (1792, 1024, 512, 256) if ff_e % c == 0), ff_e)
    ybuf = _moe_experts(blk_e, n_used, xbuf, bf(exp_w1[0]), bf(exp_w3[0]), bf(exp_w2[0]), tm=MOE_TM, tf=tf)
    ple1 = (bf(ple_w[1]), bf(ple_gate_w[1]), row2(ple_norm[1]), row2(final_norm))
    y_p = _moe_combine(dest_p, ybuf, h4_p, meta_p, p_prompt[1, 0], *ple1, tm=_row_tile(n_tok, 256))
    y_s = _moe_combine(dest_s, ybuf, h4_s, meta_s, time_major(p_sample[1].reshape(n_s, -1)), *ple1,
                       tm=_row_tile(n_s, 256))

    def batch_major(x):
        return x.reshape(n_new, n_batch, -1).transpose(1, 0, 2)

    kv = lambda x, b, l: x.reshape(1, b, l, n_heads, HEAD_DIM)
    return (y_p[None], batch_major(y_s),
            kv(k_p, 1, n_tok), kv(v_p, 1, n_tok), kv(k_s, n_batch, n_new), kv(v_s, n_batch, n_new),
            va_s.reshape(1, n_batch, n_new, a_width),
            cstate_p[None, None], cstate_s.transpose(1, 0, 2)[None],
            pstate_p[None, None], pstate_s.transpose(1, 0, 2)[None])
```

```python
import functools
import math

import jax
import jax.numpy as jnp
from jax import lax
from jax.experimental import pallas as pl
from jax.experimental.pallas import tpu as pltpu

F32 = jnp.float32
BF16 = jnp.bfloat16
I32 = jnp.int32

EPS = 1e-6
NEG = -1e30
LOG2E = 1.4426950408889634
CHUNK = 128
A_GROUPS = 8
HEAD_DIM = 64
MOBA_BLOCK = 256
MOBA_TOPK = 3
ROPE_THETA = 10000.0
POOL_WINDOWS = (2, 4, 8, 16)
TOP_K = 2

LANES = 128
MXU_WIDTH = 256
VMEM_LIMIT_BYTES = 56 * 1024 * 1024
HALO = 32
MOE_TM = 512
MOBA_UNROLL = 4
MOBA_SHARE = 2
KV_PAGES_PER_STEP = 32


def _cparams(sem):
    return pltpu.CompilerParams(dimension_semantics=sem, vmem_limit_bytes=VMEM_LIMIT_BYTES)


def _full(shape):
    n = len(shape)
    return pl.BlockSpec(shape, lambda *a, _n=n: (0,) * _n, pipeline_mode=pl.Buffered(1))


def _whole(shape):
    n = len(shape)
    return pl.BlockSpec(shape, lambda *a, _n=n: (0,) * _n)


def _dot(a, b):
    return jnp.dot(a, b, preferred_element_type=F32)


def _dot_nt(a, b):
    return lax.dot_general(a, b, (((1,), (1,)), ((), ())), preferred_element_type=F32)


def _rms(x, g):
    return x * lax.rsqrt(jnp.mean(x * x, axis=-1, keepdims=True) + EPS) * g


def _layernorm(x, g, b):
    xc = x - jnp.mean(x, axis=-1, keepdims=True)
    var = jnp.mean(xc * xc, axis=-1, keepdims=True)
    return xc * lax.rsqrt(var + EPS) * g + b


def _split_bf16(x):
    hi = x.astype(BF16)
    lo = (x - hi.astype(F32)).astype(BF16)
    return hi, lo


def _lane_pick(x, lane, idx):
    return jnp.sum(jnp.where(lane == idx, x, 0.0), axis=-1, keepdims=True)


def _even_in_kernel(h_ref, g_ref, w_ref, lng_ref, lnb_ref, ws_ref, bias_ref, cos_ref, sin_ref, *outs,
                    seq_l, prompt, a_width, b_width):
    tm = h_ref.shape[0]
    i = pl.program_id(0)
    xn = _rms(h_ref[...], g_ref[...]).astype(BF16)
    aw, bw = a_width, b_width

    u = jax.nn.gelu(_dot(xn, w_ref[:, 0:aw]))
    va = _layernorm(jax.nn.gelu(_dot(xn, w_ref[:, aw:2 * aw])), lng_ref[...], lnb_ref[...])
    vab = va.astype(BF16)

    r = lax.broadcasted_iota(I32, (CHUNK, CHUNK), 0)
    c = lax.broadcasted_iota(I32, (CHUNK, CHUNK), 1)
    sh = int(math.log2(seq_l))
    ok = (lax.shift_right_logical(r, sh) == lax.shift_right_logical(c, sh)) & (c <= r)
    wm = [jnp.where(ok, ws_ref[g], 0.0).astype(BF16) for g in range(A_GROUPS)]
    lane = lax.broadcasted_iota(I32, (CHUNK, LANES), 1)
    gdim = aw // A_GROUPS
    gates = []
    for ci in range(tm // CHUNK):
        vc = vab[ci * CHUNK:(ci + 1) * CHUNK]
        parts = []
        for p in range(aw // LANES):
            vp = vc[:, p * LANES:(p + 1) * LANES]
            r0 = _dot(wm[2 * p], vp)
            r1 = _dot(wm[2 * p + 1], vp)
            parts.append(jnp.where(lane < gdim, r0, r1))
        gates.append(jnp.concatenate(parts, axis=1) + bias_ref[...])
    gate = jnp.concatenate(gates, axis=0)
    a_out = u * gate

    cosv = cos_ref[...]
    sinv = sin_ref[...]
    lane_t = lax.broadcasted_iota(I32, (tm, LANES), 1)
    first_half = (lane_t & (HEAD_DIM - 1)) < HEAD_DIM // 2

    def rope(z):
        parts = []
        for p in range(bw // LANES):
            xs = z[:, p * LANES:(p + 1) * LANES]
            rot = jnp.where(first_half, pltpu.roll(xs, LANES - HEAD_DIM // 2, 1), pltpu.roll(xs, HEAD_DIM // 2, 1))
            parts.append(xs * cosv + rot * sinv)
        return parts

    q_parts = rope(_dot(xn, w_ref[:, 2 * aw:2 * aw + bw]))
    k_parts = rope(_dot(xn, w_ref[:, 2 * aw + bw:2 * aw + 2 * bw]))
    v = _dot(xn, w_ref[:, 2 * aw + 2 * bw:2 * aw + 3 * bw])
    k = jnp.concatenate(k_parts, axis=1)
    q = jnp.concatenate(q_parts, axis=1) * (HEAD_DIM ** -0.5)

    if prompt:
        a_ref, q_ref, k_ref, v_ref, kaug_ref, vaug_ref, kmean_ref = outs
        a_ref[...] = a_out.astype(BF16)
        q_ref[...] = (q * LOG2E).astype(BF16)
        k_ref[...] = k
        v_ref[...] = v
        row = lax.broadcasted_iota(I32, (tm, LANES), 0)
        blk = i * (tm // MOBA_BLOCK) + lax.shift_right_logical(row, int(math.log2(MOBA_BLOCK)))
        lo = lane_t < HEAD_DIM
        hot_hi = jnp.where(blk == lane_t - HEAD_DIM, 1.0, 0.0)
        hot_lo = jnp.where(blk == lane_t, 1.0, 0.0)
        for p in range(bw // LANES):
            kp = k_parts[p]
            vp = v[:, p * LANES:(p + 1) * LANES]
            kaug_ref[:, (2 * p) * LANES:(2 * p + 1) * LANES] = jnp.where(lo, kp, hot_hi).astype(BF16)
            kaug_ref[:, (2 * p + 1) * LANES:(2 * p + 2) * LANES] = jnp.where(lo, hot_lo, kp).astype(BF16)
            vaug_ref[:, (2 * p) * LANES:(2 * p + 1) * LANES] = jnp.where(lo, vp, 1.0).astype(BF16)
            vaug_ref[:, (2 * p + 1) * LANES:(2 * p + 2) * LANES] = jnp.where(lo, 1.0, vp).astype(BF16)
        for b in range(tm // MOBA_BLOCK):
            kmean_ref[b] = jnp.mean(k[b * MOBA_BLOCK:(b + 1) * MOBA_BLOCK], axis=0, keepdims=True)
    else:
        a_ref, va_ref, q_ref, k_ref, v_ref = outs
        a_ref[...] = a_out.astype(BF16)
        va_ref[...] = va
        q_ref[...] = q
        k_ref[...] = k
        v_ref[...] = v


def _even_in(h, norm_g, w_in, ln_g, ln_b, ws, bias, cos_t, sin_t, *, tm, seq_l, prompt, a_width, b_width):
    t, d = h.shape
    nw = w_in.shape[1]
    row = lambda i: (i, 0)
    in_specs = [pl.BlockSpec((tm, d), row), _full((1, d)), _full((d, nw)), _full((1, a_width)), _full((1, a_width)),
                _full(ws.shape), _full(bias.shape), pl.BlockSpec((tm, LANES), row), pl.BlockSpec((tm, LANES), row)]
    if prompt:
        nb = t // MOBA_BLOCK
        out_shape = [jax.ShapeDtypeStruct((t, a_width), BF16), jax.ShapeDtypeStruct((t, b_width), BF16),
                     jax.ShapeDtypeStruct((t, b_width), F32), jax.ShapeDtypeStruct((t, b_width), F32),
                     jax.ShapeDtypeStruct((t, 2 * b_width), BF16), jax.ShapeDtypeStruct((t, 2 * b_width), BF16),
                     jax.ShapeDtypeStruct((nb, 1, b_width), F32)]
        out_specs = [pl.BlockSpec((tm, a_width), row), pl.BlockSpec((tm, b_width), row),
                     pl.BlockSpec((tm, b_width), row), pl.BlockSpec((tm, b_width), row),
                     pl.BlockSpec((tm, 2 * b_width), row), pl.BlockSpec((tm, 2 * b_width), row),
                     pl.BlockSpec((tm // MOBA_BLOCK, 1, b_width), lambda i: (i, 0, 0))]
    else:
        out_shape = [jax.ShapeDtypeStruct((t, a_width), BF16), jax.ShapeDtypeStruct((t, a_width), F32),
                     jax.ShapeDtypeStruct((t, b_width), F32), jax.ShapeDtypeStruct((t, b_width), F32),
                     jax.ShapeDtypeStruct((t, b_width), F32)]
        out_specs = [pl.BlockSpec((tm, a_width), row), pl.BlockSpec((tm, a_width), row),
                     pl.BlockSpec((tm, b_width), row), pl.BlockSpec((tm, b_width), row),
                     pl.BlockSpec((tm, b_width), row)]
    return pl.pallas_call(
        functools.partial(_even_in_kernel, seq_l=seq_l, prompt=prompt, a_width=a_width, b_width=b_width),
        grid=(t // tm,), in_specs=in_specs, out_specs=out_specs, out_shape=out_shape,
        compiler_params=_cparams(("arbitrary",)), name="even_in_prompt" if prompt else "even_in_sample",
    )(h, norm_g, w_in, ln_g, ln_b, ws, bias, cos_t, sin_t)


def _moba_prompt_kernel(q_ref, k0_ref, k1_ref, v0_ref, v1_ref, kmh_ref, kml_ref, o_ref, p_scr, m_scr):
    tq = q_ref.shape[0]
    i = pl.program_id(1)
    q = q_ref[...]
    qf = q.astype(F32)
    lane = lax.broadcasted_iota(I32, (tq, LANES), 1)
    k_refs = (k0_ref, k1_ref)
    v_refs = (v0_ref, v1_ref)
    neg_inf = jnp.float32(-jnp.inf)

    qa, q_own = [], []
    for e in range(2):
        in_blk = (lane >= HEAD_DIM) if e == 0 else (lane < HEAD_DIM)
        blk = lane - HEAD_DIM if e == 0 else lane
        sc = _dot(q, kmh_ref[0, e]) + _dot(q, kml_ref[0, e])
        cand = jnp.where(in_blk & (blk < i), sc, neg_inf)
        sel = jnp.zeros((tq, LANES), jnp.bool_)
        for _ in range(MOBA_TOPK):
            mx = jnp.max(cand, axis=-1, keepdims=True)
            first = jnp.min(jnp.where(cand == mx, blk, jnp.int32(2 ** 30)), axis=-1, keepdims=True)
            pick = (blk == first) & in_blk & (mx > neg_inf)
            sel = sel | pick
            cand = jnp.where(pick, neg_inf, cand)
        qa.append(jnp.where(in_blk, jnp.where(sel, 0.0, NEG), qf).astype(BF16))
        q_own.append(jnp.where(in_blk, 0.0, qf).astype(BF16))

    n_blk = k0_ref.shape[0] // MOBA_BLOCK

    def offset(blk_idx):
        return pl.multiple_of(jnp.minimum(blk_idx, n_blk - 1) * MOBA_BLOCK, MOBA_BLOCK)

    def probs(e, qe, blk_idx, mask=None):
        s = _dot_nt(qe, k_refs[e][pl.ds(offset(blk_idx), MOBA_BLOCK), :])
        if mask is not None:
            s = jnp.where(mask, s, NEG)
        m = jnp.max(s, axis=-1, keepdims=True)
        return m, jnp.exp2((s - m).astype(BF16))

    def values(e, p, blk_idx):
        return _dot(p, v_refs[e][pl.ds(offset(blk_idx), MOBA_BLOCK), :])

    def stash(g):
        slot = g & 1
        for e in range(2):
            for u0 in range(0, MOBA_UNROLL, MOBA_SHARE):
                s_us = [_dot_nt(qa[e], k_refs[e][pl.ds(offset(g * MOBA_UNROLL + u0 + x), MOBA_BLOCK), :])
                        for x in range(MOBA_SHARE)]
                top = s_us[0]
                for s_u in s_us[1:]:
                    top = jnp.maximum(top, s_u)
                m_u = jnp.max(top, axis=-1, keepdims=True)
                for x, s_u in enumerate(s_us):
                    p_scr[slot, e * MOBA_UNROLL + u0 + x] = jnp.exp2((s_u - m_u).astype(BF16))
                m_scr[slot, e * MOBA_UNROLL + u0] = m_u

    def merge(g, ms, accs):
        slot = g & 1
        new_m, new_acc = [], []
        for e in range(2):
            m_us = [m_scr[slot, e * MOBA_UNROLL + u0] for u0 in range(0, MOBA_UNROLL, MOBA_SHARE)]
            m_new = ms[e]
            for m_u in m_us:
                m_new = jnp.maximum(m_new, m_u)
            acc = jnp.exp2(ms[e] - m_new) * accs[e]
            for j, m_u in enumerate(m_us):
                u0 = j * MOBA_SHARE
                o_u = values(e, p_scr[slot, e * MOBA_UNROLL + u0], g * MOBA_UNROLL + u0)
                for x in range(1, MOBA_SHARE):
                    o_u = o_u + values(e, p_scr[slot, e * MOBA_UNROLL + u0 + x], g * MOBA_UNROLL + u0 + x)
                acc = acc + jnp.exp2(m_u - m_new) * o_u
            new_m.append(m_new)
            new_acc.append(acc)
        return tuple(new_m), tuple(new_acc)

    r = lax.broadcasted_iota(I32, (tq, MOBA_BLOCK), 0)
    c = lax.broadcasted_iota(I32, (tq, MOBA_BLOCK), 1)
    own = [probs(e, q_own[e], i, c <= r) for e in range(2)]
    ms = tuple(m for m, _ in own)
    accs = tuple(values(e, own[e][1], i) for e in range(2))

    def body(g, carry):
        ms, accs = merge(g - 1, *carry)
        stash(g)
        return ms, accs

    n_groups = lax.shift_right_logical(i + (MOBA_UNROLL - 1), int(math.log2(MOBA_UNROLL)))
    stash(0)
    ms, accs = lax.fori_loop(1, n_groups, body, (ms, accs))
    _, accs = merge(jnp.maximum(n_groups - 1, 0), ms, accs)
    outs = [a / pltpu.roll(a, HEAD_DIM, 1) for a in accs]
    o_ref[...] = jnp.where(lane < HEAD_DIM, outs[0], outs[1]).astype(o_ref.dtype)


def _moba_prompt(q, kaug, vaug, km_hi, km_lo):
    t, bw = q.shape
    n_pairs = bw // LANES
    tq = MOBA_BLOCK
    col = lambda e: (lambda p, i, _e=e: (0, 2 * p + _e))
    in_specs = [pl.BlockSpec((tq, LANES), lambda p, i: (i, p)),
                pl.BlockSpec((t, LANES), col(0)), pl.BlockSpec((t, LANES), col(1)),
                pl.BlockSpec((t, LANES), col(0)), pl.BlockSpec((t, LANES), col(1)),
                pl.BlockSpec((1, 2, LANES, LANES), lambda p, i: (p, 0, 0, 0)),
                pl.BlockSpec((1, 2, LANES, LANES), lambda p, i: (p, 0, 0, 0))]
    return pl.pallas_call(
        _moba_prompt_kernel, grid=(n_pairs, t // tq), in_specs=in_specs,
        out_specs=pl.BlockSpec((tq, LANES), lambda p, i: (i, p)),
        out_shape=jax.ShapeDtypeStruct((t, bw), BF16),
        scratch_shapes=[pltpu.VMEM((2, 2 * MOBA_UNROLL, tq, MOBA_BLOCK), BF16),
                        pltpu.VMEM((2, 2 * MOBA_UNROLL, tq, 1), F32)],
        compiler_params=_cparams(("arbitrary", "arbitrary")), name="moba_prompt",
    )(q, kaug, kaug, vaug, vaug, km_hi, km_lo)


def _moba_sample_keys_kernel(pt_ref, *refs, n_steps):
    npg = KV_PAGES_PER_STEP
    pages = refs[:npg]
    qbd_ref, knew_ref, p_ref, pown_ref, linv_ref, s_scr, kmean_scr, bmax_scr = refs[npg:]
    c = pl.program_id(1)
    hw, page = pages[0].shape[1], pages[0].shape[2]
    ppb = MOBA_BLOCK // page
    bps = npg // ppb
    qbd = qbd_ref[0]
    n_rows = qbd.shape[0]
    n_blocks = n_steps * bps
    lane = lax.broadcasted_iota(I32, (n_rows, LANES), 1)
    neg_inf = jnp.float32(-jnp.inf)

    @pl.when(c == 0)
    def _():
        bmax_scr[...] = jnp.full((n_rows, LANES), neg_inf, F32)
        kmean_scr[...] = jnp.zeros_like(kmean_scr)

    lane_k = lax.broadcasted_iota(I32, (hw, LANES), 1)
    for j in range(bps):
        kb = jnp.concatenate([pages[j * ppb + x][0] for x in range(ppb)], axis=1)
        blk = c * bps + j
        kmean_scr[...] = jnp.where(lane_k == blk, jnp.mean(kb, axis=1, keepdims=True), kmean_scr[...])
        s = _dot(qbd, kb.astype(BF16))
        s_scr[:, pl.ds(pl.multiple_of(blk * MOBA_BLOCK, MOBA_BLOCK), MOBA_BLOCK)] = s
        bmax_scr[...] = jnp.where(lane == blk, jnp.max(s, axis=-1, keepdims=True), bmax_scr[...])

    @pl.when(c == n_steps - 1)
    def _():
        kmh, kml = _split_bf16(kmean_scr[...])
        sc = _dot(qbd, kmh) + _dot(qbd, kml)
        cand = jnp.where(lane < n_blocks, sc, neg_inf)
        sel = jnp.zeros((n_rows, LANES), jnp.bool_)
        for _ in range(min(MOBA_TOPK, n_blocks)):
            mx = jnp.max(cand, axis=-1, keepdims=True)
            first = jnp.min(jnp.where(cand == mx, lane, jnp.int32(2 ** 30)), axis=-1, keepdims=True)
            pick = lane == first
            sel = sel | pick
            cand = jnp.where(pick, neg_inf, cand)
        so = _dot_nt(qbd, knew_ref[0].astype(BF16))
        n_new = so.shape[1]
        rr = lax.broadcasted_iota(I32, (n_rows, n_new), 0)
        ss = lax.broadcasted_iota(I32, (n_rows, n_new), 1)
        so = jnp.where(ss <= (rr & (n_new - 1)), so, NEG)
        m = jnp.maximum(jnp.max(jnp.where(sel, bmax_scr[...], neg_inf), axis=-1, keepdims=True),
                        jnp.max(so, axis=-1, keepdims=True))
        po = jnp.exp(so - m)
        l = jnp.sum(po, axis=-1, keepdims=True)
        self32 = sel.astype(F32)
        for b in range(n_blocks):
            on = jnp.sum(jnp.where(lane == b, self32, 0.0), axis=-1, keepdims=True)
            sb = s_scr[:, b * MOBA_BLOCK:(b + 1) * MOBA_BLOCK]
            pb = (jnp.exp(sb - m) * on).astype(BF16)
            l = l + jnp.sum(pb.astype(F32), axis=-1, keepdims=True)
            p_ref[0, :, b * MOBA_BLOCK:(b + 1) * MOBA_BLOCK] = pb
        pown_ref[0] = jnp.concatenate([po, jnp.zeros((n_rows, LANES - n_new), F32)], axis=1)
        linv_ref[0] = jnp.broadcast_to(1.0 / l, (n_rows, LANES))


def _moba_sample_values_kernel(pt_ref, *refs, n_steps, n_heads):
    npg = KV_PAGES_PER_STEP
    pages = refs[:npg]
    p_ref, pown_ref, linv_ref, vnew_ref, o_ref, acc_scr = refs[npg:]
    c = pl.program_id(1)
    vb = jnp.concatenate([pg[0] for pg in pages], axis=1).astype(BF16)
    part = _dot_nt(p_ref[0], vb)

    @pl.when(c == 0)
    def _():
        acc_scr[...] = part

    @pl.when(c > 0)
    def _():
        acc_scr[...] += part

    @pl.when(c == n_steps - 1)
    def _():
        vnew = vnew_ref[0]
        n_new = vnew.shape[0]
        o = (acc_scr[...] + _dot(pown_ref[0][:, :n_new].astype(BF16), vnew.astype(BF16))) * linv_ref[0][:, :1]
        n_rows, hw = o.shape
        rr = lax.broadcasted_iota(I32, (n_rows, hw), 0)
        ll = lax.broadcasted_iota(I32, (n_rows, hw), 1)
        keep = (lax.shift_right_logical(rr, int(math.log2(n_new)))
                == lax.shift_right_logical(ll, int(math.log2(HEAD_DIM))))
        o = jnp.where(keep, o, 0.0).reshape(n_heads, n_new, hw)
        o_ref[0] = jnp.sum(o, axis=0).astype(o_ref.dtype)


def _moba_sample(q, k, v, cache_k, cache_v, page_table, n_batch):
    hw = q.shape[1]
    n_heads = hw // HEAD_DIM
    n_new = q.shape[0] // n_batch
    n_phys, page = cache_k.shape[0], cache_k.shape[1]
    n_pages = page_table.shape[1]
    npg = KV_PAGES_PER_STEP
    assert n_pages % npg == 0 and (n_pages * page) % MOBA_BLOCK == 0 and MOBA_BLOCK % page == 0
    n_steps = n_pages // npg
    past = n_pages * page
    n_blocks = past // MOBA_BLOCK
    assert n_blocks <= LANES and n_new & (n_new - 1) == 0
    n_rows = n_heads * n_new
    ck = cache_k.transpose(0, 2, 3, 1).reshape(n_phys, hw, page)
    cv = cache_v.transpose(0, 2, 3, 1).reshape(n_phys, hw, page)
    pt = page_table.reshape(-1).astype(I32)
    q3 = q.reshape(n_batch, n_new, n_heads, HEAD_DIM)
    eye = jnp.eye(n_heads, dtype=F32)
    qbd = (q3.transpose(0, 2, 1, 3)[:, :, :, None, :] * eye[None, :, None, :, None]).reshape(n_batch, n_rows, hw)
    qbd = qbd.astype(BF16)
    k3 = k.reshape(n_batch, n_new, hw)
    v3 = v.reshape(n_batch, n_new, hw)

    def page_spec(s):
        return pl.BlockSpec((1, hw, page), lambda b, c, pt_ref, _s=s: (pt_ref[b * n_pages + c * npg + _s], 0, 0))

    per_b = lambda shape: pl.BlockSpec(shape, lambda b, c, pt_ref: (b,) + (0,) * (len(shape) - 1))
    p, pown, linv = pl.pallas_call(
        functools.partial(_moba_sample_keys_kernel, n_steps=n_steps),
        grid_spec=pltpu.PrefetchScalarGridSpec(
            num_scalar_prefetch=1, grid=(n_batch, n_steps),
            in_specs=[page_spec(s) for s in range(npg)] + [per_b((1, n_rows, hw)), per_b((1, n_new, hw))],
            out_specs=[per_b((1, n_rows, past)), per_b((1, n_rows, LANES)), per_b((1, n_rows, LANES))],
            scratch_shapes=[pltpu.VMEM((n_rows, past), F32), pltpu.VMEM((hw, LANES), F32),
                            pltpu.VMEM((n_rows, LANES), F32)]),
        out_shape=[jax.ShapeDtypeStruct((n_batch, n_rows, past), BF16),
                   jax.ShapeDtypeStruct((n_batch, n_rows, LANES), F32),
                   jax.ShapeDtypeStruct((n_batch, n_rows, LANES), F32)],
        compiler_params=_cparams(("arbitrary", "arbitrary")), name="moba_sample_keys",
    )(pt, *([ck] * npg), qbd, k3)

    out = pl.pallas_call(
        functools.partial(_moba_sample_values_kernel, n_steps=n_steps, n_heads=n_heads),
        grid_spec=pltpu.PrefetchScalarGridSpec(
            num_scalar_prefetch=1, grid=(n_batch, n_steps),
            in_specs=[page_spec(s) for s in range(npg)]
            + [pl.BlockSpec((1, n_rows, npg * page), lambda b, c, pt_ref: (b, 0, c)),
               per_b((1, n_rows, LANES)), per_b((1, n_rows, LANES)), per_b((1, n_new, hw))],
            out_specs=per_b((1, n_new, hw)),
            scratch_shapes=[pltpu.VMEM((n_rows, hw), F32)]),
        out_shape=jax.ShapeDtypeStruct((n_batch, n_new, hw), BF16),
        compiler_params=_cparams(("arbitrary", "arbitrary")), name="moba_sample_values",
    )(pt, *([cv] * npg), p, pown, linv, v3)
    return out.reshape(n_batch * n_new, hw)


def _ple(h, p_ref, wp_ref, wg_ref, pn_ref):
    gate = jax.nn.sigmoid(_dot(_rms(h, pn_ref[...]).astype(BF16), wg_ref[...]))
    return h + _dot(p_ref[...].astype(BF16), wp_ref[...]) * gate


def _even_tail_kernel(h_ref, a_ref, b_ref, wo_ref, nf_ref, w1_ref, w3_ref, w2_ref, p_ref, wp_ref, wg_ref, pn_ref,
                      nn_ref, h_out_ref, xn_out_ref, *, ff_chunks):
    aw = a_ref.shape[1]
    h1 = h_ref[...] + _dot(a_ref[...], wo_ref[0:aw, :]) + _dot(b_ref[...], wo_ref[aw:, :])
    xn = _rms(h1, nf_ref[...]).astype(BF16)
    ff = w1_ref.shape[1]
    cw = ff // ff_chunks
    hid = []
    for ci in range(ff_chunks):
        sl = slice(ci * cw, (ci + 1) * cw)
        hid.append((jax.nn.silu(_dot(xn, w1_ref[:, sl])) * _dot(xn, w3_ref[:, sl])).astype(BF16))
    acc = h1 + _dot(jnp.concatenate(hid, axis=1), w2_ref[...])
    h3 = _ple(acc, p_ref, wp_ref, wg_ref, pn_ref)
    h_out_ref[...] = h3
    xn_out_ref[...] = _rms(h3, nn_ref[...]).astype(BF16)


def _even_tail(h, a, b, w_out, norm_ffn, w1, w3, w2, p, wp, wg, pn, next_norm, *, tm):
    t, d = h.shape
    ff = w1.shape[1]
    ff_chunks = max((n for n in range(1, ff // MXU_WIDTH + 1) if ff % (n * MXU_WIDTH) == 0 and ff // n >= 1024),
                    default=1)
    row = lambda i: (i, 0)
    in_specs = [pl.BlockSpec((tm, d), row), pl.BlockSpec((tm, a.shape[1]), row), pl.BlockSpec((tm, b.shape[1]), row),
                _full(w_out.shape), _full((1, d)), _full(w1.shape), _full(w3.shape), _full(w2.shape),
                pl.BlockSpec((tm, p.shape[1]), row), _full(wp.shape), _full(wg.shape), _full((1, d)), _full((1, d))]
    return pl.pallas_call(
        functools.partial(_even_tail_kernel, ff_chunks=ff_chunks),
        grid=(t // tm,), in_specs=in_specs,
        out_specs=[pl.BlockSpec((tm, d), row), pl.BlockSpec((tm, d), row)],
        out_shape=[jax.ShapeDtypeStruct((t, d), F32), jax.ShapeDtypeStruct((t, d), BF16)],
        compiler_params=_cparams(("arbitrary",)), name="even_tail",
    )(h, a, b, w_out, norm_ffn, w1, w3, w2, p, wp, wg, pn, next_norm)


def _odd_tail(conv, d_in, h3, refs, count_scr, n_experts):
    (cb_ref, clg_ref, clb_ref, pw_ref, ps_ref, wo_ref, nf_ref, rwh_ref, rwl_ref, rb_ref) = refs
    rows = conv.shape[0]
    cw = conv.shape[1]
    cact = jax.nn.silu(_layernorm(conv + cb_ref[...], clg_ref[...], clb_ref[...]))
    db = d_in.astype(BF16)
    gdim = d_in.shape[1] // len(POOL_WINDOWS)
    dparts = [_dot(db[:, g * gdim:(g + 1) * gdim], pw_ref[g]) for g in range(len(POOL_WINDOWS))]
    dmix = jnp.concatenate(dparts, axis=1) * ps_ref[...]
    h4 = h3 + _dot(cact.astype(BF16), wo_ref[0:cw, :]) + _dot(dmix.astype(BF16), wo_ref[cw:, :])
    xn = _rms(h4, nf_ref[...])
    xh, xl = _split_bf16(xn)
    logits = _dot(xh, rwh_ref[...]) + _dot(xl, rwh_ref[...]) + _dot(xh, rwl_ref[...]) + rb_ref[...]
    lane = lax.broadcasted_iota(I32, (rows, LANES), 1)
    neg_inf = jnp.float32(-jnp.inf)
    cand = jnp.where(lane < n_experts, logits, neg_inf)
    v1 = jnp.max(cand, axis=-1, keepdims=True)
    e1 = jnp.min(jnp.where(cand == v1, lane, jnp.int32(2 ** 30)), axis=-1, keepdims=True)
    cand2 = jnp.where(lane == e1, neg_inf, cand)
    v2 = jnp.max(cand2, axis=-1, keepdims=True)
    e2 = jnp.min(jnp.where(cand2 == v2, lane, jnp.int32(2 ** 30)), axis=-1, keepdims=True)
    ex = jnp.exp(v2 - v1)
    g1 = 1.0 / (1.0 + ex)
    g2 = ex / (1.0 + ex)
    chosen = jnp.where((lane == e1) | (lane == e2), 1.0, 0.0).astype(BF16)
    rr = lax.broadcasted_iota(I32, (rows, rows), 0)
    cc = lax.broadcasted_iota(I32, (rows, rows), 1)
    before = jnp.where(cc < rr, 1.0, 0.0).astype(BF16)
    prefix = _dot(before, chosen) + count_scr[...]
    r1 = _lane_pick(prefix, lane, e1)
    r2 = _lane_pick(prefix, lane, e2)
    count_scr[...] = count_scr[...] + jnp.sum(chosen.astype(F32), axis=0, keepdims=True)
    meta = jnp.zeros((rows, LANES), F32)
    for idx, val in enumerate((e1.astype(F32), e2.astype(F32), g1, g2, r1, r2)):
        meta = jnp.where(lane == idx, val, meta)
    return h4, xn, meta


def _odd_prompt_kernel(xm_ref, xh_ref, h3_ref, wi_ref, cw_ref, *refs, n_experts, conv_hist, pool_hist):
    tail_refs = refs[:10]
    cnt_in_ref, h4_ref, xn_ref, meta_ref, cnt_ref, cstate_ref, pstate_ref, count_scr = refs[10:]
    i = pl.program_id(0)
    tm = xm_ref.shape[0]
    cwid = cw_ref.shape[1]

    @pl.when(i == 0)
    def _():
        count_scr[...] = cnt_in_ref[...]

    x = jnp.concatenate([xh_ref[...], xm_ref[...]], axis=0)
    z = _dot(x, wi_ref[...])
    row = lax.broadcasted_iota(I32, (HALO + tm, 1), 0)
    z = jnp.where((row >= HALO) | (i > 0), z, 0.0)
    c_in = z[:, :cwid] * jax.nn.sigmoid(z[:, cwid:2 * cwid])
    dx = z[:, 2 * cwid:]

    cwv = cw_ref[...]
    width = cwv.shape[0]
    lead = HALO - (width - 1)

    def tap(kp):
        k = kp - lead
        return cwv[k:k + 1, :] if 0 <= k < width else None

    ext = c_in
    conv = None
    for b in range(8):
        n_a = (HALO // 8 + 1) if b == 0 else HALO // 8
        rows_b = tm if b == 0 else tm + 8
        ub = None
        for a in range(n_a):
            w = tap(8 * a + b)
            if w is None:
                continue
            term = ext[8 * a:8 * a + rows_b] * w
            ub = term if ub is None else ub + term
        if ub is None:
            continue
        piece = ub if b == 0 else pltpu.roll(ub, rows_b - b, 0)[:tm]
        conv = piece if conv is None else conv + piece

    s = dx
    sums = {}
    span = 1
    while span < max(POOL_WINDOWS):
        s = s + pltpu.roll(s, span, 0)
        span *= 2
        sums[span] = s
    gdim = dx.shape[1] // len(POOL_WINDOWS)
    pos = i * tm + lax.broadcasted_iota(I32, (tm, 1), 0)
    pooled = []
    for g, w in enumerate(POOL_WINDOWS):
        win = sums[w][HALO:, g * gdim:(g + 1) * gdim]
        cnt = jnp.minimum(pos + 1, w).astype(F32)
        pooled.append(win / cnt)
    dmain = dx[HALO:]
    d_in = jnp.concatenate(pooled, axis=1) - dmain

    h4, xn, meta = _odd_tail(conv, d_in, h3_ref[...], tail_refs, count_scr, n_experts)
    h4_ref[...] = h4
    _to_row_tiles(xn_ref, xn)
    meta_ref[...] = meta
    cnt_ref[...] = count_scr[...]
    ctail = c_in[tm:]
    cstate_ref[...] = pltpu.roll(ctail, conv_hist, 0)[:conv_hist] if conv_hist < HALO else ctail
    ptail_rows = 8 * (-(-pool_hist // 8))
    ptail = dmain[tm - ptail_rows:]
    pstate_ref[...] = pltpu.roll(ptail, pool_hist, 0)[:pool_hist] if pool_hist < ptail_rows else ptail


def _odd_sample_kernel(x_ref, h3_ref, wi_ref, cw_ref, *refs, n_experts, n_batch):
    tail_refs = refs[:10]
    cnt_in_ref, chist_ref, phist_ref, h4_ref, xn_ref, meta_ref, cnt_ref, cstate_ref, pstate_ref, count_scr = refs[10:]
    rows = x_ref.shape[0]
    n_new = rows // n_batch
    cwid = cw_ref.shape[1]
    count_scr[...] = cnt_in_ref[...]
    z = _dot(x_ref[...], wi_ref[...])
    c_in = z[:, :cwid] * jax.nn.sigmoid(z[:, cwid:2 * cwid])
    dx = z[:, 2 * cwid:]
    cwv = cw_ref[...]
    width = cwv.shape[0]
    hist_c = chist_ref[...]
    xp = [hist_c[j] for j in range(width - 1)] + [c_in[t * n_batch:(t + 1) * n_batch] for t in range(n_new)]
    conv = []
    for t in range(n_new):
        acc = xp[t] * cwv[0:1, :]
        for k in range(1, width):
            acc = acc + xp[t + k] * cwv[k:k + 1, :]
        conv.append(acc)
    conv = jnp.concatenate(conv, axis=0)
    cstate_ref[...] = jnp.stack(xp[n_new:], axis=0)

    hist_p = phist_ref[...]
    n_ph = hist_p.shape[0]
    xq = [hist_p[j] for j in range(n_ph)] + [dx[t * n_batch:(t + 1) * n_batch] for t in range(n_new)]
    gdim = dx.shape[1] // len(POOL_WINDOWS)
    lane = lax.broadcasted_iota(I32, (n_batch, dx.shape[1]), 1)
    d_in = []
    for t in range(n_new):
        run = xq[n_ph + t]
        tot = jnp.zeros_like(run)
        done = 1
        for g, w in enumerate(POOL_WINDOWS):
            for j in range(done, w):
                run = run + xq[n_ph + t - j]
            done = w
            in_g = (lane >= g * gdim) & (lane < (g + 1) * gdim)
            tot = jnp.where(in_g, run / float(w), tot)
        d_in.append(tot - xq[n_ph + t])
    d_in = jnp.concatenate(d_in, axis=0)
    pstate_ref[...] = jnp.stack(xq[n_new:], axis=0)

    h4, xn, meta = _odd_tail(conv, d_in, h3_ref[...], tail_refs, count_scr, n_experts)
    h4_ref[...] = h4
    _to_row_tiles(xn_ref, xn)
    meta_ref[...] = meta
    cnt_ref[...] = count_scr[...]


def _odd_common_args(conv_b, c_ln_g, c_ln_b, pool_w, pool_scale, w_out, norm_ffn, rw_hi, rw_lo, rb):
    args = (conv_b, c_ln_g, c_ln_b, pool_w, pool_scale, w_out, norm_ffn, rw_hi, rw_lo, rb)
    return args, [_full(a.shape) for a in args]


def _odd_prompt(xn3, h3, w_in, conv_w, common, counts_in, *, tm, n_experts):
    t, d = h3.shape
    cwid = conv_w.shape[1]
    conv_hist = conv_w.shape[0] - 1
    pool_hist = max(POOL_WINDOWS) - 1
    args, specs = common
    row = lambda i: (i, 0)
    hpb = tm // HALO
    in_specs = [pl.BlockSpec((tm, d), row), pl.BlockSpec((HALO, d), lambda i: (jnp.maximum(i * hpb - 1, 0), 0)),
                pl.BlockSpec((tm, d), row), _full(w_in.shape), _full(conv_w.shape)] + specs + [_full((1, LANES))]
    dwid = w_in.shape[1] - 2 * cwid
    return pl.pallas_call(
        functools.partial(_odd_prompt_kernel, n_experts=n_experts, conv_hist=conv_hist, pool_hist=pool_hist),
        grid=(t // tm,), in_specs=in_specs,
        out_specs=[pl.BlockSpec((tm, d), row), pl.BlockSpec((tm, d // LANES, LANES), lambda i: (i, 0, 0)),
                   pl.BlockSpec((tm, LANES), row),
                   _whole((1, LANES)), _whole((conv_hist, cwid)), _whole((pool_hist, dwid))],
        out_shape=[jax.ShapeDtypeStruct((t, d), F32), jax.ShapeDtypeStruct((t, d // LANES, LANES), F32),
                   jax.ShapeDtypeStruct((t, LANES), F32), jax.ShapeDtypeStruct((1, LANES), F32),
                   jax.ShapeDtypeStruct((conv_hist, cwid), F32), jax.ShapeDtypeStruct((pool_hist, dwid), F32)],
        scratch_shapes=[pltpu.VMEM((1, LANES), F32)],
        compiler_params=_cparams(("arbitrary",)), name="odd_prompt",
    )(xn3, xn3, h3, w_in, conv_w, *args, counts_in)


def _odd_sample(xn3, h3, w_in, conv_w, common, counts_in, conv_hist_t, pool_hist_t, *, n_batch, n_experts):
    t, d = h3.shape
    cwid = conv_w.shape[1]
    dwid = w_in.shape[1] - 2 * cwid
    args, specs = common
    in_specs = [_full((t, d)), _full((t, d)), _full(w_in.shape), _full(conv_w.shape)] + specs + \
               [_full((1, LANES)), _full(conv_hist_t.shape), _full(pool_hist_t.shape)]
    return pl.pallas_call(
        functools.partial(_odd_sample_kernel, n_experts=n_experts, n_batch=n_batch),
        grid=(1,), in_specs=in_specs,
        out_specs=[_whole((t, d)), _whole((t, d // LANES, LANES)), _whole((t, LANES)), _whole((1, LANES)),
                   _whole(conv_hist_t.shape), _whole(pool_hist_t.shape)],
        out_shape=[jax.ShapeDtypeStruct((t, d), F32), jax.ShapeDtypeStruct((t, d // LANES, LANES), F32),
                   jax.ShapeDtypeStruct((t, LANES), F32), jax.ShapeDtypeStruct((1, LANES), F32),
                   jax.ShapeDtypeStruct(conv_hist_t.shape, F32), jax.ShapeDtypeStruct(pool_hist_t.shape, F32)],
        scratch_shapes=[pltpu.VMEM((1, LANES), F32)],
        compiler_params=_cparams(("arbitrary",)), name="odd_sample",
    )(xn3, h3, w_in, conv_w, *args, counts_in, conv_hist_t, pool_hist_t)


def _row_copy(src, s, dst, d, sem):
    return pltpu.make_async_copy(src.at[pl.ds(s, 1)], dst.at[pl.ds(d, 1)], sem)


def _rows_wait(ref, n, sem):
    pltpu.make_async_copy(ref.at[pl.ds(0, n)], ref.at[pl.ds(0, n)], sem).wait()


def _to_row_tiles(ref, x):
    rows, ns, _ = ref.shape
    y = jnp.stack([x[:, s * LANES:(s + 1) * LANES].reshape(rows // 8, 8, LANES) for s in range(ns)], axis=1)
    ref[...] = jnp.swapaxes(y, 1, 2).reshape(rows, ns, LANES)


def _from_row_tiles(ref):
    rows, ns, _ = ref.shape
    y = jnp.swapaxes(ref[...].reshape(rows // 8, 8, ns, LANES), 1, 2)
    return jnp.concatenate([y[:, s].reshape(rows, LANES) for s in range(ns)], axis=1)


def _moe_scatter_kernel(dest_ref, x_ref, buf_in_hbm, buf_hbm, sem, *, tm):
    del buf_in_hbm
    base = pl.program_id(0) * tm

    def start(r, carry):
        for k in range(TOP_K):
            _row_copy(x_ref, r, buf_hbm, dest_ref[(base + r) * TOP_K + k], sem).start()
        return carry

    lax.fori_loop(0, tm, start, 0)
    _rows_wait(buf_hbm, tm * TOP_K, sem)


def _moe_scatter(dest, x, buf, *, tm):
    t = x.shape[0]
    any_spec = pl.BlockSpec(memory_space=pl.ANY)
    return pl.pallas_call(
        functools.partial(_moe_scatter_kernel, tm=tm),
        grid_spec=pltpu.PrefetchScalarGridSpec(
            num_scalar_prefetch=1, grid=(t // tm,),
            in_specs=[pl.BlockSpec((tm,) + x.shape[1:], lambda i, dref: (i, 0, 0)), any_spec], out_specs=any_spec,
            scratch_shapes=[pltpu.SemaphoreType.DMA(())]),
        out_shape=jax.ShapeDtypeStruct(buf.shape, buf.dtype),
        input_output_aliases={2: 0},
        compiler_params=_cparams(("arbitrary",)), name="moe_scatter",
    )(dest, x, buf)


def _moe_experts_kernel(be_ref, nu_ref, x_ref, w1_ref, w3_ref, w2_ref, y_ref, xb_scr, acc_scr, *, nj):
    i = pl.program_id(0)
    j = pl.program_id(1)
    used = i < nu_ref[0]

    @pl.when(used)
    def _():
        @pl.when(j == 0)
        def _():
            xb_scr[...] = _from_row_tiles(x_ref).astype(BF16)

        xb = xb_scr[...]
        hid = jax.nn.silu(_dot(xb, w1_ref[0])) * _dot(xb, w3_ref[0])
        part = _dot(hid.astype(BF16), w2_ref[0])

        @pl.when(j == 0)
        def _():
            acc_scr[...] = part

        @pl.when(j > 0)
        def _():
            acc_scr[...] += part

        @pl.when(j == nj - 1)
        def _():
            _to_row_tiles(y_ref, acc_scr[...])

    @pl.when(jnp.logical_not(used) & (j == nj - 1))
    def _():
        y_ref[...] = jnp.zeros_like(y_ref)


def _moe_experts(blk_e, n_used, xbuf, w1, w3, w2, *, tm, tf):
    rows, ns, _ = xbuf.shape
    d = ns * LANES
    ff = w1.shape[2]
    nj = ff // tf
    nb = rows // tm
    live_j = lambda i, j, nu: jnp.where(i < nu[0], j, nj - 1)
    return pl.pallas_call(
        functools.partial(_moe_experts_kernel, nj=nj),
        grid_spec=pltpu.PrefetchScalarGridSpec(
            num_scalar_prefetch=2, grid=(nb, nj),
            in_specs=[pl.BlockSpec((tm, ns, LANES), lambda i, j, be, nu: (i, 0, 0)),
                      pl.BlockSpec((1, d, tf), lambda i, j, be, nu: (be[i], 0, live_j(i, j, nu))),
                      pl.BlockSpec((1, d, tf), lambda i, j, be, nu: (be[i], 0, live_j(i, j, nu))),
                      pl.BlockSpec((1, tf, d), lambda i, j, be, nu: (be[i], live_j(i, j, nu), 0))],
            out_specs=pl.BlockSpec((tm, ns, LANES), lambda i, j, be, nu: (i, 0, 0)),
            scratch_shapes=[pltpu.VMEM((tm, d), BF16), pltpu.VMEM((tm, d), F32)]),
        out_shape=jax.ShapeDtypeStruct((rows, ns, LANES), F32),
        compiler_params=_cparams(("arbitrary", "arbitrary")), name="moe_experts",
    )(blk_e, n_used, xbuf, w1, w3, w2)


def _moe_combine_kernel(dest_ref, y_hbm, h4_ref, meta_ref, p_ref, wp_ref, wg_ref, pn_ref, fn_ref, o_ref,
                        y0_scr, y1_scr, sem, *, tm):
    base = pl.program_id(0) * tm
    bufs = (y0_scr, y1_scr)

    def start(r, carry):
        for k in range(TOP_K):
            _row_copy(y_hbm, dest_ref[(base + r) * TOP_K + k], bufs[k], r, sem).start()
        return carry

    lax.fori_loop(0, tm, start, 0)
    for k in range(TOP_K):
        _rows_wait(bufs[k], tm, sem)
    meta = meta_ref[...]
    lane = lax.broadcasted_iota(I32, meta.shape, 1)
    g1 = jnp.sum(jnp.where(lane == 2, meta, 0.0), axis=-1, keepdims=True)
    g2 = jnp.sum(jnp.where(lane == 3, meta, 0.0), axis=-1, keepdims=True)
    h5 = h4_ref[...] + _from_row_tiles(y0_scr) * g1 + _from_row_tiles(y1_scr) * g2
    h6 = _ple(h5, p_ref, wp_ref, wg_ref, pn_ref)
    o_ref[...] = _rms(h6, fn_ref[...])


def _moe_combine(dest, ybuf, h4, meta, p, wp, wg, pn, fn, *, tm):
    t, d = h4.shape
    row = lambda i, dref: (i, 0)
    full = lambda shape: pl.BlockSpec(shape, lambda i, dref, _n=len(shape): (0,) * _n)
    return pl.pallas_call(
        functools.partial(_moe_combine_kernel, tm=tm),
        grid_spec=pltpu.PrefetchScalarGridSpec(
            num_scalar_prefetch=1, grid=(t // tm,),
            in_specs=[pl.BlockSpec(memory_space=pl.ANY), pl.BlockSpec((tm, d), row), pl.BlockSpec((tm, LANES), row),
                      pl.BlockSpec((tm, p.shape[1]), row), full(wp.shape), full(wg.shape), full((1, d)), full((1, d))],
            out_specs=pl.BlockSpec((tm, d), row),
            scratch_shapes=[pltpu.VMEM((tm, d // LANES, LANES), F32), pltpu.VMEM((tm, d // LANES, LANES), F32),
                            pltpu.SemaphoreType.DMA(())]),
        out_shape=jax.ShapeDtypeStruct((t, d), F32),
        compiler_params=_cparams(("arbitrary",)), name="moe_combine",
    )(dest, ybuf, h4, meta, p, wp, wg, pn, fn)


def _rope_tables(pos):
    half = HEAD_DIM // 2
    inv = ROPE_THETA ** (-jnp.arange(half, dtype=F32) * (2.0 / HEAD_DIM))
    ang = pos.astype(F32)[:, None] * inv[None, :]
    cos, sin = jnp.cos(ang), jnp.sin(ang)
    reps = LANES // HEAD_DIM
    cos_t = jnp.tile(jnp.concatenate([cos, cos], axis=1), (1, reps))
    sin_t = jnp.tile(jnp.concatenate([-sin, sin], axis=1), (1, reps))
    return cos_t, sin_t


def _row_tile(t, pref):
    for tm in (pref, 256, 128, 64, 32, 16, 8):
        if tm <= t and t % tm == 0:
            return tm
    return t


def kernel(x_prompt, x_sample, cache_k, cache_v, state_conv, state_pool, page_table, p_prompt, p_sample, norm_mix_e, w_in_e, a_ln_g, a_ln_b, a_ws, a_bs, w_out_e, norm_ffn_e, ffn_w1, ffn_w3, ffn_w2, norm_mix_o, w_in_o, conv_w, conv_b, c_ln_g, c_ln_b, pool_w, pool_scale, w_out_o, norm_ffn_o, router_w, router_b, exp_w1, exp_w3, exp_w2, ple_w, ple_gate_w, ple_norm, final_norm):
    assert x_prompt.shape[0] == 1 and norm_mix_e.shape[0] == 1 and norm_mix_o.shape[0] == 1 and ple_w.shape[0] == 2
    n_tok, d = x_prompt.shape[1], x_prompt.shape[2]
    n_batch, n_new = x_sample.shape[0], x_sample.shape[1]
    n_s = n_batch * n_new
    a_width = a_ln_g.shape[1]
    b_width = (w_in_e.shape[2] - 2 * a_width) // 3
    n_heads = b_width // HEAD_DIM
    gdim = a_width // A_GROUPS
    n_experts = router_w.shape[2]
    page = cache_k.shape[2]
    past_len = page_table.shape[1] * page
    assert n_tok % (MOBA_BLOCK * MOBA_UNROLL) == 0 and n_tok // MOBA_BLOCK <= HEAD_DIM and past_len % MOBA_BLOCK == 0
    assert n_new <= CHUNK and CHUNK % n_new == 0 and n_s % CHUNK == 0

    bf = lambda w: w.astype(BF16)
    row2 = lambda v: v.reshape(1, -1)

    w_in = bf(w_in_e[0])
    ws_p = a_ws[0]
    bias_p = jnp.repeat(a_bs[0].T, gdim, axis=1)
    reps = CHUNK // n_new
    ws_s = jnp.tile(a_ws[0][:, :n_new, :n_new], (1, reps, reps))
    bias_s = jnp.tile(jnp.repeat(a_bs[0][:, :n_new].T, gdim, axis=1), (reps, 1))
    cos_p, sin_p = _rope_tables(jnp.arange(n_tok, dtype=I32))
    cos_s, sin_s = _rope_tables(past_len + (jnp.arange(n_s, dtype=I32) % n_new))
    hp0 = x_prompt[0]
    hs0 = x_sample.reshape(n_s, d)
    ev = dict(a_width=a_width, b_width=b_width)
    tm_p = _row_tile(n_tok, 512)
    a_p, q_p, k_p, v_p, kaug, vaug, kmean = _even_in(
        hp0, row2(norm_mix_e[0]), w_in, row2(a_ln_g[0]), row2(a_ln_b[0]), ws_p, bias_p, cos_p, sin_p,
        tm=tm_p, seq_l=CHUNK, prompt=True, **ev)
    a_s, va_s, q_s, k_s, v_s = _even_in(
        hs0, row2(norm_mix_e[0]), w_in, row2(a_ln_g[0]), row2(a_ln_b[0]), ws_s, bias_s, cos_s, sin_s,
        tm=_row_tile(n_s, 256), seq_l=n_new, prompt=False, **ev)

    n_blk = n_tok // MOBA_BLOCK
    km = kmean.reshape(n_blk, n_heads // 2, 2, HEAD_DIM)
    km_t = jnp.zeros((n_heads // 2, 2, LANES, LANES), F32)
    km_t = km_t.at[:, 0, :HEAD_DIM, HEAD_DIM:HEAD_DIM + n_blk].set(km[:, :, 0].transpose(1, 2, 0))
    km_t = km_t.at[:, 1, HEAD_DIM:, :n_blk].set(km[:, :, 1].transpose(1, 2, 0))
    km_hi = km_t.astype(BF16)
    km_lo = (km_t - km_hi.astype(F32)).astype(BF16)
    attn_p = _moba_prompt(q_p, kaug, vaug, km_hi, km_lo)
    attn_s = _moba_sample(q_s, k_s, v_s, cache_k[0], cache_v[0], page_table, n_batch)

    tail = (bf(w_out_e[0]), row2(norm_ffn_e[0]), bf(ffn_w1[0]), bf(ffn_w3[0]), bf(ffn_w2[0]))
    ple0 = (bf(ple_w[0]), bf(ple_gate_w[0]), row2(ple_norm[0]), row2(norm_mix_o[0]))
    h3_p, xn3_p = _even_tail(hp0, a_p, attn_p, *tail, p_prompt[0, 0], *ple0, tm=tm_p)
    h3_s, xn3_s = _even_tail(hs0, a_s, attn_s, *tail, p_sample[0].reshape(n_s, -1), *ple0, tm=_row_tile(n_s, 256))

    def time_major(x):
        return x.reshape(n_batch, n_new, -1).transpose(1, 0, 2).reshape(n_s, -1)

    rw = jnp.zeros((d, LANES), F32).at[:, :n_experts].set(router_w[0])
    rw_hi = rw.astype(BF16)
    rw_lo = (rw - rw_hi.astype(F32)).astype(BF16)
    rb = jnp.zeros((1, LANES), F32).at[0, :n_experts].set(router_b[0])
    common = _odd_common_args(row2(conv_b[0]), row2(c_ln_g[0]), row2(c_ln_b[0]), bf(pool_w[0]), row2(pool_scale[0]),
                              bf(w_out_o[0]), row2(norm_ffn_o[0]), rw_hi, rw_lo, rb)
    w_in1 = bf(w_in_o[0])
    zero_counts = jnp.zeros((1, LANES), F32)
    h4_p, xn5_p, meta_p, cnt_p, cstate_p, pstate_p = _odd_prompt(
        xn3_p, h3_p, w_in1, conv_w[0], common, zero_counts, tm=tm_p, n_experts=n_experts)
    h4_s, xn5_s, meta_s, cnt_all, cstate_s, pstate_s = _odd_sample(
        time_major(xn3_s), time_major(h3_s), w_in1, conv_w[0], common, cnt_p,
        state_conv[0].transpose(1, 0, 2), state_pool[0].transpose(1, 0, 2), n_batch=n_batch, n_experts=n_experts)

    counts = cnt_all[0, :n_experts].astype(I32)
    padded = (counts + MOE_TM - 1) // MOE_TM * MOE_TM
    pends = jnp.cumsum(padded)
    pstarts = (pends - padded).astype(F32)
    n_rows_all = (n_tok + n_s) * TOP_K
    n_blocks = -(-n_rows_all // MOE_TM) + n_experts
    blk_row0 = jnp.arange(n_blocks, dtype=I32) * MOE_TM
    blk_e = jnp.minimum(jnp.sum((pends[None, :] <= blk_row0[:, None]).astype(I32), axis=1), n_experts - 1)
    n_used = (pends[-1] // MOE_TM).astype(I32).reshape(1)
    last_e = blk_e[jnp.maximum(n_used[0] - 1, 0)]
    blk_e = jnp.where(jnp.arange(n_blocks) < n_used[0], blk_e, last_e)

    def dests(meta):
        e = meta[:, :TOP_K].astype(I32)
        return (pstarts[e] + meta[:, 4:4 + TOP_K]).astype(I32).reshape(-1)

    dest_p, dest_s = dests(meta_p), dests(meta_s)
    xbuf = jnp.zeros((n_blocks * MOE_TM, d // LANES, LANES), F32)
    xbuf = _moe_scatter(dest_p, xn5_p, xbuf, tm=_row_tile(n_tok, 512))
    xbuf = _moe_scatter(dest_s, xn5_s, xbuf, tm=_row_tile(n_s, 256))
    ff_e = exp_w1.shape[3]
    tf = next((c for c in (1792, 1024, 512, 256) if ff_e % c == 0), ff_e)
    ybuf = _moe_experts(blk_e, n_used, xbuf, bf(exp_w1[0]), bf(exp_w3[0]), bf(exp_w2[0]), tm=MOE_TM, tf=tf)
    ple1 = (bf(ple_w[1]), bf(ple_gate_w[1]), row2(ple_norm[1]), row2(final_norm))
    y_p = _moe_combine(dest_p, ybuf, h4_p, meta_p, p_prompt[1, 0], *ple1, tm=_row_tile(n_tok, 256))
    y_s = _moe_combine(dest_s, ybuf, h4_s, meta_s, time_major(p_sample[1].reshape(n_s, -1)), *ple1,
                       tm=_row_tile(n_s, 256))

    def batch_major(x):
        return x.reshape(n_new, n_batch, -1).transpose(1, 0, 2)

    kv = lambda x, b, l: x.reshape(1, b, l, n_heads, HEAD_DIM)
    return (y_p[None], batch_major(y_s),
            kv(k_p, 1, n_tok), kv(v_p, 1, n_tok), kv(k_s, n_batch, n_new), kv(v_s, n_batch, n_new),
            va_s.reshape(1, n_batch, n_new, a_width),
            cstate_p[None, None], cstate_s.transpose(1, 0, 2)[None],
            pstate_p[None, None], pstate_s.transpose(1, 0, 2)[None])
```

```python
import functools
import math

import jax
import jax.numpy as jnp
from jax import lax
from jax.experimental import pallas as pl
from jax.experimental.pallas import tpu as pltpu

F32 = jnp.float32
BF16 = jnp.bfloat16
I32 = jnp.int32

EPS = 1e-6
NEG = -1e30
BIG_INDEX = 1e9
LOG2E = 1.4426950408889634
CHUNK = 128
A_GROUPS = 8
HEAD_DIM = 64
MOBA_BLOCK = 256
MOBA_TOPK = 3
ROPE_THETA = 10000.0
POOL_WINDOWS = (2, 4, 8, 16)
TOP_K = 2

LANES = 128
MXU_WIDTH = 256
VMEM_LIMIT_BYTES = 56 * 1024 * 1024
HALO = 32
MOE_TM = 512
MOBA_UNROLL = 4
MOBA_SHARE = 2
KV_PAGES_PER_STEP = 32


def _cparams(sem):
    return pltpu.CompilerParams(dimension_semantics=sem, vmem_limit_bytes=VMEM_LIMIT_BYTES)


def _full(shape):
    n = len(shape)
    return pl.BlockSpec(shape, lambda *a, _n=n: (0,) * _n, pipeline_mode=pl.Buffered(1))


def _whole(shape):
    n = len(shape)
    return pl.BlockSpec(shape, lambda *a, _n=n: (0,) * _n)


def _dot(a, b):
    return jnp.dot(a, b, preferred_element_type=F32)


def _dot_nt(a, b):
    return lax.dot_general(a, b, (((1,), (1,)), ((), ())), preferred_element_type=F32)


def _rms(x, g):
    return x * lax.rsqrt(jnp.mean(x * x, axis=-1, keepdims=True) + EPS) * g


def _layernorm(x, g, b):
    xc = x - jnp.mean(x, axis=-1, keepdims=True)
    var = jnp.mean(xc * xc, axis=-1, keepdims=True)
    return xc * lax.rsqrt(var + EPS) * g + b


def _split_bf16(x):
    hi = x.astype(BF16)
    lo = (x - hi.astype(F32)).astype(BF16)
    return hi, lo


def _lane_pick(x, lane, idx):
    return jnp.sum(jnp.where(lane == idx, x, 0.0), axis=-1, keepdims=True)


def _even_in_kernel(h_ref, g_ref, w_ref, lng_ref, lnb_ref, ws_ref, bias_ref, cos_ref, sin_ref, *outs,
                    seq_l, prompt, a_width, b_width):
    tm = h_ref.shape[0]
    i = pl.program_id(0)
    xn = _rms(h_ref[...], g_ref[...]).astype(BF16)
    aw, bw = a_width, b_width

    u = jax.nn.gelu(_dot(xn, w_ref[:, 0:aw]))
    va = _layernorm(jax.nn.gelu(_dot(xn, w_ref[:, aw:2 * aw])), lng_ref[...], lnb_ref[...])
    vab = va.astype(BF16)

    r = lax.broadcasted_iota(I32, (CHUNK, CHUNK), 0)
    c = lax.broadcasted_iota(I32, (CHUNK, CHUNK), 1)
    sh = int(math.log2(seq_l))
    ok = (lax.shift_right_logical(r, sh) == lax.shift_right_logical(c, sh)) & (c <= r)
    wm = [jnp.where(ok, ws_ref[g], 0.0).astype(BF16) for g in range(A_GROUPS)]
    lane = lax.broadcasted_iota(I32, (CHUNK, LANES), 1)
    gdim = aw // A_GROUPS
    gates = []
    for ci in range(tm // CHUNK):
        vc = vab[ci * CHUNK:(ci + 1) * CHUNK]
        parts = []
        for p in range(aw // LANES):
            vp = vc[:, p * LANES:(p + 1) * LANES]
            r0 = _dot(wm[2 * p], vp)
            r1 = _dot(wm[2 * p + 1], vp)
            parts.append(jnp.where(lane < gdim, r0, r1))
        gates.append(jnp.concatenate(parts, axis=1) + bias_ref[...])
    gate = jnp.concatenate(gates, axis=0)
    a_out = u * gate

    cosv = cos_ref[...]
    sinv = sin_ref[...]
    lane_t = lax.broadcasted_iota(I32, (tm, LANES), 1)
    first_half = (lane_t & (HEAD_DIM - 1)) < HEAD_DIM // 2

    def rope(z):
        parts = []
        for p in range(bw // LANES):
            xs = z[:, p * LANES:(p + 1) * LANES]
            rot = jnp.where(first_half, pltpu.roll(xs, LANES - HEAD_DIM // 2, 1), pltpu.roll(xs, HEAD_DIM // 2, 1))
            parts.append(xs * cosv + rot * sinv)
        return parts

    q_parts = rope(_dot(xn, w_ref[:, 2 * aw:2 * aw + bw]))
    k_parts = rope(_dot(xn, w_ref[:, 2 * aw + bw:2 * aw + 2 * bw]))
    v = _dot(xn, w_ref[:, 2 * aw + 2 * bw:2 * aw + 3 * bw])
    k = jnp.concatenate(k_parts, axis=1)
    q = jnp.concatenate(q_parts, axis=1) * (HEAD_DIM ** -0.5)

    if prompt:
        a_ref, q_ref, k_ref, v_ref, kaug_ref, vaug_ref, kmean_ref = outs
        a_ref[...] = a_out.astype(BF16)
        q_ref[...] = (q * LOG2E).astype(BF16)
        k_ref[...] = k
        v_ref[...] = v
        row = lax.broadcasted_iota(I32, (tm, LANES), 0)
        blk = i * (tm // MOBA_BLOCK) + lax.shift_right_logical(row, int(math.log2(MOBA_BLOCK)))
        lo = lane_t < HEAD_DIM
        hot_hi = jnp.where(blk == lane_t - HEAD_DIM, 1.0, 0.0)
        hot_lo = jnp.where(blk == lane_t, 1.0, 0.0)
        for p in range(bw // LANES):
            kp = k_parts[p]
            vp = v[:, p * LANES:(p + 1) * LANES]
            kaug_ref[:, (2 * p) * LANES:(2 * p + 1) * LANES] = jnp.where(lo, kp, hot_hi).astype(BF16)
            kaug_ref[:, (2 * p + 1) * LANES:(2 * p + 2) * LANES] = jnp.where(lo, hot_lo, kp).astype(BF16)
            vaug_ref[:, (2 * p) * LANES:(2 * p + 1) * LANES] = jnp.where(lo, vp, 1.0).astype(BF16)
            vaug_ref[:, (2 * p + 1) * LANES:(2 * p + 2) * LANES] = jnp.where(lo, 1.0, vp).astype(BF16)
        for b in range(tm // MOBA_BLOCK):
            kmean_ref[b] = jnp.mean(k[b * MOBA_BLOCK:(b + 1) * MOBA_BLOCK], axis=0, keepdims=True)
    else:
        a_ref, va_ref, q_ref, k_ref, v_ref = outs
        a_ref[...] = a_out.astype(BF16)
        va_ref[...] = va
        q_ref[...] = q
        k_ref[...] = k
        v_ref[...] = v


def _even_in(h, norm_g, w_in, ln_g, ln_b, ws, bias, cos_t, sin_t, *, tm, seq_l, prompt, a_width, b_width):
    t, d = h.shape
    nw = w_in.shape[1]
    row = lambda i: (i, 0)
    in_specs = [pl.BlockSpec((tm, d), row), _full((1, d)), _full((d, nw)), _full((1, a_width)), _full((1, a_width)),
                _full(ws.shape), _full(bias.shape), pl.BlockSpec((tm, LANES), row), pl.BlockSpec((tm, LANES), row)]
    if prompt:
        nb = t // MOBA_BLOCK
        out_shape = [jax.ShapeDtypeStruct((t, a_width), BF16), jax.ShapeDtypeStruct((t, b_width), BF16),
                     jax.ShapeDtypeStruct((t, b_width), F32), jax.ShapeDtypeStruct((t, b_width), F32),
                     jax.ShapeDtypeStruct((t, 2 * b_width), BF16), jax.ShapeDtypeStruct((t, 2 * b_width), BF16),
                     jax.ShapeDtypeStruct((nb, 1, b_width), F32)]
        out_specs = [pl.BlockSpec((tm, a_width), row), pl.BlockSpec((tm, b_width), row),
                     pl.BlockSpec((tm, b_width), row), pl.BlockSpec((tm, b_width), row),
                     pl.BlockSpec((tm, 2 * b_width), row), pl.BlockSpec((tm, 2 * b_width), row),
                     pl.BlockSpec((tm // MOBA_BLOCK, 1, b_width), lambda i: (i, 0, 0))]
    else:
        out_shape = [jax.ShapeDtypeStruct((t, a_width), BF16), jax.ShapeDtypeStruct((t, a_width), F32),
                     jax.ShapeDtypeStruct((t, b_width), F32), jax.ShapeDtypeStruct((t, b_width), F32),
                     jax.ShapeDtypeStruct((t, b_width), F32)]
        out_specs = [pl.BlockSpec((tm, a_width), row), pl.BlockSpec((tm, a_width), row),
                     pl.BlockSpec((tm, b_width), row), pl.BlockSpec((tm, b_width), row),
                     pl.BlockSpec((tm, b_width), row)]
    return pl.pallas_call(
        functools.partial(_even_in_kernel, seq_l=seq_l, prompt=prompt, a_width=a_width, b_width=b_width),
        grid=(t // tm,), in_specs=in_specs, out_specs=out_specs, out_shape=out_shape,
        compiler_params=_cparams(("arbitrary",)), name="even_in_prompt" if prompt else "even_in_sample",
    )(h, norm_g, w_in, ln_g, ln_b, ws, bias, cos_t, sin_t)


def _moba_prompt_kernel(q_ref, k0_ref, k1_ref, v0_ref, v1_ref, kmh_ref, kml_ref, o_ref, p_scr, m_scr):
    tq = q_ref.shape[0]
    i = pl.program_id(1)
    q = q_ref[...]
    qf = q.astype(F32)
    lane = lax.broadcasted_iota(I32, (tq, LANES), 1)
    k_refs = (k0_ref, k1_ref)
    v_refs = (v0_ref, v1_ref)
    neg_inf = jnp.float32(-jnp.inf)

    qa, q_own = [], []
    for e in range(2):
        in_blk = (lane >= HEAD_DIM) if e == 0 else (lane < HEAD_DIM)
        blk = lane - HEAD_DIM if e == 0 else lane
        sc = _dot(q, kmh_ref[0, e]) + _dot(q, kml_ref[0, e])
        cand = jnp.where(in_blk & (blk < i), sc, neg_inf)
        sel = jnp.zeros((tq, LANES), jnp.bool_)
        blk_f = blk.astype(F32)
        for _ in range(MOBA_TOPK):
            mx = jnp.max(cand, axis=-1, keepdims=True)
            first = jnp.min(jnp.where(cand == mx, blk_f, BIG_INDEX), axis=-1, keepdims=True)
            pick = (blk_f == first) & in_blk & (mx > neg_inf)
            sel = sel | pick
            cand = jnp.where(pick, neg_inf, cand)
        qa.append(jnp.where(in_blk, jnp.where(sel, 0.0, NEG), qf).astype(BF16))
        q_own.append(jnp.where(in_blk, 0.0, qf).astype(BF16))

    n_blk = k0_ref.shape[0] // MOBA_BLOCK

    def offset(blk_idx):
        return pl.multiple_of(jnp.minimum(blk_idx, n_blk - 1) * MOBA_BLOCK, MOBA_BLOCK)

    def probs(e, qe, blk_idx, mask=None):
        s = _dot_nt(qe, k_refs[e][pl.ds(offset(blk_idx), MOBA_BLOCK), :])
        if mask is not None:
            s = jnp.where(mask, s, NEG)
        m = jnp.max(s, axis=-1, keepdims=True)
        return m, jnp.exp2((s - m).astype(BF16))

    def values(e, p, blk_idx):
        return _dot(p, v_refs[e][pl.ds(offset(blk_idx), MOBA_BLOCK), :])

    def stash(g):
        slot = g & 1
        for e in range(2):
            for u0 in range(0, MOBA_UNROLL, MOBA_SHARE):
                s_us = [_dot_nt(qa[e], k_refs[e][pl.ds(offset(g * MOBA_UNROLL + u0 + x), MOBA_BLOCK), :])
                        for x in range(MOBA_SHARE)]
                top = s_us[0]
                for s_u in s_us[1:]:
                    top = jnp.maximum(top, s_u)
                m_u = jnp.max(top, axis=-1, keepdims=True)
                for x, s_u in enumerate(s_us):
                    p_scr[slot, e * MOBA_UNROLL + u0 + x] = jnp.exp2((s_u - m_u).astype(BF16))
                m_scr[slot, e * MOBA_UNROLL + u0] = m_u

    def merge(g, ms, accs):
        slot = g & 1
        new_m, new_acc = [], []
        for e in range(2):
            m_us = [m_scr[slot, e * MOBA_UNROLL + u0] for u0 in range(0, MOBA_UNROLL, MOBA_SHARE)]
            m_new = ms[e]
            for m_u in m_us:
                m_new = jnp.maximum(m_new, m_u)
            acc = jnp.exp2(ms[e] - m_new) * accs[e]
            for j, m_u in enumerate(m_us):
                u0 = j * MOBA_SHARE
                o_u = values(e, p_scr[slot, e * MOBA_UNROLL + u0], g * MOBA_UNROLL + u0)
                for x in range(1, MOBA_SHARE):
                    o_u = o_u + values(e, p_scr[slot, e * MOBA_UNROLL + u0 + x], g * MOBA_UNROLL + u0 + x)
                acc = acc + jnp.exp2(m_u - m_new) * o_u
            new_m.append(m_new)
            new_acc.append(acc)
        return tuple(new_m), tuple(new_acc)

    r = lax.broadcasted_iota(I32, (tq, MOBA_BLOCK), 0)
    c = lax.broadcasted_iota(I32, (tq, MOBA_BLOCK), 1)
    own = [probs(e, q_own[e], i, c <= r) for e in range(2)]
    ms = tuple(m for m, _ in own)
    accs = tuple(values(e, own[e][1], i) for e in range(2))

    def body(g, carry):
        ms, accs = merge(g - 1, *carry)
        stash(g)
        return ms, accs

    n_groups = lax.shift_right_logical(i + (MOBA_UNROLL - 1), int(math.log2(MOBA_UNROLL)))
    stash(0)
    ms, accs = lax.fori_loop(1, n_groups, body, (ms, accs))
    _, accs = merge(jnp.maximum(n_groups - 1, 0), ms, accs)
    outs = [a / pltpu.roll(a, HEAD_DIM, 1) for a in accs]
    o_ref[...] = jnp.where(lane < HEAD_DIM, outs[0], outs[1]).astype(o_ref.dtype)


def _moba_prompt(q, kaug, vaug, km_hi, km_lo):
    t, bw = q.shape
    n_pairs = bw // LANES
    tq = MOBA_BLOCK
    col = lambda e: (lambda p, i, _e=e: (0, 2 * p + _e))
    in_specs = [pl.BlockSpec((tq, LANES), lambda p, i: (i, p)),
                pl.BlockSpec((t, LANES), col(0)), pl.BlockSpec((t, LANES), col(1)),
                pl.BlockSpec((t, LANES), col(0)), pl.BlockSpec((t, LANES), col(1)),
                pl.BlockSpec((1, 2, LANES, LANES), lambda p, i: (p, 0, 0, 0)),
                pl.BlockSpec((1, 2, LANES, LANES), lambda p, i: (p, 0, 0, 0))]
    return pl.pallas_call(
        _moba_prompt_kernel, grid=(n_pairs, t // tq), in_specs=in_specs,
        out_specs=pl.BlockSpec((tq, LANES), lambda p, i: (i, p)),
        out_shape=jax.ShapeDtypeStruct((t, bw), BF16),
        scratch_shapes=[pltpu.VMEM((2, 2 * MOBA_UNROLL, tq, MOBA_BLOCK), BF16),
                        pltpu.VMEM((2, 2 * MOBA_UNROLL, tq, 1), F32)],
        compiler_params=_cparams(("arbitrary", "arbitrary")), name="moba_prompt",
    )(q, kaug, kaug, vaug, vaug, km_hi, km_lo)


def _moba_sample_keys_kernel(pt_ref, *refs, n_steps):
    npg = KV_PAGES_PER_STEP
    pages = refs[:npg]
    qbd_ref, knew_ref, p_ref, pown_ref, linv_ref, s_scr, kmean_scr, bmax_scr = refs[npg:]
    c = pl.program_id(1)
    hw, page = pages[0].shape[1], pages[0].shape[2]
    ppb = MOBA_BLOCK // page
    bps = npg // ppb
    qbd = qbd_ref[0]
    n_rows = qbd.shape[0]
    n_blocks = n_steps * bps
    lane = lax.broadcasted_iota(I32, (n_rows, LANES), 1)
    neg_inf = jnp.float32(-jnp.inf)

    @pl.when(c == 0)
    def _():
        bmax_scr[...] = jnp.full((n_rows, LANES), neg_inf, F32)
        kmean_scr[...] = jnp.zeros_like(kmean_scr)

    lane_k = lax.broadcasted_iota(I32, (hw, LANES), 1)
    for j in range(bps):
        kb = jnp.concatenate([pages[j * ppb + x][0] for x in range(ppb)], axis=1)
        blk = c * bps + j
        kmean_scr[...] = jnp.where(lane_k == blk, jnp.mean(kb, axis=1, keepdims=True), kmean_scr[...])
        s = _dot(qbd, kb.astype(BF16))
        s_scr[:, pl.ds(pl.multiple_of(blk * MOBA_BLOCK, MOBA_BLOCK), MOBA_BLOCK)] = s
        bmax_scr[...] = jnp.where(lane == blk, jnp.max(s, axis=-1, keepdims=True), bmax_scr[...])

    @pl.when(c == n_steps - 1)
    def _():
        kmh, kml = _split_bf16(kmean_scr[...])
        sc = _dot(qbd, kmh) + _dot(qbd, kml)
        cand = jnp.where(lane < n_blocks, sc, neg_inf)
        sel = jnp.zeros((n_rows, LANES), jnp.bool_)
        for _ in range(min(MOBA_TOPK, n_blocks)):
            mx = jnp.max(cand, axis=-1, keepdims=True)
            first = jnp.min(jnp.where(cand == mx, lane, jnp.int32(2 ** 30)), axis=-1, keepdims=True)
            pick = lane == first
            sel = sel | pick
            cand = jnp.where(pick, neg_inf, cand)
        so = _dot_nt(qbd, knew_ref[0].astype(BF16))
        n_new = so.shape[1]
        rr = lax.broadcasted_iota(I32, (n_rows, n_new), 0)
        ss = lax.broadcasted_iota(I32, (n_rows, n_new), 1)
        so = jnp.where(ss <= (rr & (n_new - 1)), so, NEG)
        m = jnp.maximum(jnp.max(jnp.where(sel, bmax_scr[...], neg_inf), axis=-1, keepdims=True),
                        jnp.max(so, axis=-1, keepdims=True))
        po = jnp.exp(so - m)
        l = jnp.sum(po, axis=-1, keepdims=True)
        self32 = sel.astype(F32)
        for b in range(n_blocks):
            on = jnp.sum(jnp.where(lane == b, self32, 0.0), axis=-1, keepdims=True)
            sb = s_scr[:, b * MOBA_BLOCK:(b + 1) * MOBA_BLOCK]
            pb = (jnp.exp(sb - m) * on).astype(BF16)
            l = l + jnp.sum(pb.astype(F32), axis=-1, keepdims=True)
            p_ref[0, :, b * MOBA_BLOCK:(b + 1) * MOBA_BLOCK] = pb
        pown_ref[0] = jnp.concatenate([po, jnp.zeros((n_rows, LANES - n_new), F32)], axis=1)
        linv_ref[0] = jnp.broadcast_to(1.0 / l, (n_rows, LANES))


def _moba_sample_values_kernel(pt_ref, *refs, n_steps, n_heads):
    npg = KV_PAGES_PER_STEP
    pages = refs[:npg]
    p_ref, pown_ref, linv_ref, vnew_ref, o_ref, acc_scr = refs[npg:]
    c = pl.program_id(1)
    vb = jnp.concatenate([pg[0] for pg in pages], axis=1).astype(BF16)
    part = _dot_nt(p_ref[0], vb)

    @pl.when(c == 0)
    def _():
        acc_scr[...] = part

    @pl.when(c > 0)
    def _():
        acc_scr[...] += part

    @pl.when(c == n_steps - 1)
    def _():
        vnew = vnew_ref[0]
        n_new = vnew.shape[0]
        o = (acc_scr[...] + _dot(pown_ref[0][:, :n_new].astype(BF16), vnew.astype(BF16))) * linv_ref[0][:, :1]
        n_rows, hw = o.shape
        rr = lax.broadcasted_iota(I32, (n_rows, hw), 0)
        ll = lax.broadcasted_iota(I32, (n_rows, hw), 1)
        keep = (lax.shift_right_logical(rr, int(math.log2(n_new)))
                == lax.shift_right_logical(ll, int(math.log2(HEAD_DIM))))
        o = jnp.where(keep, o, 0.0).reshape(n_heads, n_new, hw)
        o_ref[0] = jnp.sum(o, axis=0).astype(o_ref.dtype)


def _moba_sample(q, k, v, cache_k, cache_v, page_table, n_batch):
    hw = q.shape[1]
    n_heads = hw // HEAD_DIM
    n_new = q.shape[0] // n_batch
    n_phys, page = cache_k.shape[0], cache_k.shape[1]
    n_pages = page_table.shape[1]
    npg = KV_PAGES_PER_STEP
    assert n_pages % npg == 0 and (n_pages * page) % MOBA_BLOCK == 0 and MOBA_BLOCK % page == 0
    n_steps = n_pages // npg
    past = n_pages * page
    n_blocks = past // MOBA_BLOCK
    assert n_blocks <= LANES and n_new & (n_new - 1) == 0
    n_rows = n_heads * n_new
    ck = cache_k.transpose(0, 2, 3, 1).reshape(n_phys, hw, page)
    cv = cache_v.transpose(0, 2, 3, 1).reshape(n_phys, hw, page)
    pt = page_table.reshape(-1).astype(I32)
    q3 = q.reshape(n_batch, n_new, n_heads, HEAD_DIM)
    eye = jnp.eye(n_heads, dtype=F32)
    qbd = (q3.transpose(0, 2, 1, 3)[:, :, :, None, :] * eye[None, :, None, :, None]).reshape(n_batch, n_rows, hw)
    qbd = qbd.astype(BF16)
    k3 = k.reshape(n_batch, n_new, hw)
    v3 = v.reshape(n_batch, n_new, hw)

    def page_spec(s):
        return pl.BlockSpec((1, hw, page), lambda b, c, pt_ref, _s=s: (pt_ref[b * n_pages + c * npg + _s], 0, 0))

    per_b = lambda shape: pl.BlockSpec(shape, lambda b, c, pt_ref: (b,) + (0,) * (len(shape) - 1))
    p, pown, linv = pl.pallas_call(
        functools.partial(_moba_sample_keys_kernel, n_steps=n_steps),
        grid_spec=pltpu.PrefetchScalarGridSpec(
            num_scalar_prefetch=1, grid=(n_batch, n_steps),
            in_specs=[page_spec(s) for s in range(npg)] + [per_b((1, n_rows, hw)), per_b((1, n_new, hw))],
            out_specs=[per_b((1, n_rows, past)), per_b((1, n_rows, LANES)), per_b((1, n_rows, LANES))],
            scratch_shapes=[pltpu.VMEM((n_rows, past), F32), pltpu.VMEM((hw, LANES), F32),
                            pltpu.VMEM((n_rows, LANES), F32)]),
        out_shape=[jax.ShapeDtypeStruct((n_batch, n_rows, past), BF16),
                   jax.ShapeDtypeStruct((n_batch, n_rows, LANES), F32),
                   jax.ShapeDtypeStruct((n_batch, n_rows, LANES), F32)],
        compiler_params=_cparams(("arbitrary", "arbitrary")), name="moba_sample_keys",
    )(pt, *([ck] * npg), qbd, k3)

    out = pl.pallas_call(
        functools.partial(_moba_sample_values_kernel, n_steps=n_steps, n_heads=n_heads),
        grid_spec=pltpu.PrefetchScalarGridSpec(
            num_scalar_prefetch=1, grid=(n_batch, n_steps),
            in_specs=[page_spec(s) for s in range(npg)]
            + [pl.BlockSpec((1, n_rows, npg * page), lambda b, c, pt_ref: (b, 0, c)),
               per_b((1, n_rows, LANES)), per_b((1, n_rows, LANES)), per_b((1, n_new, hw))],
            out_specs=per_b((1, n_new, hw)),
            scratch_shapes=[pltpu.VMEM((n_rows, hw), F32)]),
        out_shape=jax.ShapeDtypeStruct((n_batch, n_new, hw), BF16),
        compiler_params=_cparams(("arbitrary", "arbitrary")), name="moba_sample_values",
    )(pt, *([cv] * npg), p, pown, linv, v3)
    return out.reshape(n_batch * n_new, hw)


def _ple(h, p_ref, wp_ref, wg_ref, pn_ref):
    gate = jax.nn.sigmoid(_dot(_rms(h, pn_ref[...]).astype(BF16), wg_ref[...]))
    return h + _dot(p_ref[...].astype(BF16), wp_ref[...]) * gate


def _even_tail_kernel(h_ref, a_ref, b_ref, wo_ref, nf_ref, w1_ref, w3_ref, w2_ref, p_ref, wp_ref, wg_ref, pn_ref,
                      nn_ref, h_out_ref, xn_out_ref, *, ff_chunks):
    aw = a_ref.shape[1]
    h1 = h_ref[...] + _dot(a_ref[...], wo_ref[0:aw, :]) + _dot(b_ref[...], wo_ref[aw:, :])
    xn = _rms(h1, nf_ref[...]).astype(BF16)
    ff = w1_ref.shape[1]
    cw = ff // ff_chunks
    hid = []
    for ci in range(ff_chunks):
        sl = slice(ci * cw, (ci + 1) * cw)
        hid.append((jax.nn.silu(_dot(xn, w1_ref[:, sl])) * _dot(xn, w3_ref[:, sl])).astype(BF16))
    acc = h1 + _dot(jnp.concatenate(hid, axis=1), w2_ref[...])
    h3 = _ple(acc, p_ref, wp_ref, wg_ref, pn_ref)
    h_out_ref[...] = h3
    xn_out_ref[...] = _rms(h3, nn_ref[...]).astype(BF16)


def _even_tail(h, a, b, w_out, norm_ffn, w1, w3, w2, p, wp, wg, pn, next_norm, *, tm):
    t, d = h.shape
    ff = w1.shape[1]
    ff_chunks = max((n for n in range(1, ff // MXU_WIDTH + 1) if ff % (n * MXU_WIDTH) == 0 and ff // n >= 1024),
                    default=1)
    row = lambda i: (i, 0)
    in_specs = [pl.BlockSpec((tm, d), row), pl.BlockSpec((tm, a.shape[1]), row), pl.BlockSpec((tm, b.shape[1]), row),
                _full(w_out.shape), _full((1, d)), _full(w1.shape), _full(w3.shape), _full(w2.shape),
                pl.BlockSpec((tm, p.shape[1]), row), _full(wp.shape), _full(wg.shape), _full((1, d)), _full((1, d))]
    return pl.pallas_call(
        functools.partial(_even_tail_kernel, ff_chunks=ff_chunks),
        grid=(t // tm,), in_specs=in_specs,
        out_specs=[pl.BlockSpec((tm, d), row), pl.BlockSpec((tm, d), row)],
        out_shape=[jax.ShapeDtypeStruct((t, d), F32), jax.ShapeDtypeStruct((t, d), BF16)],
        compiler_params=_cparams(("arbitrary",)), name="even_tail",
    )(h, a, b, w_out, norm_ffn, w1, w3, w2, p, wp, wg, pn, next_norm)


def _odd_tail(conv, d_in, h3, refs, count_scr, n_experts):
    (cb_ref, clg_ref, clb_ref, pw_ref, ps_ref, wo_ref, nf_ref, rwh_ref, rwl_ref, rb_ref) = refs
    rows = conv.shape[0]
    cw = conv.shape[1]
    cact = jax.nn.silu(_layernorm(conv + cb_ref[...], clg_ref[...], clb_ref[...]))
    db = d_in.astype(BF16)
    gdim = d_in.shape[1] // len(POOL_WINDOWS)
    dparts = [_dot(db[:, g * gdim:(g + 1) * gdim], pw_ref[g]) for g in range(len(POOL_WINDOWS))]
    dmix = jnp.concatenate(dparts, axis=1) * ps_ref[...]
    h4 = h3 + _dot(cact.astype(BF16), wo_ref[0:cw, :]) + _dot(dmix.astype(BF16), wo_ref[cw:, :])
    xn = _rms(h4, nf_ref[...])
    xh, xl = _split_bf16(xn)
    logits = _dot(xh, rwh_ref[...]) + _dot(xl, rwh_ref[...]) + _dot(xh, rwl_ref[...]) + rb_ref[...]
    lane = lax.broadcasted_iota(I32, (rows, LANES), 1)
    neg_inf = jnp.float32(-jnp.inf)
    cand = jnp.where(lane < n_experts, logits, neg_inf)
    lane_f = lane.astype(F32)
    v1 = jnp.max(cand, axis=-1, keepdims=True)
    e1 = jnp.min(jnp.where(cand == v1, lane_f, BIG_INDEX), axis=-1, keepdims=True)
    cand2 = jnp.where(lane_f == e1, neg_inf, cand)
    v2 = jnp.max(cand2, axis=-1, keepdims=True)
    e2 = jnp.min(jnp.where(cand2 == v2, lane_f, BIG_INDEX), axis=-1, keepdims=True)
    ex = jnp.exp(v2 - v1)
    g1 = 1.0 / (1.0 + ex)
    g2 = ex / (1.0 + ex)
    chosen = jnp.where((lane_f == e1) | (lane_f == e2), 1.0, 0.0).astype(BF16)
    rr = lax.broadcasted_iota(I32, (rows, rows), 0)
    cc = lax.broadcasted_iota(I32, (rows, rows), 1)
    before = jnp.where(cc < rr, 1.0, 0.0).astype(BF16)
    prefix = _dot(before, chosen) + count_scr[...]
    r1 = _lane_pick(prefix, lane_f, e1)
    r2 = _lane_pick(prefix, lane_f, e2)
    count_scr[...] = count_scr[...] + jnp.sum(chosen.astype(F32), axis=0, keepdims=True)
    meta = jnp.zeros((rows, LANES), F32)
    for idx, val in enumerate((e1, e2, g1, g2, r1, r2)):
        meta = jnp.where(lane == idx, val, meta)
    return h4, xn, meta


def _odd_prompt_kernel(xm_ref, xh_ref, h3_ref, wi_ref, cw_ref, *refs, n_experts, conv_hist, pool_hist):
    tail_refs = refs[:10]
    cnt_in_ref, h4_ref, xn_ref, meta_ref, cnt_ref, cstate_ref, pstate_ref, count_scr = refs[10:]
    i = pl.program_id(0)
    tm = xm_ref.shape[0]
    cwid = cw_ref.shape[1]

    @pl.when(i == 0)
    def _():
        count_scr[...] = cnt_in_ref[...]

    x = jnp.concatenate([xh_ref[...], xm_ref[...]], axis=0)
    z = _dot(x, wi_ref[...])
    row = lax.broadcasted_iota(I32, (HALO + tm, 1), 0)
    z = jnp.where((row >= HALO) | (i > 0), z, 0.0)
    c_in = z[:, :cwid] * jax.nn.sigmoid(z[:, cwid:2 * cwid])
    dx = z[:, 2 * cwid:]

    cwv = cw_ref[...]
    width = cwv.shape[0]
    lead = HALO - (width - 1)

    def tap(kp):
        k = kp - lead
        return cwv[k:k + 1, :] if 0 <= k < width else None

    ext = c_in
    conv = None
    for b in range(8):
        n_a = (HALO // 8 + 1) if b == 0 else HALO // 8
        rows_b = tm if b == 0 else tm + 8
        ub = None
        for a in range(n_a):
            w = tap(8 * a + b)
            if w is None:
                continue
            term = ext[8 * a:8 * a + rows_b] * w
            ub = term if ub is None else ub + term
        if ub is None:
            continue
        piece = ub if b == 0 else pltpu.roll(ub, rows_b - b, 0)[:tm]
        conv = piece if conv is None else conv + piece

    s = dx
    sums = {}
    span = 1
    while span < max(POOL_WINDOWS):
        s = s + pltpu.roll(s, span, 0)
        span *= 2
        sums[span] = s
    gdim = dx.shape[1] // len(POOL_WINDOWS)
    pos = i * tm + lax.broadcasted_iota(I32, (tm, 1), 0)
    pooled = []
    for g, w in enumerate(POOL_WINDOWS):
        win = sums[w][HALO:, g * gdim:(g + 1) * gdim]
        cnt = jnp.minimum(pos + 1, w).astype(F32)
        pooled.append(win / cnt)
    dmain = dx[HALO:]
    d_in = jnp.concatenate(pooled, axis=1) - dmain

    h4, xn, meta = _odd_tail(conv, d_in, h3_ref[...], tail_refs, count_scr, n_experts)
    h4_ref[...] = h4
    _to_row_tiles(xn_ref, xn)
    meta_ref[...] = meta
    cnt_ref[...] = count_scr[...]
    ctail = c_in[tm:]
    cstate_ref[...] = pltpu.roll(ctail, conv_hist, 0)[:conv_hist] if conv_hist < HALO else ctail
    ptail_rows = 8 * (-(-pool_hist // 8))
    ptail = dmain[tm - ptail_rows:]
    pstate_ref[...] = pltpu.roll(ptail, pool_hist, 0)[:pool_hist] if pool_hist < ptail_rows else ptail


def _odd_sample_kernel(x_ref, h3_ref, wi_ref, cw_ref, *refs, n_experts, n_batch):
    tail_refs = refs[:10]
    cnt_in_ref, chist_ref, phist_ref, h4_ref, xn_ref, meta_ref, cnt_ref, cstate_ref, pstate_ref, count_scr = refs[10:]
    rows = x_ref.shape[0]
    n_new = rows // n_batch
    cwid = cw_ref.shape[1]
    count_scr[...] = cnt_in_ref[...]
    z = _dot(x_ref[...], wi_ref[...])
    c_in = z[:, :cwid] * jax.nn.sigmoid(z[:, cwid:2 * cwid])
    dx = z[:, 2 * cwid:]
    cwv = cw_ref[...]
    width = cwv.shape[0]
    hist_c = chist_ref[...]
    xp = [hist_c[j] for j in range(width - 1)] + [c_in[t * n_batch:(t + 1) * n_batch] for t in range(n_new)]
    conv = []
    for t in range(n_new):
        acc = xp[t] * cwv[0:1, :]
        for k in range(1, width):
            acc = acc + xp[t + k] * cwv[k:k + 1, :]
        conv.append(acc)
    conv = jnp.concatenate(conv, axis=0)
    cstate_ref[...] = jnp.stack(xp[n_new:], axis=0)

    hist_p = phist_ref[...]
    n_ph = hist_p.shape[0]
    xq = [hist_p[j] for j in range(n_ph)] + [dx[t * n_batch:(t + 1) * n_batch] for t in range(n_new)]
    gdim = dx.shape[1] // len(POOL_WINDOWS)
    lane = lax.broadcasted_iota(I32, (n_batch, dx.shape[1]), 1)
    d_in = []
    for t in range(n_new):
        run = xq[n_ph + t]
        tot = jnp.zeros_like(run)
        done = 1
        for g, w in enumerate(POOL_WINDOWS):
            for j in range(done, w):
                run = run + xq[n_ph + t - j]
            done = w
            in_g = (lane >= g * gdim) & (lane < (g + 1) * gdim)
            tot = jnp.where(in_g, run / float(w), tot)
        d_in.append(tot - xq[n_ph + t])
    d_in = jnp.concatenate(d_in, axis=0)
    pstate_ref[...] = jnp.stack(xq[n_new:], axis=0)

    h4, xn, meta = _odd_tail(conv, d_in, h3_ref[...], tail_refs, count_scr, n_experts)
    h4_ref[...] = h4
    _to_row_tiles(xn_ref, xn)
    meta_ref[...] = meta
    cnt_ref[...] = count_scr[...]


def _odd_common_args(conv_b, c_ln_g, c_ln_b, pool_w, pool_scale, w_out, norm_ffn, rw_hi, rw_lo, rb):
    args = (conv_b, c_ln_g, c_ln_b, pool_w, pool_scale, w_out, norm_ffn, rw_hi, rw_lo, rb)
    return args, [_full(a.shape) for a in args]


def _odd_prompt(xn3, h3, w_in, conv_w, common, counts_in, *, tm, n_experts):
    t, d = h3.shape
    cwid = conv_w.shape[1]
    conv_hist = conv_w.shape[0] - 1
    pool_hist = max(POOL_WINDOWS) - 1
    args, specs = common
    row = lambda i: (i, 0)
    hpb = tm // HALO
    in_specs = [pl.BlockSpec((tm, d), row), pl.BlockSpec((HALO, d), lambda i: (jnp.maximum(i * hpb - 1, 0), 0)),
                pl.BlockSpec((tm, d), row), _full(w_in.shape), _full(conv_w.shape)] + specs + [_full((1, LANES))]
    dwid = w_in.shape[1] - 2 * cwid
    return pl.pallas_call(
        functools.partial(_odd_prompt_kernel, n_experts=n_experts, conv_hist=conv_hist, pool_hist=pool_hist),
        grid=(t // tm,), in_specs=in_specs,
        out_specs=[pl.BlockSpec((tm, d), row), pl.BlockSpec((tm, d // LANES, LANES), lambda i: (i, 0, 0)),
                   pl.BlockSpec((tm, LANES), row),
                   _whole((1, LANES)), _whole((conv_hist, cwid)), _whole((pool_hist, dwid))],
        out_shape=[jax.ShapeDtypeStruct((t, d), F32), jax.ShapeDtypeStruct((t, d // LANES, LANES), F32),
                   jax.ShapeDtypeStruct((t, LANES), F32), jax.ShapeDtypeStruct((1, LANES), F32),
                   jax.ShapeDtypeStruct((conv_hist, cwid), F32), jax.ShapeDtypeStruct((pool_hist, dwid), F32)],
        scratch_shapes=[pltpu.VMEM((1, LANES), F32)],
        compiler_params=_cparams(("arbitrary",)), name="odd_prompt",
    )(xn3, xn3, h3, w_in, conv_w, *args, counts_in)


def _odd_sample(xn3, h3, w_in, conv_w, common, counts_in, conv_hist_t, pool_hist_t, *, n_batch, n_experts):
    t, d = h3.shape
    cwid = conv_w.shape[1]
    dwid = w_in.shape[1] - 2 * cwid
    args, specs = common
    in_specs = [_full((t, d)), _full((t, d)), _full(w_in.shape), _full(conv_w.shape)] + specs + \
               [_full((1, LANES)), _full(conv_hist_t.shape), _full(pool_hist_t.shape)]
    return pl.pallas_call(
        functools.partial(_odd_sample_kernel, n_experts=n_experts, n_batch=n_batch),
        grid=(1,), in_specs=in_specs,
        out_specs=[_whole((t, d)), _whole((t, d // LANES, LANES)), _whole((t, LANES)), _whole((1, LANES)),
                   _whole(conv_hist_t.shape), _whole(pool_hist_t.shape)],
        out_shape=[jax.ShapeDtypeStruct((t, d), F32), jax.ShapeDtypeStruct((t, d // LANES, LANES), F32),
                   jax.ShapeDtypeStruct((t, LANES), F32), jax.ShapeDtypeStruct((1, LANES), F32),
                   jax.ShapeDtypeStruct(conv_hist_t.shape, F32), jax.ShapeDtypeStruct(pool_hist_t.shape, F32)],
        scratch_shapes=[pltpu.VMEM((1, LANES), F32)],
        compiler_params=_cparams(("arbitrary",)), name="odd_sample",
    )(xn3, h3, w_in, conv_w, *args, counts_in, conv_hist_t, pool_hist_t)


def _row_copy(src, s, dst, d, sem):
    return pltpu.make_async_copy(src.at[pl.ds(s, 1)], dst.at[pl.ds(d, 1)], sem)


def _rows_wait(ref, n, sem):
    pltpu.make_async_copy(ref.at[pl.ds(0, n)], ref.at[pl.ds(0, n)], sem).wait()


def _to_row_tiles(ref, x):
    rows, ns, _ = ref.shape
    y = jnp.stack([x[:, s * LANES:(s + 1) * LANES].reshape(rows // 8, 8, LANES) for s in range(ns)], axis=1)
    ref[...] = jnp.swapaxes(y, 1, 2).reshape(rows, ns, LANES)


def _from_row_tiles(ref):
    rows, ns, _ = ref.shape
    y = jnp.swapaxes(ref[...].reshape(rows // 8, 8, ns, LANES), 1, 2)
    return jnp.concatenate([y[:, s].reshape(rows, LANES) for s in range(ns)], axis=1)


def _moe_scatter_kernel(dest_ref, x_ref, buf_in_hbm, buf_hbm, sem, *, tm):
    del buf_in_hbm
    base = pl.program_id(0) * tm

    def start(r, carry):
        for k in range(TOP_K):
            _row_copy(x_ref, r, buf_hbm, dest_ref[(base + r) * TOP_K + k], sem).start()
        return carry

    lax.fori_loop(0, tm, start, 0)
    _rows_wait(buf_hbm, tm * TOP_K, sem)


def _moe_scatter(dest, x, buf, *, tm):
    t = x.shape[0]
    any_spec = pl.BlockSpec(memory_space=pl.ANY)
    return pl.pallas_call(
        functools.partial(_moe_scatter_kernel, tm=tm),
        grid_spec=pltpu.PrefetchScalarGridSpec(
            num_scalar_prefetch=1, grid=(t // tm,),
            in_specs=[pl.BlockSpec((tm,) + x.shape[1:], lambda i, dref: (i, 0, 0)), any_spec], out_specs=any_spec,
            scratch_shapes=[pltpu.SemaphoreType.DMA(())]),
        out_shape=jax.ShapeDtypeStruct(buf.shape, buf.dtype),
        input_output_aliases={2: 0},
        compiler_params=_cparams(("arbitrary",)), name="moe_scatter",
    )(dest, x, buf)


def _moe_experts_kernel(be_ref, nu_ref, x_ref, w1_ref, w3_ref, w2_ref, y_ref, xb_scr, acc_scr, *, nj):
    i = pl.program_id(0)
    j = pl.program_id(1)
    used = i < nu_ref[0]

    @pl.when(used)
    def _():
        @pl.when(j == 0)
        def _():
            xb_scr[...] = _from_row_tiles(x_ref).astype(BF16)

        xb = xb_scr[...]
        hid = jax.nn.silu(_dot(xb, w1_ref[0])) * _dot(xb, w3_ref[0])
        part = _dot(hid.astype(BF16), w2_ref[0])

        @pl.when(j == 0)
        def _():
            acc_scr[...] = part

        @pl.when(j > 0)
        def _():
            acc_scr[...] += part

        @pl.when(j == nj - 1)
        def _():
            _to_row_tiles(y_ref, acc_scr[...])

    @pl.when(jnp.logical_not(used) & (j == nj - 1))
    def _():
        y_ref[...] = jnp.zeros_like(y_ref)


def _moe_experts(blk_e, n_used, xbuf, w1, w3, w2, *, tm, tf):
    rows, ns, _ = xbuf.shape
    d = ns * LANES
    ff = w1.shape[2]
    nj = ff // tf
    nb = rows // tm
    live_j = lambda i, j, nu: jnp.where(i < nu[0], j, nj - 1)
    return pl.pallas_call(
        functools.partial(_moe_experts_kernel, nj=nj),
        grid_spec=pltpu.PrefetchScalarGridSpec(
            num_scalar_prefetch=2, grid=(nb, nj),
            in_specs=[pl.BlockSpec((tm, ns, LANES), lambda i, j, be, nu: (i, 0, 0)),
                      pl.BlockSpec((1, d, tf), lambda i, j, be, nu: (be[i], 0, live_j(i, j, nu))),
                      pl.BlockSpec((1, d, tf), lambda i, j, be, nu: (be[i], 0, live_j(i, j, nu))),
                      pl.BlockSpec((1, tf, d), lambda i, j, be, nu: (be[i], live_j(i, j, nu), 0))],
            out_specs=pl.BlockSpec((tm, ns, LANES), lambda i, j, be, nu: (i, 0, 0)),
            scratch_shapes=[pltpu.VMEM((tm, d), BF16), pltpu.VMEM((tm, d), F32)]),
        out_shape=jax.ShapeDtypeStruct((rows, ns, LANES), F32),
        compiler_params=_cparams(("arbitrary", "arbitrary")), name="moe_experts",
    )(blk_e, n_used, xbuf, w1, w3, w2)


def _moe_combine_kernel(dest_ref, y_hbm, h4_ref, meta_ref, p_ref, wp_ref, wg_ref, pn_ref, fn_ref, o_ref,
                        y0_scr, y1_scr, sem, *, tm):
    base = pl.program_id(0) * tm
    bufs = (y0_scr, y1_scr)

    def start(r, carry):
        for k in range(TOP_K):
            _row_copy(y_hbm, dest_ref[(base + r) * TOP_K + k], bufs[k], r, sem).start()
        return carry

    lax.fori_loop(0, tm, start, 0)
    for k in range(TOP_K):
        _rows_wait(bufs[k], tm, sem)
    meta = meta_ref[...]
    lane = lax.broadcasted_iota(I32, meta.shape, 1)
    g1 = jnp.sum(jnp.where(lane == 2, meta, 0.0), axis=-1, keepdims=True)
    g2 = jnp.sum(jnp.where(lane == 3, meta, 0.0), axis=-1, keepdims=True)
    h5 = h4_ref[...] + _from_row_tiles(y0_scr) * g1 + _from_row_tiles(y1_scr) * g2
    h6 = _ple(h5, p_ref, wp_ref, wg_ref, pn_ref)
    o_ref[...] = _rms(h6, fn_ref[...])


def _moe_combine(dest, ybuf, h4, meta, p, wp, wg, pn, fn, *, tm):
    t, d = h4.shape
    row = lambda i, dref: (i, 0)
    full = lambda shape: pl.BlockSpec(shape, lambda i, dref, _n=len(shape): (0,) * _n)
    return pl.pallas_call(
        functools.partial(_moe_combine_kernel, tm=tm),
        grid_spec=pltpu.PrefetchScalarGridSpec(
            num_scalar_prefetch=1, grid=(t // tm,),
            in_specs=[pl.BlockSpec(memory_space=pl.ANY), pl.BlockSpec((tm, d), row), pl.BlockSpec((tm, LANES), row),
                      pl.BlockSpec((tm, p.shape[1]), row), full(wp.shape), full(wg.shape), full((1, d)), full((1, d))],
            out_specs=pl.BlockSpec((tm, d), row),
            scratch_shapes=[pltpu.VMEM((tm, d // LANES, LANES), F32), pltpu.VMEM((tm, d // LANES, LANES), F32),
                            pltpu.SemaphoreType.DMA(())]),
        out_shape=jax.ShapeDtypeStruct((t, d), F32),
        compiler_params=_cparams(("arbitrary",)), name="moe_combine",
    )(dest, ybuf, h4, meta, p, wp, wg, pn, fn)


def _rope_tables(pos):
    half = HEAD_DIM // 2
    inv = ROPE_THETA ** (-jnp.arange(half, dtype=F32) * (2.0 / HEAD_DIM))
    ang = pos.astype(F32)[:, None] * inv[None, :]
    cos, sin = jnp.cos(ang), jnp.sin(ang)
    reps = LANES // HEAD_DIM
    cos_t = jnp.tile(jnp.concatenate([cos, cos], axis=1), (1, reps))
    sin_t = jnp.tile(jnp.concatenate([-sin, sin], axis=1), (1, reps))
    return cos_t, sin_t


def _row_tile(t, pref):
    for tm in (pref, 256, 128, 64, 32, 16, 8):
        if tm <= t and t % tm == 0:
            return tm
    return t


def kernel(x_prompt, x_sample, cache_k, cache_v, state_conv, state_pool, page_table, p_prompt, p_sample, norm_mix_e, w_in_e, a_ln_g, a_ln_b, a_ws, a_bs, w_out_e, norm_ffn_e, ffn_w1, ffn_w3, ffn_w2, norm_mix_o, w_in_o, conv_w, conv_b, c_ln_g, c_ln_b, pool_w, pool_scale, w_out_o, norm_ffn_o, router_w, router_b, exp_w1, exp_w3, exp_w2, ple_w, ple_gate_w, ple_norm, final_norm):
    assert x_prompt.shape[0] == 1 and norm_mix_e.shape[0] == 1 and norm_mix_o.shape[0] == 1 and ple_w.shape[0] == 2
    n_tok, d = x_prompt.shape[1], x_prompt.shape[2]
    n_batch, n_new = x_sample.shape[0], x_sample.shape[1]
    n_s = n_batch * n_new
    a_width = a_ln_g.shape[1]
    b_width = (w_in_e.shape[2] - 2 * a_width) // 3
    n_heads = b_width // HEAD_DIM
    gdim = a_width // A_GROUPS
    n_experts = router_w.shape[2]
    page = cache_k.shape[2]
    past_len = page_table.shape[1] * page
    assert n_tok % (MOBA_BLOCK * MOBA_UNROLL) == 0 and n_tok // MOBA_BLOCK <= HEAD_DIM and past_len % MOBA_BLOCK == 0
    assert n_new <= CHUNK and CHUNK % n_new == 0 and n_s % CHUNK == 0

    bf = lambda w: w.astype(BF16)
    row2 = lambda v: v.reshape(1, -1)

    w_in = bf(w_in_e[0])
    ws_p = a_ws[0]
    bias_p = jnp.repeat(a_bs[0].T, gdim, axis=1)
    reps = CHUNK // n_new
    ws_s = jnp.tile(a_ws[0][:, :n_new, :n_new], (1, reps, reps))
    bias_s = jnp.tile(jnp.repeat(a_bs[0][:, :n_new].T, gdim, axis=1), (reps, 1))
    cos_p, sin_p = _rope_tables(jnp.arange(n_tok, dtype=I32))
    cos_s, sin_s = _rope_tables(past_len + (jnp.arange(n_s, dtype=I32) % n_new))
    hp0 = x_prompt[0]
    hs0 = x_sample.reshape(n_s, d)
    ev = dict(a_width=a_width, b_width=b_width)
    tm_p = _row_tile(n_tok, 512)
    a_p, q_p, k_p, v_p, kaug, vaug, kmean = _even_in(
        hp0, row2(norm_mix_e[0]), w_in, row2(a_ln_g[0]), row2(a_ln_b[0]), ws_p, bias_p, cos_p, sin_p,
        tm=tm_p, seq_l=CHUNK, prompt=True, **ev)
    a_s, va_s, q_s, k_s, v_s = _even_in(
        hs0, row2(norm_mix_e[0]), w_in, row2(a_ln_g[0]), row2(a_ln_b[0]), ws_s, bias_s, cos_s, sin_s,
        tm=_row_tile(n_s, 256), seq_l=n_new, prompt=False, **ev)

    n_blk = n_tok // MOBA_BLOCK
    km = kmean.reshape(n_blk, n_heads // 2, 2, HEAD_DIM)
    km_t = jnp.zeros((n_heads // 2, 2, LANES, LANES), F32)
    km_t = km_t.at[:, 0, :HEAD_DIM, HEAD_DIM:HEAD_DIM + n_blk].set(km[:, :, 0].transpose(1, 2, 0))
    km_t = km_t.at[:, 1, HEAD_DIM:, :n_blk].set(km[:, :, 1].transpose(1, 2, 0))
    km_hi = km_t.astype(BF16)
    km_lo = (km_t - km_hi.astype(F32)).astype(BF16)
    attn_p = _moba_prompt(q_p, kaug, vaug, km_hi, km_lo)
    attn_s = _moba_sample(q_s, k_s, v_s, cache_k[0], cache_v[0], page_table, n_batch)

    tail = (bf(w_out_e[0]), row2(norm_ffn_e[0]), bf(ffn_w1[0]), bf(ffn_w3[0]), bf(ffn_w2[0]))
    ple0 = (bf(ple_w[0]), bf(ple_gate_w[0]), row2(ple_norm[0]), row2(norm_mix_o[0]))
    h3_p, xn3_p = _even_tail(hp0, a_p, attn_p, *tail, p_prompt[0, 0], *ple0, tm=tm_p)
    h3_s, xn3_s = _even_tail(hs0, a_s, attn_s, *tail, p_sample[0].reshape(n_s, -1), *ple0, tm=_row_tile(n_s, 256))

    def time_major(x):
        return x.reshape(n_batch, n_new, -1).transpose(1, 0, 2).reshape(n_s, -1)

    rw = jnp.zeros((d, LANES), F32).at[:, :n_experts].set(router_w[0])
    rw_hi = rw.astype(BF16)
    rw_lo = (rw - rw_hi.astype(F32)).astype(BF16)
    rb = jnp.zeros((1, LANES), F32).at[0, :n_experts].set(router_b[0])
    common = _odd_common_args(row2(conv_b[0]), row2(c_ln_g[0]), row2(c_ln_b[0]), bf(pool_w[0]), row2(pool_scale[0]),
                              bf(w_out_o[0]), row2(norm_ffn_o[0]), rw_hi, rw_lo, rb)
    w_in1 = bf(w_in_o[0])
    zero_counts = jnp.zeros((1, LANES), F32)
    h4_p, xn5_p, meta_p, cnt_p, cstate_p, pstate_p = _odd_prompt(
        xn3_p, h3_p, w_in1, conv_w[0], common, zero_counts, tm=tm_p, n_experts=n_experts)
    h4_s, xn5_s, meta_s, cnt_all, cstate_s, pstate_s = _odd_sample(
        time_major(xn3_s), time_major(h3_s), w_in1, conv_w[0], common, cnt_p,
        state_conv[0].transpose(1, 0, 2), state_pool[0].transpose(1, 0, 2), n_batch=n_batch, n_experts=n_experts)

    counts = cnt_all[0, :n_experts].astype(I32)
    padded = (counts + MOE_TM - 1) // MOE_TM * MOE_TM
    pends = jnp.cumsum(padded)
    pstarts = (pends - padded).astype(F32)
    n_rows_all = (n_tok + n_s) * TOP_K
    n_blocks = -(-n_rows_all // MOE_TM) + n_experts
    blk_row0 = jnp.arange(n_blocks, dtype=I32) * MOE_TM
    blk_e = jnp.minimum(jnp.sum((pends[None, :] <= blk_row0[:, None]).astype(I32), axis=1), n_experts - 1)
    n_used = (pends[-1] // MOE_TM).astype(I32).reshape(1)
    last_e = blk_e[jnp.maximum(n_used[0] - 1, 0)]
    blk_e = jnp.where(jnp.arange(n_blocks) < n_used[0], blk_e, last_e)

    def dests(meta):
        e = meta[:, :TOP_K].astype(I32)
        return (pstarts[e] + meta[:, 4:4 + TOP_K]).astype(I32).reshape(-1)

    dest_p, dest_s = dests(meta_p), dests(meta_s)
    xbuf = jnp.zeros((n_blocks * MOE_TM, d // LANES, LANES), F32)
    xbuf = _moe_scatter(dest_p, xn5_p, xbuf, tm=_row_tile(n_tok, 512))
    xbuf = _moe_scatter(dest_s, xn5_s, xbuf, tm=_row_tile(n_s, 256))
    ff_e = exp_w1.shape[3]
    tf = next((c for c in (1792, 1024, 512, 256) if ff_e % c == 0), ff_e)
    ybuf = _moe_experts(blk_e, n_used, xbuf, bf(exp_w1[0]), bf(exp_w3[0]), bf(exp_w2[0]), tm=MOE_TM, tf=tf)
    ple1 = (bf(ple_w[1]), bf(ple_gate_w[1]), row2(ple_norm[1]), row2(final_norm))
    y_p = _moe_combine(dest_p, ybuf, h4_p, meta_p, p_prompt[1, 0], *ple1, tm=_row_tile(n_tok, 256))
    y_s = _moe_combine(dest_s, ybuf, h4_s, meta_s, time_major(p_sample[1].reshape(n_s, -1)), *ple1,
                       tm=_row_tile(n_s, 256))

    def batch_major(x):
        return x.reshape(n_new, n_batch, -1).transpose(1, 0, 2)

    kv = lambda x, b, l: x.reshape(1, b, l, n_heads, HEAD_DIM)
    return (y_p[None], batch_major(y_s),
            kv(k_p, 1, n_tok), kv(v_p, 1, n_tok), kv(k_s, n_batch, n_new), kv(v_s, n_batch, n_new),
            va_s.reshape(1, n_batch, n_new, a_width),
            cstate_p[None, None], cstate_s.transpose(1, 0, 2)[None],
            pstate_p[None, None], pstate_s.transpose(1, 0, 2)[None])
```

```python
import functools
import math

import jax
import jax.numpy as jnp
from jax import lax
from jax.experimental import pallas as pl
from jax.experimental.pallas import tpu as pltpu

F32 = jnp.float32
BF16 = jnp.bfloat16
I32 = jnp.int32

EPS = 1e-6
NEG = -1e30
BIG_INDEX = 1e9
LOG2E = 1.4426950408889634
CHUNK = 128
A_GROUPS = 8
HEAD_DIM = 64
MOBA_BLOCK = 256
MOBA_TOPK = 3
ROPE_THETA = 10000.0
POOL_WINDOWS = (2, 4, 8, 16)
TOP_K = 2

LANES = 128
MXU_WIDTH = 256
VMEM_LIMIT_BYTES = 56 * 1024 * 1024
HALO = 32
MOE_TM = 512
MOBA_UNROLL = 4
MOBA_SHARE = 2
KV_PAGES_PER_STEP = 64


def _cparams(sem):
    return pltpu.CompilerParams(dimension_semantics=sem, vmem_limit_bytes=VMEM_LIMIT_BYTES)


def _full(shape):
    n = len(shape)
    return pl.BlockSpec(shape, lambda *a, _n=n: (0,) * _n, pipeline_mode=pl.Buffered(1))


def _whole(shape):
    n = len(shape)
    return pl.BlockSpec(shape, lambda *a, _n=n: (0,) * _n)


def _dot(a, b):
    return jnp.dot(a, b, preferred_element_type=F32)


def _dot_nt(a, b):
    return lax.dot_general(a, b, (((1,), (1,)), ((), ())), preferred_element_type=F32)


def _rms(x, g):
    return x * lax.rsqrt(jnp.mean(x * x, axis=-1, keepdims=True) + EPS) * g


def _layernorm(x, g, b):
    xc = x - jnp.mean(x, axis=-1, keepdims=True)
    var = jnp.mean(xc * xc, axis=-1, keepdims=True)
    return xc * lax.rsqrt(var + EPS) * g + b


def _split_bf16(x):
    hi = x.astype(BF16)
    lo = (x - hi.astype(F32)).astype(BF16)
    return hi, lo


def _lane_pick(x, lane, idx):
    return jnp.sum(jnp.where(lane == idx, x, 0.0), axis=-1, keepdims=True)


def _even_in_kernel(h_ref, g_ref, w_ref, lng_ref, lnb_ref, ws_ref, bias_ref, cos_ref, sin_ref, *outs,
                    seq_l, prompt, a_width, b_width):
    tm = h_ref.shape[0]
    i = pl.program_id(0)
    xn = _rms(h_ref[...], g_ref[...]).astype(BF16)
    aw, bw = a_width, b_width

    u = jax.nn.gelu(_dot(xn, w_ref[:, 0:aw]))
    va = _layernorm(jax.nn.gelu(_dot(xn, w_ref[:, aw:2 * aw])), lng_ref[...], lnb_ref[...])
    vab = va.astype(BF16)

    r = lax.broadcasted_iota(I32, (CHUNK, CHUNK), 0)
    c = lax.broadcasted_iota(I32, (CHUNK, CHUNK), 1)
    sh = int(math.log2(seq_l))
    ok = (lax.shift_right_logical(r, sh) == lax.shift_right_logical(c, sh)) & (c <= r)
    wm = [jnp.where(ok, ws_ref[g], 0.0).astype(BF16) for g in range(A_GROUPS)]
    lane = lax.broadcasted_iota(I32, (CHUNK, LANES), 1)
    gdim = aw // A_GROUPS
    gates = []
    for ci in range(tm // CHUNK):
        vc = vab[ci * CHUNK:(ci + 1) * CHUNK]
        parts = []
        for p in range(aw // LANES):
            vp = vc[:, p * LANES:(p + 1) * LANES]
            r0 = _dot(wm[2 * p], vp)
            r1 = _dot(wm[2 * p + 1], vp)
            parts.append(jnp.where(lane < gdim, r0, r1))
        gates.append(jnp.concatenate(parts, axis=1) + bias_ref[...])
    gate = jnp.concatenate(gates, axis=0)
    a_out = u * gate

    cosv = cos_ref[...]
    sinv = sin_ref[...]
    lane_t = lax.broadcasted_iota(I32, (tm, LANES), 1)
    first_half = (lane_t & (HEAD_DIM - 1)) < HEAD_DIM // 2

    def rope(z):
        parts = []
        for p in range(bw // LANES):
            xs = z[:, p * LANES:(p + 1) * LANES]
            rot = jnp.where(first_half, pltpu.roll(xs, LANES - HEAD_DIM // 2, 1), pltpu.roll(xs, HEAD_DIM // 2, 1))
            parts.append(xs * cosv + rot * sinv)
        return parts

    q_parts = rope(_dot(xn, w_ref[:, 2 * aw:2 * aw + bw]))
    k_parts = rope(_dot(xn, w_ref[:, 2 * aw + bw:2 * aw + 2 * bw]))
    v = _dot(xn, w_ref[:, 2 * aw + 2 * bw:2 * aw + 3 * bw])
    k = jnp.concatenate(k_parts, axis=1)
    q = jnp.concatenate(q_parts, axis=1) * (HEAD_DIM ** -0.5)

    if prompt:
        a_ref, q_ref, k_ref, v_ref, kaug_ref, vaug_ref, kmean_ref = outs
        a_ref[...] = a_out.astype(BF16)
        q_ref[...] = (q * LOG2E).astype(BF16)
        k_ref[...] = k
        v_ref[...] = v
        row = lax.broadcasted_iota(I32, (tm, LANES), 0)
        blk = i * (tm // MOBA_BLOCK) + lax.shift_right_logical(row, int(math.log2(MOBA_BLOCK)))
        lo = lane_t < HEAD_DIM
        hot_hi = jnp.where(blk == lane_t - HEAD_DIM, 1.0, 0.0)
        hot_lo = jnp.where(blk == lane_t, 1.0, 0.0)
        for p in range(bw // LANES):
            kp = k_parts[p]
            vp = v[:, p * LANES:(p + 1) * LANES]
            kaug_ref[:, (2 * p) * LANES:(2 * p + 1) * LANES] = jnp.where(lo, kp, hot_hi).astype(BF16)
            kaug_ref[:, (2 * p + 1) * LANES:(2 * p + 2) * LANES] = jnp.where(lo, hot_lo, kp).astype(BF16)
            vaug_ref[:, (2 * p) * LANES:(2 * p + 1) * LANES] = jnp.where(lo, vp, 1.0).astype(BF16)
            vaug_ref[:, (2 * p + 1) * LANES:(2 * p + 2) * LANES] = jnp.where(lo, 1.0, vp).astype(BF16)
        for b in range(tm // MOBA_BLOCK):
            kmean_ref[b] = jnp.mean(k[b * MOBA_BLOCK:(b + 1) * MOBA_BLOCK], axis=0, keepdims=True)
    else:
        a_ref, va_ref, q_ref, k_ref, v_ref = outs
        a_ref[...] = a_out.astype(BF16)
        va_ref[...] = va
        q_ref[...] = q
        k_ref[...] = k
        v_ref[...] = v


def _even_in(h, norm_g, w_in, ln_g, ln_b, ws, bias, cos_t, sin_t, *, tm, seq_l, prompt, a_width, b_width):
    t, d = h.shape
    nw = w_in.shape[1]
    row = lambda i: (i, 0)
    in_specs = [pl.BlockSpec((tm, d), row), _full((1, d)), _full((d, nw)), _full((1, a_width)), _full((1, a_width)),
                _full(ws.shape), _full(bias.shape), pl.BlockSpec((tm, LANES), row), pl.BlockSpec((tm, LANES), row)]
    if prompt:
        nb = t // MOBA_BLOCK
        out_shape = [jax.ShapeDtypeStruct((t, a_width), BF16), jax.ShapeDtypeStruct((t, b_width), BF16),
                     jax.ShapeDtypeStruct((t, b_width), F32), jax.ShapeDtypeStruct((t, b_width), F32),
                     jax.ShapeDtypeStruct((t, 2 * b_width), BF16), jax.ShapeDtypeStruct((t, 2 * b_width), BF16),
                     jax.ShapeDtypeStruct((nb, 1, b_width), F32)]
        out_specs = [pl.BlockSpec((tm, a_width), row), pl.BlockSpec((tm, b_width), row),
                     pl.BlockSpec((tm, b_width), row), pl.BlockSpec((tm, b_width), row),
                     pl.BlockSpec((tm, 2 * b_width), row), pl.BlockSpec((tm, 2 * b_width), row),
                     pl.BlockSpec((tm // MOBA_BLOCK, 1, b_width), lambda i: (i, 0, 0))]
    else:
        out_shape = [jax.ShapeDtypeStruct((t, a_width), BF16), jax.ShapeDtypeStruct((t, a_width), F32),
                     jax.ShapeDtypeStruct((t, b_width), F32), jax.ShapeDtypeStruct((t, b_width), F32),
                     jax.ShapeDtypeStruct((t, b_width), F32)]
        out_specs = [pl.BlockSpec((tm, a_width), row), pl.BlockSpec((tm, a_width), row),
                     pl.BlockSpec((tm, b_width), row), pl.BlockSpec((tm, b_width), row),
                     pl.BlockSpec((tm, b_width), row)]
    return pl.pallas_call(
        functools.partial(_even_in_kernel, seq_l=seq_l, prompt=prompt, a_width=a_width, b_width=b_width),
        grid=(t // tm,), in_specs=in_specs, out_specs=out_specs, out_shape=out_shape,
        compiler_params=_cparams(("arbitrary",)), name="even_in_prompt" if prompt else "even_in_sample",
    )(h, norm_g, w_in, ln_g, ln_b, ws, bias, cos_t, sin_t)


def _moba_prompt_kernel(q_ref, k0_ref, k1_ref, v0_ref, v1_ref, kmh_ref, kml_ref, o_ref, p_scr, m_scr):
    tq = q_ref.shape[0]
    i = pl.program_id(1)
    q = q_ref[...]
    qf = q.astype(F32)
    lane = lax.broadcasted_iota(I32, (tq, LANES), 1)
    k_refs = (k0_ref, k1_ref)
    v_refs = (v0_ref, v1_ref)
    neg_inf = jnp.float32(-jnp.inf)

    qa, q_own = [], []
    for e in range(2):
        in_blk = (lane >= HEAD_DIM) if e == 0 else (lane < HEAD_DIM)
        blk = lane - HEAD_DIM if e == 0 else lane
        sc = _dot(q, kmh_ref[0, e]) + _dot(q, kml_ref[0, e])
        cand = jnp.where(in_blk & (blk < i), sc, neg_inf)
        sel = jnp.zeros((tq, LANES), jnp.bool_)
        blk_f = blk.astype(F32)
        for _ in range(MOBA_TOPK):
            mx = jnp.max(cand, axis=-1, keepdims=True)
            first = jnp.min(jnp.where(cand == mx, blk_f, BIG_INDEX), axis=-1, keepdims=True)
            pick = (blk_f == first) & in_blk & (mx > neg_inf)
            sel = sel | pick
            cand = jnp.where(pick, neg_inf, cand)
        qa.append(jnp.where(in_blk, jnp.where(sel, 0.0, NEG), qf).astype(BF16))
        q_own.append(jnp.where(in_blk, 0.0, qf).astype(BF16))

    n_blk = k0_ref.shape[0] // MOBA_BLOCK

    def offset(blk_idx):
        return pl.multiple_of(jnp.minimum(blk_idx, n_blk - 1) * MOBA_BLOCK, MOBA_BLOCK)

    def probs(e, qe, blk_idx, mask=None):
        s = _dot_nt(qe, k_refs[e][pl.ds(offset(blk_idx), MOBA_BLOCK), :])
        if mask is not None:
            s = jnp.where(mask, s, NEG)
        m = jnp.max(s, axis=-1, keepdims=True)
        return m, jnp.exp2((s - m).astype(BF16))

    def values(e, p, blk_idx):
        return _dot(p, v_refs[e][pl.ds(offset(blk_idx), MOBA_BLOCK), :])

    def stash(g):
        slot = g & 1
        for e in range(2):
            for u0 in range(0, MOBA_UNROLL, MOBA_SHARE):
                s_us = [_dot_nt(qa[e], k_refs[e][pl.ds(offset(g * MOBA_UNROLL + u0 + x), MOBA_BLOCK), :])
                        for x in range(MOBA_SHARE)]
                top = s_us[0]
                for s_u in s_us[1:]:
                    top = jnp.maximum(top, s_u)
                m_u = jnp.max(top, axis=-1, keepdims=True)
                for x, s_u in enumerate(s_us):
                    p_scr[slot, e * MOBA_UNROLL + u0 + x] = jnp.exp2((s_u - m_u).astype(BF16))
                m_scr[slot, e * MOBA_UNROLL + u0] = m_u

    def merge(g, ms, accs):
        slot = g & 1
        new_m, new_acc = [], []
        for e in range(2):
            m_us = [m_scr[slot, e * MOBA_UNROLL + u0] for u0 in range(0, MOBA_UNROLL, MOBA_SHARE)]
            m_new = ms[e]
            for m_u in m_us:
                m_new = jnp.maximum(m_new, m_u)
            acc = jnp.exp2(ms[e] - m_new) * accs[e]
            for j, m_u in enumerate(m_us):
                u0 = j * MOBA_SHARE
                o_u = values(e, p_scr[slot, e * MOBA_UNROLL + u0], g * MOBA_UNROLL + u0)
                for x in range(1, MOBA_SHARE):
                    o_u = o_u + values(e, p_scr[slot, e * MOBA_UNROLL + u0 + x], g * MOBA_UNROLL + u0 + x)
                acc = acc + jnp.exp2(m_u - m_new) * o_u
            new_m.append(m_new)
            new_acc.append(acc)
        return tuple(new_m), tuple(new_acc)

    r = lax.broadcasted_iota(I32, (tq, MOBA_BLOCK), 0)
    c = lax.broadcasted_iota(I32, (tq, MOBA_BLOCK), 1)
    own = [probs(e, q_own[e], i, c <= r) for e in range(2)]
    ms = tuple(m for m, _ in own)
    accs = tuple(values(e, own[e][1], i) for e in range(2))

    def body(g, carry):
        ms, accs = merge(g - 1, *carry)
        stash(g)
        return ms, accs

    n_groups = lax.shift_right_logical(i + (MOBA_UNROLL - 1), int(math.log2(MOBA_UNROLL)))
    stash(0)
    ms, accs = lax.fori_loop(1, n_groups, body, (ms, accs))
    _, accs = merge(jnp.maximum(n_groups - 1, 0), ms, accs)
    outs = [a / pltpu.roll(a, HEAD_DIM, 1) for a in accs]
    o_ref[...] = jnp.where(lane < HEAD_DIM, outs[0], outs[1]).astype(o_ref.dtype)


def _moba_prompt(q, kaug, vaug, km_hi, km_lo):
    t, bw = q.shape
    n_pairs = bw // LANES
    tq = MOBA_BLOCK
    col = lambda e: (lambda p, i, _e=e: (0, 2 * p + _e))
    in_specs = [pl.BlockSpec((tq, LANES), lambda p, i: (i, p)),
                pl.BlockSpec((t, LANES), col(0)), pl.BlockSpec((t, LANES), col(1)),
                pl.BlockSpec((t, LANES), col(0)), pl.BlockSpec((t, LANES), col(1)),
                pl.BlockSpec((1, 2, LANES, LANES), lambda p, i: (p, 0, 0, 0)),
                pl.BlockSpec((1, 2, LANES, LANES), lambda p, i: (p, 0, 0, 0))]
    return pl.pallas_call(
        _moba_prompt_kernel, grid=(n_pairs, t // tq), in_specs=in_specs,
        out_specs=pl.BlockSpec((tq, LANES), lambda p, i: (i, p)),
        out_shape=jax.ShapeDtypeStruct((t, bw), BF16),
        scratch_shapes=[pltpu.VMEM((2, 2 * MOBA_UNROLL, tq, MOBA_BLOCK), BF16),
                        pltpu.VMEM((2, 2 * MOBA_UNROLL, tq, 1), F32)],
        compiler_params=_cparams(("arbitrary", "arbitrary")), name="moba_prompt",
    )(q, kaug, kaug, vaug, vaug, km_hi, km_lo)


def _moba_sample_keys_kernel(pt_ref, *refs, n_steps):
    npg = KV_PAGES_PER_STEP
    pages = refs[:npg]
    qbd_ref, knew_ref, p_ref, pown_ref, linv_ref, s_scr, kmean_scr, bmax_scr = refs[npg:]
    c = pl.program_id(1)
    hw, page = pages[0].shape[1], pages[0].shape[2]
    ppb = MOBA_BLOCK // page
    bps = npg // ppb
    qbd = qbd_ref[0]
    n_rows = qbd.shape[0]
    n_blocks = n_steps * bps
    lane = lax.broadcasted_iota(I32, (n_rows, LANES), 1)
    neg_inf = jnp.float32(-jnp.inf)

    @pl.when(c == 0)
    def _():
        bmax_scr[...] = jnp.full((n_rows, LANES), neg_inf, F32)
        kmean_scr[...] = jnp.zeros_like(kmean_scr)

    lane_k = lax.broadcasted_iota(I32, (hw, LANES), 1)
    for j in range(bps):
        kb = jnp.concatenate([pages[j * ppb + x][0] for x in range(ppb)], axis=1)
        blk = c * bps + j
        kmean_scr[...] = jnp.where(lane_k == blk, jnp.mean(kb, axis=1, keepdims=True), kmean_scr[...])
        s = _dot(qbd, kb.astype(BF16))
        s_scr[:, pl.ds(pl.multiple_of(blk * MOBA_BLOCK, MOBA_BLOCK), MOBA_BLOCK)] = s
        bmax_scr[...] = jnp.where(lane == blk, jnp.max(s, axis=-1, keepdims=True), bmax_scr[...])

    @pl.when(c == n_steps - 1)
    def _():
        kmh, kml = _split_bf16(kmean_scr[...])
        sc = _dot(qbd, kmh) + _dot(qbd, kml)
        cand = jnp.where(lane < n_blocks, sc, neg_inf)
        sel = jnp.zeros((n_rows, LANES), jnp.bool_)
        for _ in range(min(MOBA_TOPK, n_blocks)):
            mx = jnp.max(cand, axis=-1, keepdims=True)
            first = jnp.min(jnp.where(cand == mx, lane, jnp.int32(2 ** 30)), axis=-1, keepdims=True)
            pick = lane == first
            sel = sel | pick
            cand = jnp.where(pick, neg_inf, cand)
        so = _dot_nt(qbd, knew_ref[0].astype(BF16))
        n_new = so.shape[1]
        rr = lax.broadcasted_iota(I32, (n_rows, n_new), 0)
        ss = lax.broadcasted_iota(I32, (n_rows, n_new), 1)
        so = jnp.where(ss <= (rr & (n_new - 1)), so, NEG)
        m = jnp.maximum(jnp.max(jnp.where(sel, bmax_scr[...], neg_inf), axis=-1, keepdims=True),
                        jnp.max(so, axis=-1, keepdims=True))
        po = jnp.exp(so - m)
        l = jnp.sum(po, axis=-1, keepdims=True)
        self32 = sel.astype(F32)
        for b in range(n_blocks):
            on = jnp.sum(jnp.where(lane == b, self32, 0.0), axis=-1, keepdims=True)
            sb = s_scr[:, b * MOBA_BLOCK:(b + 1) * MOBA_BLOCK]
            pb = (jnp.exp(sb - m) * on).astype(BF16)
            l = l + jnp.sum(pb.astype(F32), axis=-1, keepdims=True)
            p_ref[0, :, b * MOBA_BLOCK:(b + 1) * MOBA_BLOCK] = pb
        pown_ref[0] = jnp.concatenate([po, jnp.zeros((n_rows, LANES - n_new), F32)], axis=1)
        linv_ref[0] = jnp.broadcast_to(1.0 / l, (n_rows, LANES))


def _moba_sample_values_kernel(pt_ref, *refs, n_steps, n_heads):
    npg = KV_PAGES_PER_STEP
    pages = refs[:npg]
    p_ref, pown_ref, linv_ref, vnew_ref, o_ref, acc_scr = refs[npg:]
    c = pl.program_id(1)
    vb = jnp.concatenate([pg[0] for pg in pages], axis=1).astype(BF16)
    part = _dot_nt(p_ref[0], vb)

    @pl.when(c == 0)
    def _():
        acc_scr[...] = part

    @pl.when(c > 0)
    def _():
        acc_scr[...] += part

    @pl.when(c == n_steps - 1)
    def _():
        vnew = vnew_ref[0]
        n_new = vnew.shape[0]
        o = (acc_scr[...] + _dot(pown_ref[0][:, :n_new].astype(BF16), vnew.astype(BF16))) * linv_ref[0][:, :1]
        n_rows, hw = o.shape
        rr = lax.broadcasted_iota(I32, (n_rows, hw), 0)
        ll = lax.broadcasted_iota(I32, (n_rows, hw), 1)
        keep = (lax.shift_right_logical(rr, int(math.log2(n_new)))
                == lax.shift_right_logical(ll, int(math.log2(HEAD_DIM))))
        o = jnp.where(keep, o, 0.0).reshape(n_heads, n_new, hw)
        o_ref[0] = jnp.sum(o, axis=0).astype(o_ref.dtype)


def _moba_sample(q, k, v, cache_k, cache_v, page_table, n_batch):
    hw = q.shape[1]
    n_heads = hw // HEAD_DIM
    n_new = q.shape[0] // n_batch
    n_phys, page = cache_k.shape[0], cache_k.shape[1]
    n_pages = page_table.shape[1]
    npg = KV_PAGES_PER_STEP
    assert n_pages % npg == 0 and (n_pages * page) % MOBA_BLOCK == 0 and MOBA_BLOCK % page == 0
    n_steps = n_pages // npg
    past = n_pages * page
    n_blocks = past // MOBA_BLOCK
    assert n_blocks <= LANES and n_new & (n_new - 1) == 0
    n_rows = n_heads * n_new
    ck = cache_k.transpose(0, 2, 3, 1).reshape(n_phys, hw, page)
    cv = cache_v.transpose(0, 2, 3, 1).reshape(n_phys, hw, page)
    pt = page_table.reshape(-1).astype(I32)
    q3 = q.reshape(n_batch, n_new, n_heads, HEAD_DIM)
    eye = jnp.eye(n_heads, dtype=F32)
    qbd = (q3.transpose(0, 2, 1, 3)[:, :, :, None, :] * eye[None, :, None, :, None]).reshape(n_batch, n_rows, hw)
    qbd = qbd.astype(BF16)
    k3 = k.reshape(n_batch, n_new, hw)
    v3 = v.reshape(n_batch, n_new, hw)

    def page_spec(s):
        return pl.BlockSpec((1, hw, page), lambda b, c, pt_ref, _s=s: (pt_ref[b * n_pages + c * npg + _s], 0, 0))

    per_b = lambda shape: pl.BlockSpec(shape, lambda b, c, pt_ref: (b,) + (0,) * (len(shape) - 1))
    p, pown, linv = pl.pallas_call(
        functools.partial(_moba_sample_keys_kernel, n_steps=n_steps),
        grid_spec=pltpu.PrefetchScalarGridSpec(
            num_scalar_prefetch=1, grid=(n_batch, n_steps),
            in_specs=[page_spec(s) for s in range(npg)] + [per_b((1, n_rows, hw)), per_b((1, n_new, hw))],
            out_specs=[per_b((1, n_rows, past)), per_b((1, n_rows, LANES)), per_b((1, n_rows, LANES))],
            scratch_shapes=[pltpu.VMEM((n_rows, past), F32), pltpu.VMEM((hw, LANES), F32),
                            pltpu.VMEM((n_rows, LANES), F32)]),
        out_shape=[jax.ShapeDtypeStruct((n_batch, n_rows, past), BF16),
                   jax.ShapeDtypeStruct((n_batch, n_rows, LANES), F32),
                   jax.ShapeDtypeStruct((n_batch, n_rows, LANES), F32)],
        compiler_params=_cparams(("arbitrary", "arbitrary")), name="moba_sample_keys",
    )(pt, *([ck] * npg), qbd, k3)

    out = pl.pallas_call(
        functools.partial(_moba_sample_values_kernel, n_steps=n_steps, n_heads=n_heads),
        grid_spec=pltpu.PrefetchScalarGridSpec(
            num_scalar_prefetch=1, grid=(n_batch, n_steps),
            in_specs=[page_spec(s) for s in range(npg)]
            + [pl.BlockSpec((1, n_rows, npg * page), lambda b, c, pt_ref: (b, 0, c)),
               per_b((1, n_rows, LANES)), per_b((1, n_rows, LANES)), per_b((1, n_new, hw))],
            out_specs=per_b((1, n_new, hw)),
            scratch_shapes=[pltpu.VMEM((n_rows, hw), F32)]),
        out_shape=jax.ShapeDtypeStruct((n_batch, n_new, hw), BF16),
        compiler_params=_cparams(("arbitrary", "arbitrary")), name="moba_sample_values",
    )(pt, *([cv] * npg), p, pown, linv, v3)
    return out.reshape(n_batch * n_new, hw)


def _ple(h, p_ref, wp_ref, wg_ref, pn_ref):
    gate = jax.nn.sigmoid(_dot(_rms(h, pn_ref[...]).astype(BF16), wg_ref[...]))
    return h + _dot(p_ref[...].astype(BF16), wp_ref[...]) * gate


def _even_tail_kernel(h_ref, a_ref, b_ref, wo_ref, nf_ref, w1_ref, w3_ref, w2_ref, p_ref, wp_ref, wg_ref, pn_ref,
                      nn_ref, h_out_ref, xn_out_ref, *, ff_chunks):
    aw = a_ref.shape[1]
    h1 = h_ref[...] + _dot(a_ref[...], wo_ref[0:aw, :]) + _dot(b_ref[...], wo_ref[aw:, :])
    xn = _rms(h1, nf_ref[...]).astype(BF16)
    ff = w1_ref.shape[1]
    cw = ff // ff_chunks
    hid = []
    for ci in range(ff_chunks):
        sl = slice(ci * cw, (ci + 1) * cw)
        hid.append((jax.nn.silu(_dot(xn, w1_ref[:, sl])) * _dot(xn, w3_ref[:, sl])).astype(BF16))
    acc = h1 + _dot(jnp.concatenate(hid, axis=1), w2_ref[...])
    h3 = _ple(acc, p_ref, wp_ref, wg_ref, pn_ref)
    h_out_ref[...] = h3
    xn_out_ref[...] = _rms(h3, nn_ref[...]).astype(BF16)


def _even_tail(h, a, b, w_out, norm_ffn, w1, w3, w2, p, wp, wg, pn, next_norm, *, tm):
    t, d = h.shape
    ff = w1.shape[1]
    ff_chunks = max((n for n in range(1, ff // MXU_WIDTH + 1) if ff % (n * MXU_WIDTH) == 0 and ff // n >= 1024),
                    default=1)
    row = lambda i: (i, 0)
    in_specs = [pl.BlockSpec((tm, d), row), pl.BlockSpec((tm, a.shape[1]), row), pl.BlockSpec((tm, b.shape[1]), row),
                _full(w_out.shape), _full((1, d)), _full(w1.shape), _full(w3.shape), _full(w2.shape),
                pl.BlockSpec((tm, p.shape[1]), row), _full(wp.shape), _full(wg.shape), _full((1, d)), _full((1, d))]
    return pl.pallas_call(
        functools.partial(_even_tail_kernel, ff_chunks=ff_chunks),
        grid=(t // tm,), in_specs=in_specs,
        out_specs=[pl.BlockSpec((tm, d), row), pl.BlockSpec((tm, d), row)],
        out_shape=[jax.ShapeDtypeStruct((t, d), F32), jax.ShapeDtypeStruct((t, d), BF16)],
        compiler_params=_cparams(("arbitrary",)), name="even_tail",
    )(h, a, b, w_out, norm_ffn, w1, w3, w2, p, wp, wg, pn, next_norm)


def _odd_tail(conv, d_in, h3, refs, count_scr, n_experts):
    (cb_ref, clg_ref, clb_ref, pw_ref, ps_ref, wo_ref, nf_ref, rwh_ref, rwl_ref, rb_ref) = refs
    rows = conv.shape[0]
    cw = conv.shape[1]
    cact = jax.nn.silu(_layernorm(conv + cb_ref[...], clg_ref[...], clb_ref[...]))
    db = d_in.astype(BF16)
    gdim = d_in.shape[1] // len(POOL_WINDOWS)
    dparts = [_dot(db[:, g * gdim:(g + 1) * gdim], pw_ref[g]) for g in range(len(POOL_WINDOWS))]
    dmix = jnp.concatenate(dparts, axis=1) * ps_ref[...]
    h4 = h3 + _dot(cact.astype(BF16), wo_ref[0:cw, :]) + _dot(dmix.astype(BF16), wo_ref[cw:, :])
    xn = _rms(h4, nf_ref[...])
    xh, xl = _split_bf16(xn)
    logits = _dot(xh, rwh_ref[...]) + _dot(xl, rwh_ref[...]) + _dot(xh, rwl_ref[...]) + rb_ref[...]
    lane = lax.broadcasted_iota(I32, (rows, LANES), 1)
    neg_inf = jnp.float32(-jnp.inf)
    cand = jnp.where(lane < n_experts, logits, neg_inf)
    lane_f = lane.astype(F32)
    v1 = jnp.max(cand, axis=-1, keepdims=True)
    e1 = jnp.min(jnp.where(cand == v1, lane_f, BIG_INDEX), axis=-1, keepdims=True)
    cand2 = jnp.where(lane_f == e1, neg_inf, cand)
    v2 = jnp.max(cand2, axis=-1, keepdims=True)
    e2 = jnp.min(jnp.where(cand2 == v2, lane_f, BIG_INDEX), axis=-1, keepdims=True)
    ex = jnp.exp(v2 - v1)
    g1 = 1.0 / (1.0 + ex)
    g2 = ex / (1.0 + ex)
    chosen = jnp.where((lane_f == e1) | (lane_f == e2), 1.0, 0.0).astype(BF16)
    rr = lax.broadcasted_iota(I32, (rows, rows), 0)
    cc = lax.broadcasted_iota(I32, (rows, rows), 1)
    before = jnp.where(cc < rr, 1.0, 0.0).astype(BF16)
    prefix = _dot(before, chosen) + count_scr[...]
    r1 = _lane_pick(prefix, lane_f, e1)
    r2 = _lane_pick(prefix, lane_f, e2)
    count_scr[...] = count_scr[...] + jnp.sum(chosen.astype(F32), axis=0, keepdims=True)
    meta = jnp.zeros((rows, LANES), F32)
    for idx, val in enumerate((e1, e2, g1, g2, r1, r2)):
        meta = jnp.where(lane == idx, val, meta)
    return h4, xn, meta


def _odd_prompt_kernel(xm_ref, xh_ref, h3_ref, wi_ref, cw_ref, *refs, n_experts, conv_hist, pool_hist):
    tail_refs = refs[:10]
    cnt_in_ref, h4_ref, xn_ref, meta_ref, cnt_ref, cstate_ref, pstate_ref, count_scr = refs[10:]
    i = pl.program_id(0)
    tm = xm_ref.shape[0]
    cwid = cw_ref.shape[1]

    @pl.when(i == 0)
    def _():
        count_scr[...] = cnt_in_ref[...]

    x = jnp.concatenate([xh_ref[...], xm_ref[...]], axis=0)
    z = _dot(x, wi_ref[...])
    row = lax.broadcasted_iota(I32, (HALO + tm, 1), 0)
    z = jnp.where((row >= HALO) | (i > 0), z, 0.0)
    c_in = z[:, :cwid] * jax.nn.sigmoid(z[:, cwid:2 * cwid])
    dx = z[:, 2 * cwid:]

    cwv = cw_ref[...]
    width = cwv.shape[0]
    lead = HALO - (width - 1)

    def tap(kp):
        k = kp - lead
        return cwv[k:k + 1, :] if 0 <= k < width else None

    ext = c_in
    conv = None
    for b in range(8):
        n_a = (HALO // 8 + 1) if b == 0 else HALO // 8
        rows_b = tm if b == 0 else tm + 8
        ub = None
        for a in range(n_a):
            w = tap(8 * a + b)
            if w is None:
                continue
            term = ext[8 * a:8 * a + rows_b] * w
            ub = term if ub is None else ub + term
        if ub is None:
            continue
        piece = ub if b == 0 else pltpu.roll(ub, rows_b - b, 0)[:tm]
        conv = piece if conv is None else conv + piece

    s = dx
    sums = {}
    span = 1
    while span < max(POOL_WINDOWS):
        s = s + pltpu.roll(s, span, 0)
        span *= 2
        sums[span] = s
    gdim = dx.shape[1] // len(POOL_WINDOWS)
    pos = i * tm + lax.broadcasted_iota(I32, (tm, 1), 0)
    pooled = []
    for g, w in enumerate(POOL_WINDOWS):
        win = sums[w][HALO:, g * gdim:(g + 1) * gdim]
        cnt = jnp.minimum(pos + 1, w).astype(F32)
        pooled.append(win / cnt)
    dmain = dx[HALO:]
    d_in = jnp.concatenate(pooled, axis=1) - dmain

    h4, xn, meta = _odd_tail(conv, d_in, h3_ref[...], tail_refs, count_scr, n_experts)
    h4_ref[...] = h4
    _to_row_tiles(xn_ref, xn)
    meta_ref[...] = meta
    cnt_ref[...] = count_scr[...]
    ctail = c_in[tm:]
    cstate_ref[...] = pltpu.roll(ctail, conv_hist, 0)[:conv_hist] if conv_hist < HALO else ctail
    ptail_rows = 8 * (-(-pool_hist // 8))
    ptail = dmain[tm - ptail_rows:]
    pstate_ref[...] = pltpu.roll(ptail, pool_hist, 0)[:pool_hist] if pool_hist < ptail_rows else ptail


def _odd_sample_kernel(x_ref, h3_ref, wi_ref, cw_ref, *refs, n_experts, n_batch):
    tail_refs = refs[:10]
    cnt_in_ref, chist_ref, phist_ref, h4_ref, xn_ref, meta_ref, cnt_ref, cstate_ref, pstate_ref, count_scr = refs[10:]
    rows = x_ref.shape[0]
    n_new = rows // n_batch
    cwid = cw_ref.shape[1]
    count_scr[...] = cnt_in_ref[...]
    z = _dot(x_ref[...], wi_ref[...])
    c_in = z[:, :cwid] * jax.nn.sigmoid(z[:, cwid:2 * cwid])
    dx = z[:, 2 * cwid:]
    cwv = cw_ref[...]
    width = cwv.shape[0]
    hist_c = chist_ref[...]
    xp = [hist_c[j] for j in range(width - 1)] + [c_in[t * n_batch:(t + 1) * n_batch] for t in range(n_new)]
    conv = []
    for t in range(n_new):
        acc = xp[t] * cwv[0:1, :]
        for k in range(1, width):
            acc = acc + xp[t + k] * cwv[k:k + 1, :]
        conv.append(acc)
    conv = jnp.concatenate(conv, axis=0)
    cstate_ref[...] = jnp.stack(xp[n_new:], axis=0)

    hist_p = phist_ref[...]
    n_ph = hist_p.shape[0]
    xq = [hist_p[j] for j in range(n_ph)] + [dx[t * n_batch:(t + 1) * n_batch] for t in range(n_new)]
    gdim = dx.shape[1] // len(POOL_WINDOWS)
    lane = lax.broadcasted_iota(I32, (n_batch, dx.shape[1]), 1)
    d_in = []
    for t in range(n_new):
        run = xq[n_ph + t]
        tot = jnp.zeros_like(run)
        done = 1
        for g, w in enumerate(POOL_WINDOWS):
            for j in range(done, w):
                run = run + xq[n_ph + t - j]
            done = w
            in_g = (lane >= g * gdim) & (lane < (g + 1) * gdim)
            tot = jnp.where(in_g, run / float(w), tot)
        d_in.append(tot - xq[n_ph + t])
    d_in = jnp.concatenate(d_in, axis=0)
    pstate_ref[...] = jnp.stack(xq[n_new:], axis=0)

    h4, xn, meta = _odd_tail(conv, d_in, h3_ref[...], tail_refs, count_scr, n_experts)
    h4_ref[...] = h4
    _to_row_tiles(xn_ref, xn)
    meta_ref[...] = meta
    cnt_ref[...] = count_scr[...]


def _odd_common_args(conv_b, c_ln_g, c_ln_b, pool_w, pool_scale, w_out, norm_ffn, rw_hi, rw_lo, rb):
    args = (conv_b, c_ln_g, c_ln_b, pool_w, pool_scale, w_out, norm_ffn, rw_hi, rw_lo, rb)
    return args, [_full(a.shape) for a in args]


def _odd_prompt(xn3, h3, w_in, conv_w, common, counts_in, *, tm, n_experts):
    t, d = h3.shape
    cwid = conv_w.shape[1]
    conv_hist = conv_w.shape[0] - 1
    pool_hist = max(POOL_WINDOWS) - 1
    args, specs = common
    row = lambda i: (i, 0)
    hpb = tm // HALO
    in_specs = [pl.BlockSpec((tm, d), row), pl.BlockSpec((HALO, d), lambda i: (jnp.maximum(i * hpb - 1, 0), 0)),
                pl.BlockSpec((tm, d), row), _full(w_in.shape), _full(conv_w.shape)] + specs + [_full((1, LANES))]
    dwid = w_in.shape[1] - 2 * cwid
    return pl.pallas_call(
        functools.partial(_odd_prompt_kernel, n_experts=n_experts, conv_hist=conv_hist, pool_hist=pool_hist),
        grid=(t // tm,), in_specs=in_specs,
        out_specs=[pl.BlockSpec((tm, d), row), pl.BlockSpec((tm, d // LANES, LANES), lambda i: (i, 0, 0)),
                   pl.BlockSpec((tm, LANES), row),
                   _whole((1, LANES)), _whole((conv_hist, cwid)), _whole((pool_hist, dwid))],
        out_shape=[jax.ShapeDtypeStruct((t, d), F32), jax.ShapeDtypeStruct((t, d // LANES, LANES), F32),
                   jax.ShapeDtypeStruct((t, LANES), F32), jax.ShapeDtypeStruct((1, LANES), F32),
                   jax.ShapeDtypeStruct((conv_hist, cwid), F32), jax.ShapeDtypeStruct((pool_hist, dwid), F32)],
        scratch_shapes=[pltpu.VMEM((1, LANES), F32)],
        compiler_params=_cparams(("arbitrary",)), name="odd_prompt",
    )(xn3, xn3, h3, w_in, conv_w, *args, counts_in)


def _odd_sample(xn3, h3, w_in, conv_w, common, counts_in, conv_hist_t, pool_hist_t, *, n_batch, n_experts):
    t, d = h3.shape
    cwid = conv_w.shape[1]
    dwid = w_in.shape[1] - 2 * cwid
    args, specs = common
    in_specs = [_full((t, d)), _full((t, d)), _full(w_in.shape), _full(conv_w.shape)] + specs + \
               [_full((1, LANES)), _full(conv_hist_t.shape), _full(pool_hist_t.shape)]
    return pl.pallas_call(
        functools.partial(_odd_sample_kernel, n_experts=n_experts, n_batch=n_batch),
        grid=(1,), in_specs=in_specs,
        out_specs=[_whole((t, d)), _whole((t, d // LANES, LANES)), _whole((t, LANES)), _whole((1, LANES)),
                   _whole(conv_hist_t.shape), _whole(pool_hist_t.shape)],
        out_shape=[jax.ShapeDtypeStruct((t, d), F32), jax.ShapeDtypeStruct((t, d // LANES, LANES), F32),
                   jax.ShapeDtypeStruct((t, LANES), F32), jax.ShapeDtypeStruct((1, LANES), F32),
                   jax.ShapeDtypeStruct(conv_hist_t.shape, F32), jax.ShapeDtypeStruct(pool_hist_t.shape, F32)],
        scratch_shapes=[pltpu.VMEM((1, LANES), F32)],
        compiler_params=_cparams(("arbitrary",)), name="odd_sample",
    )(xn3, h3, w_in, conv_w, *args, counts_in, conv_hist_t, pool_hist_t)


def _row_copy(src, s, dst, d, sem):
    return pltpu.make_async_copy(src.at[pl.ds(s, 1)], dst.at[pl.ds(d, 1)], sem)


def _rows_wait(ref, n, sem):
    pltpu.make_async_copy(ref.at[pl.ds(0, n)], ref.at[pl.ds(0, n)], sem).wait()


def _to_row_tiles(ref, x):
    rows, ns, _ = ref.shape
    y = jnp.stack([x[:, s * LANES:(s + 1) * LANES].reshape(rows // 8, 8, LANES) for s in range(ns)], axis=1)
    ref[...] = jnp.swapaxes(y, 1, 2).reshape(rows, ns, LANES)


def _from_row_tiles(ref):
    rows, ns, _ = ref.shape
    y = jnp.swapaxes(ref[...].reshape(rows // 8, 8, ns, LANES), 1, 2)
    return jnp.concatenate([y[:, s].reshape(rows, LANES) for s in range(ns)], axis=1)


def _moe_scatter_kernel(dest_ref, x_ref, buf_in_hbm, buf_hbm, sem, *, tm):
    del buf_in_hbm
    base = pl.program_id(0) * tm

    def start(r, carry):
        for k in range(TOP_K):
            _row_copy(x_ref, r, buf_hbm, dest_ref[(base + r) * TOP_K + k], sem).start()
        return carry

    lax.fori_loop(0, tm, start, 0)
    _rows_wait(buf_hbm, tm * TOP_K, sem)


def _moe_scatter(dest, x, buf, *, tm):
    t = x.shape[0]
    any_spec = pl.BlockSpec(memory_space=pl.ANY)
    return pl.pallas_call(
        functools.partial(_moe_scatter_kernel, tm=tm),
        grid_spec=pltpu.PrefetchScalarGridSpec(
            num_scalar_prefetch=1, grid=(t // tm,),
            in_specs=[pl.BlockSpec((tm,) + x.shape[1:], lambda i, dref: (i, 0, 0)), any_spec], out_specs=any_spec,
            scratch_shapes=[pltpu.SemaphoreType.DMA(())]),
        out_shape=jax.ShapeDtypeStruct(buf.shape, buf.dtype),
        input_output_aliases={2: 0},
        compiler_params=_cparams(("arbitrary",)), name="moe_scatter",
    )(dest, x, buf)


def _moe_experts_kernel(be_ref, nu_ref, x_ref, w1_ref, w3_ref, w2_ref, y_ref, xb_scr, acc_scr, *, nj):
    i = pl.program_id(0)
    j = pl.program_id(1)
    used = i < nu_ref[0]

    @pl.when(used)
    def _():
        @pl.when(j == 0)
        def _():
            xb_scr[...] = _from_row_tiles(x_ref).astype(BF16)

        xb = xb_scr[...]
        hid = jax.nn.silu(_dot(xb, w1_ref[0])) * _dot(xb, w3_ref[0])
        part = _dot(hid.astype(BF16), w2_ref[0])

        @pl.when(j == 0)
        def _():
            acc_scr[...] = part

        @pl.when(j > 0)
        def _():
            acc_scr[...] += part

        @pl.when(j == nj - 1)
        def _():
            _to_row_tiles(y_ref, acc_scr[...])

    @pl.when(jnp.logical_not(used) & (j == nj - 1))
    def _():
        y_ref[...] = jnp.zeros_like(y_ref)


def _moe_experts(blk_e, n_used, xbuf, w1, w3, w2, *, tm, tf):
    rows, ns, _ = xbuf.shape
    d = ns * LANES
    ff = w1.shape[2]
    nj = ff // tf
    nb = rows // tm
    live_j = lambda i, j, nu: jnp.where(i < nu[0], j, nj - 1)
    return pl.pallas_call(
        functools.partial(_moe_experts_kernel, nj=nj),
        grid_spec=pltpu.PrefetchScalarGridSpec(
            num_scalar_prefetch=2, grid=(nb, nj),
            in_specs=[pl.BlockSpec((tm, ns, LANES), lambda i, j, be, nu: (i, 0, 0)),
                      pl.BlockSpec((1, d, tf), lambda i, j, be, nu: (be[i], 0, live_j(i, j, nu))),
                      pl.BlockSpec((1, d, tf), lambda i, j, be, nu: (be[i], 0, live_j(i, j, nu))),
                      pl.BlockSpec((1, tf, d), lambda i, j, be, nu: (be[i], live_j(i, j, nu), 0))],
            out_specs=pl.BlockSpec((tm, ns, LANES), lambda i, j, be, nu: (i, 0, 0)),
            scratch_shapes=[pltpu.VMEM((tm, d), BF16), pltpu.VMEM((tm, d), F32)]),
        out_shape=jax.ShapeDtypeStruct((rows, ns, LANES), F32),
        compiler_params=_cparams(("arbitrary", "arbitrary")), name="moe_experts",
    )(blk_e, n_used, xbuf, w1, w3, w2)


def _moe_combine_kernel(dest_ref, y_hbm, h4_ref, meta_ref, p_ref, wp_ref, wg_ref, pn_ref, fn_ref, o_ref,
                        y0_scr, y1_scr, sem, *, tm):
    base = pl.program_id(0) * tm
    bufs = (y0_scr, y1_scr)

    def start(r, carry):
        for k in range(TOP_K):
            _row_copy(y_hbm, dest_ref[(base + r) * TOP_K + k], bufs[k], r, sem).start()
        return carry

    lax.fori_loop(0, tm, start, 0)
    for k in range(TOP_K):
        _rows_wait(bufs[k], tm, sem)
    meta = meta_ref[...]
    lane = lax.broadcasted_iota(I32, meta.shape, 1)
    g1 = jnp.sum(jnp.where(lane == 2, meta, 0.0), axis=-1, keepdims=True)
    g2 = jnp.sum(jnp.where(lane == 3, meta, 0.0), axis=-1, keepdims=True)
    h5 = h4_ref[...] + _from_row_tiles(y0_scr) * g1 + _from_row_tiles(y1_scr) * g2
    h6 = _ple(h5, p_ref, wp_ref, wg_ref, pn_ref)
    o_ref[...] = _rms(h6, fn_ref[...])


def _moe_combine(dest, ybuf, h4, meta, p, wp, wg, pn, fn, *, tm):
    t, d = h4.shape
    row = lambda i, dref: (i, 0)
    full = lambda shape: pl.BlockSpec(shape, lambda i, dref, _n=len(shape): (0,) * _n)
    return pl.pallas_call(
        functools.partial(_moe_combine_kernel, tm=tm),
        grid_spec=pltpu.PrefetchScalarGridSpec(
            num_scalar_prefetch=1, grid=(t // tm,),
            in_specs=[pl.BlockSpec(memory_space=pl.ANY), pl.BlockSpec((tm, d), row), pl.BlockSpec((tm, LANES), row),
                      pl.BlockSpec((tm, p.shape[1]), row), full(wp.shape), full(wg.shape), full((1, d)), full((1, d))],
            out_specs=pl.BlockSpec((tm, d), row),
            scratch_shapes=[pltpu.VMEM((tm, d // LANES, LANES), F32), pltpu.VMEM((tm, d // LANES, LANES), F32),
                            pltpu.SemaphoreType.DMA(())]),
        out_shape=jax.ShapeDtypeStruct((t, d), F32),
        compiler_params=_cparams(("arbitrary",)), name="moe_combine",
    )(dest, ybuf, h4, meta, p, wp, wg, pn, fn)


def _rope_tables(pos):
    half = HEAD_DIM // 2
    inv = ROPE_THETA ** (-jnp.arange(half, dtype=F32) * (2.0 / HEAD_DIM))
    ang = pos.astype(F32)[:, None] * inv[None, :]
    cos, sin = jnp.cos(ang), jnp.sin(ang)
    reps = LANES // HEAD_DIM
    cos_t = jnp.tile(jnp.concatenate([cos, cos], axis=1), (1, reps))
    sin_t = jnp.tile(jnp.concatenate([-sin, sin], axis=1), (1, reps))
    return cos_t, sin_t


def _row_tile(t, pref):
    for tm in (pref, 256, 128, 64, 32, 16, 8):
        if tm <= t and t % tm == 0:
            return tm
    return t


def kernel(x_prompt, x_sample, cache_k, cache_v, state_conv, state_pool, page_table, p_prompt, p_sample, norm_mix_e, w_in_e, a_ln_g, a_ln_b, a_ws, a_bs, w_out_e, norm_ffn_e, ffn_w1, ffn_w3, ffn_w2, norm_mix_o, w_in_o, conv_w, conv_b, c_ln_g, c_ln_b, pool_w, pool_scale, w_out_o, norm_ffn_o, router_w, router_b, exp_w1, exp_w3, exp_w2, ple_w, ple_gate_w, ple_norm, final_norm):
    assert x_prompt.shape[0] == 1 and norm_mix_e.shape[0] == 1 and norm_mix_o.shape[0] == 1 and ple_w.shape[0] == 2
    n_tok, d = x_prompt.shape[1], x_prompt.shape[2]
    n_batch, n_new = x_sample.shape[0], x_sample.shape[1]
    n_s = n_batch * n_new
    a_width = a_ln_g.shape[1]
    b_width = (w_in_e.shape[2] - 2 * a_width) // 3
    n_heads = b_width // HEAD_DIM
    gdim = a_width // A_GROUPS
    n_experts = router_w.shape[2]
    page = cache_k.shape[2]
    past_len = page_table.shape[1] * page
    assert n_tok % (MOBA_BLOCK * MOBA_UNROLL) == 0 and n_tok // MOBA_BLOCK <= HEAD_DIM and past_len % MOBA_BLOCK == 0
    assert n_new <= CHUNK and CHUNK % n_new == 0 and n_s % CHUNK == 0

    bf = lambda w: w.astype(BF16)
    row2 = lambda v: v.reshape(1, -1)

    w_in = bf(w_in_e[0])
    ws_p = a_ws[0]
    bias_p = jnp.repeat(a_bs[0].T, gdim, axis=1)
    reps = CHUNK // n_new
    ws_s = jnp.tile(a_ws[0][:, :n_new, :n_new], (1, reps, reps))
    bias_s = jnp.tile(jnp.repeat(a_bs[0][:, :n_new].T, gdim, axis=1), (reps, 1))
    cos_p, sin_p = _rope_tables(jnp.arange(n_tok, dtype=I32))
    cos_s, sin_s = _rope_tables(past_len + (jnp.arange(n_s, dtype=I32) % n_new))
    hp0 = x_prompt[0]
    hs0 = x_sample.reshape(n_s, d)
    ev = dict(a_width=a_width, b_width=b_width)
    tm_p = _row_tile(n_tok, 512)
    a_p, q_p, k_p, v_p, kaug, vaug, kmean = _even_in(
        hp0, row2(norm_mix_e[0]), w_in, row2(a_ln_g[0]), row2(a_ln_b[0]), ws_p, bias_p, cos_p, sin_p,
        tm=tm_p, seq_l=CHUNK, prompt=True, **ev)
    a_s, va_s, q_s, k_s, v_s = _even_in(
        hs0, row2(norm_mix_e[0]), w_in, row2(a_ln_g[0]), row2(a_ln_b[0]), ws_s, bias_s, cos_s, sin_s,
        tm=_row_tile(n_s, 256), seq_l=n_new, prompt=False, **ev)

    n_blk = n_tok // MOBA_BLOCK
    km = kmean.reshape(n_blk, n_heads // 2, 2, HEAD_DIM)
    km_t = jnp.zeros((n_heads // 2, 2, LANES, LANES), F32)
    km_t = km_t.at[:, 0, :HEAD_DIM, HEAD_DIM:HEAD_DIM + n_blk].set(km[:, :, 0].transpose(1, 2, 0))
    km_t = km_t.at[:, 1, HEAD_DIM:, :n_blk].set(km[:, :, 1].transpose(1, 2, 0))
    km_hi = km_t.astype(BF16)
    km_lo = (km_t - km_hi.astype(F32)).astype(BF16)
    attn_p = _moba_prompt(q_p, kaug, vaug, km_hi, km_lo)
    attn_s = _moba_sample(q_s, k_s, v_s, cache_k[0], cache_v[0], page_table, n_batch)

    tail = (bf(w_out_e[0]), row2(norm_ffn_e[0]), bf(ffn_w1[0]), bf(ffn_w3[0]), bf(ffn_w2[0]))
    ple0 = (bf(ple_w[0]), bf(ple_gate_w[0]), row2(ple_norm[0]), row2(norm_mix_o[0]))
    h3_p, xn3_p = _even_tail(hp0, a_p, attn_p, *tail, p_prompt[0, 0], *ple0, tm=tm_p)
    h3_s, xn3_s = _even_tail(hs0, a_s, attn_s, *tail, p_sample[0].reshape(n_s, -1), *ple0, tm=_row_tile(n_s, 256))

    def time_major(x):
        return x.reshape(n_batch, n_new, -1).transpose(1, 0, 2).reshape(n_s, -1)

    rw = jnp.zeros((d, LANES), F32).at[:, :n_experts].set(router_w[0])
    rw_hi = rw.astype(BF16)
    rw_lo = (rw - rw_hi.astype(F32)).astype(BF16)
    rb = jnp.zeros((1, LANES), F32).at[0, :n_experts].set(router_b[0])
    common = _odd_common_args(row2(conv_b[0]), row2(c_ln_g[0]), row2(c_ln_b[0]), bf(pool_w[0]), row2(pool_scale[0]),
                              bf(w_out_o[0]), row2(norm_ffn_o[0]), rw_hi, rw_lo, rb)
    w_in1 = bf(w_in_o[0])
    zero_counts = jnp.zeros((1, LANES), F32)
    h4_p, xn5_p, meta_p, cnt_p, cstate_p, pstate_p = _odd_prompt(
        xn3_p, h3_p, w_in1, conv_w[0], common, zero_counts, tm=tm_p, n_experts=n_experts)
    h4_s, xn5_s, meta_s, cnt_all, cstate_s, pstate_s = _odd_sample(
        time_major(xn3_s), time_major(h3_s), w_in1, conv_w[0], common, cnt_p,
        state_conv[0].transpose(1, 0, 2), state_pool[0].transpose(1, 0, 2), n_batch=n_batch, n_experts=n_experts)

    counts = cnt_all[0, :n_experts].astype(I32)
    padded = (counts + MOE_TM - 1) // MOE_TM * MOE_TM
    pends = jnp.cumsum(padded)
    pstarts = (pends - padded).astype(F32)
    n_rows_all = (n_tok + n_s) * TOP_K
    n_blocks = -(-n_rows_all // MOE_TM) + n_experts
    blk_row0 = jnp.arange(n_blocks, dtype=I32) * MOE_TM
    blk_e = jnp.minimum(jnp.sum((pends[None, :] <= blk_row0[:, None]).astype(I32), axis=1), n_experts - 1)
    n_used = (pends[-1] // MOE_TM).astype(I32).reshape(1)
    last_e = blk_e[jnp.maximum(n_used[0] - 1, 0)]
    blk_e = jnp.where(jnp.arange(n_blocks) < n_used[0], blk_e, last_e)

    def dests(meta):
        e = meta[:, :TOP_K].astype(I32)
        return (pstarts[e] + meta[:, 4:4 + TOP_K]).astype(I32).reshape(-1)

    dest_p, dest_s = dests(meta_p), dests(meta_s)
    xbuf = jnp.zeros((n_blocks * MOE_TM, d // LANES, LANES), F32)
    xbuf = _moe_scatter(dest_p, xn5_p, xbuf, tm=_row_tile(n_tok, 512))
    xbuf = _moe_scatter(dest_s, xn5_s, xbuf, tm=_row_tile(n_s, 256))
    ff_e = exp_w1.shape[3]
    tf = next((c for c in (1792, 1024, 512, 256) if ff_e % c == 0), ff_e)
    ybuf = _moe_experts(blk_e, n_used, xbuf, bf(exp_w1[0]), bf(exp_w3[0]), bf(exp_w2[0]), tm=MOE_TM, tf=tf)
    ple1 = (bf(ple_w[1]), bf(ple_gate_w[1]), row2(ple_norm[1]), row2(final_norm))
    y_p = _moe_combine(dest_p, ybuf, h4_p, meta_p, p_prompt[1, 0], *ple1, tm=_row_tile(n_tok, 512))
    y_s = _moe_combine(dest_s, ybuf, h4_s, meta_s, time_major(p_sample[1].reshape(n_s, -1)), *ple1,
                       tm=_row_tile(n_s, 256))

    def batch_major(x):
        return x.reshape(n_new, n_batch, -1).transpose(1, 0, 2)

    kv = lambda x, b, l: x.reshape(1, b, l, n_heads, HEAD_DIM)
    return (y_p[None], batch_major(y_s),
            kv(k_p, 1, n_tok), kv(v_p, 1, n_tok), kv(k_s, n_batch, n_new), kv(v_s, n_batch, n_new),
            va_s.reshape(1, n_batch, n_new, a_width),
            cstate_p[None, None], cstate_s.transpose(1, 0, 2)[None],
            pstate_p[None, None], pstate_s.transpose(1, 0, 2)[None])
```
